```python
import math
import jax, jax.numpy as jnp
from jax import lax
import numpy as np

D_MODEL = 2048
BATCH = 4
SEQ = 2048
DEPTH = 2
DEC_BATCH = 32
DEC_SEQ = 4
PAST_LEN = 8192
PAGE_SIZE = 128

N_MIXERS = 2
N_ATTN_LAYERS = (DEPTH + 1) // 2
N_HGRN_LAYERS = DEPTH // 2
A_HEADS = 12
A_DK = 64
A_DV = 2 * A_DK
ROPE_THETA = 10000.0
Q_BLOCK = 128
B_HEADS = 12
B_DK = 128
B_DV = 128
B_CHUNK = 64
MEM_TOKENS = 256
MEM_HEADS = 4
MEM_HD = 128
D_FF = 5632
EPS = 1e-6

A_QK_W = A_HEADS * 2 * A_DK
A_WIDTH = A_HEADS * A_DV
B_KW = B_HEADS * B_DK
B_WIDTH = B_HEADS * B_DV
MEM_WIDTH = MEM_HEADS * MEM_HD
MIX_WIDTH = A_WIDTH + MEM_WIDTH
A_IN = 2 * A_QK_W + A_WIDTH + MEM_WIDTH
B_IN = 2 * B_KW + 2 * B_WIDTH + MEM_WIDTH

kernel_name = 'hybrid_diffattn_hgrn2_macaron_step'


def rmsnorm(x, g):
    xf = x.astype(jnp.float32)
    y = xf * lax.rsqrt(jnp.mean(xf * xf, axis=-1, keepdims=True) + EPS)
    return (y * g.astype(jnp.float32)).astype(x.dtype)


def rope(x, pos):
    half = x.shape[-1] // 2
    inv_freq = ROPE_THETA ** (-jnp.arange(half, dtype=jnp.float32) / half)
    ang = pos.astype(jnp.float32)[:, None] * inv_freq[None, :]
    shape = (1, pos.shape[0]) + (1,) * (x.ndim - 3) + (half,)
    cos = jnp.cos(ang).reshape(shape)
    sin = jnp.sin(ang).reshape(shape)
    xf = x.astype(jnp.float32)
    x1, x2 = xf[..., :half], xf[..., half:]
    return jnp.concatenate([x1 * cos - x2 * sin, x2 * cos + x1 * sin], axis=-1).astype(x.dtype)


def swiglu_half(x, g, wg, wu, wd):
    h = rmsnorm(x, g)
    return 0.5 * ((jax.nn.silu(h @ wg) * (h @ wu)) @ wd)


def memory_kv(mem, g, w, gk):
    bsz, m = mem.shape[:2]
    kv = rmsnorm(mem, g) @ w
    k = rmsnorm(kv[..., :MEM_WIDTH].reshape(bsz, m, MEM_HEADS, MEM_HD), gk)
    v = kv[..., MEM_WIDTH:].reshape(bsz, m, MEM_HEADS, MEM_HD)
    return k, v


def memory_attend(mq, mk, mv, gq):
    bsz, L = mq.shape[:2]
    q = rmsnorm(mq.reshape(bsz, L, MEM_HEADS, MEM_HD), gq)
    s = jnp.einsum('bqhd,bmhd->bhqm', q, mk).astype(jnp.float32) * MEM_HD ** -0.5
    p = jax.nn.softmax(s, axis=-1).astype(mv.dtype)
    return jnp.einsum('bhqm,bmhd->bqhd', p, mv).reshape(bsz, L, MEM_WIDTH)


def diff_project(h, pos, w_in, gq, gk):
    bsz, L = h.shape[:2]
    proj = h @ w_in
    q = proj[..., :A_QK_W].reshape(bsz, L, A_HEADS, 2, A_DK)
    k = proj[..., A_QK_W:2 * A_QK_W].reshape(bsz, L, A_HEADS, 2, A_DK)
    v = proj[..., 2 * A_QK_W:2 * A_QK_W + A_WIDTH].reshape(bsz, L, A_HEADS, A_DV)
    mq = proj[..., 2 * A_QK_W + A_WIDTH:]
    q = rope(rmsnorm(q, gq), pos)
    k = rope(rmsnorm(k, gk), pos)
    return q, k, v, mq


def diff_lambda(lq1, lk1, lq2, lk2, lam_init):
    f32 = jnp.float32
    return (jnp.exp(jnp.sum(lq1.astype(f32) * lk1.astype(f32)))
            - jnp.exp(jnp.sum(lq2.astype(f32) * lk2.astype(f32))) + lam_init)


def diff_attend(q, q_pos, lam, segments):
    scores = []
    for k, _, k_pos in segments:
        s = jnp.einsum('bqhcd,bkhcd->bhcqk', q, k).astype(jnp.float32) * A_DK ** -0.5
        mask = k_pos[None, :] <= q_pos[:, None]
        scores.append(jnp.where(mask, s, -jnp.inf))
    p = jax.nn.softmax(jnp.concatenate(scores, axis=-1), axis=-1)
    a = p[:, :, 0] - lam * p[:, :, 1]
    outs = []
    start = 0
    for k, v, _ in segments:
        n = k.shape[1]
        outs.append(jnp.einsum('bhqk,bkhv->bqhv', a[..., start:start + n].astype(v.dtype), v))
        start += n
    return sum(outs)


def diff_attn_prompt(q, k, v, lam):
    bsz, L = q.shape[:2]
    k_pos = jnp.arange(L)

    def block(i):
        start = i * Q_BLOCK
        qb = lax.dynamic_slice_in_dim(q, start, Q_BLOCK, axis=1)
        return diff_attend(qb, start + jnp.arange(Q_BLOCK), lam, ((k, v, k_pos),))

    o = lax.map(block, jnp.arange(L // Q_BLOCK))
    return jnp.moveaxis(o, 0, 1).reshape(bsz, L, A_HEADS, A_DV)


def diff_output(o, g_sub, lam_init):
    bsz, L = o.shape[:2]
    return (rmsnorm(o, g_sub) * (1.0 - lam_init)).reshape(bsz, L, A_WIDTH)


def hgrn_project(h, w_in, lb):
    bsz, L = h.shape[:2]
    proj = h @ w_in
    q = proj[..., :B_KW].reshape(bsz, L, B_HEADS, B_DK)
    fpre = proj[..., B_KW:2 * B_KW].reshape(bsz, L, B_HEADS, B_DK)
    i_in = proj[..., 2 * B_KW:2 * B_KW + B_WIDTH].reshape(bsz, L, B_HEADS, B_DV)
    gate = proj[..., 2 * B_KW + B_WIDTH:2 * B_KW + 2 * B_WIDTH].reshape(bsz, L, B_HEADS, B_DV)
    mq = proj[..., 2 * B_KW + 2 * B_WIDTH:]
    lbf = lb.astype(jnp.float32).reshape(B_HEADS, B_DK)
    f = lbf + (1.0 - lbf) * jax.nn.sigmoid(fpre.astype(jnp.float32))
    return q, (1.0 - f).astype(h.dtype), i_in, jnp.log(f), gate, mq


def hgrn2_chunked(q, k, v, logf, s0):
    bsz, L = q.shape[:2]
    C = math.gcd(L, B_CHUNK)
    n = L // C

    def chunks(t):
        return jnp.moveaxis(t.reshape((bsz, n, C) + t.shape[2:]), 1, 0)

    causal = (jnp.arange(C)[:, None] >= jnp.arange(C)[None, :])[None, :, :, None, None]

    def step(S, inp):
        qc, kc, vc, lfc = [t.astype(jnp.float32) for t in inp]
        b = jnp.cumsum(lfc, axis=1)
        o_inter = jnp.einsum('bthk,bhkv->bthv', qc * jnp.exp(b), S)
        decay = jnp.exp(jnp.where(causal, b[:, :, None] - b[:, None, :], -jnp.inf))
        att = jnp.einsum('bthk,bshk,btshk->bhts', qc, kc, decay)
        o_intra = jnp.einsum('bhts,bshv->bthv', att, vc)
        b_last = b[:, -1]
        S_new = (jnp.exp(b_last)[..., None] * S
                 + jnp.einsum('bshk,bshv->bhkv', kc * jnp.exp(b_last[:, None] - b), vc))
        return S_new, o_inter + o_intra

    S, o = lax.scan(step, s0.astype(jnp.float32), (chunks(q), chunks(k), chunks(v), chunks(logf)))
    o = jnp.moveaxis(o, 0, 1).reshape((bsz, L) + v.shape[2:])
    return o.astype(v.dtype), S


def hgrn_output(o, gate, g):
    bsz, L = o.shape[:2]
    return (rmsnorm(o, g) * jax.nn.silu(gate)).reshape(bsz, L, B_WIDTH)


def setup_inputs(seed: int = 0) -> dict:
    key = jax.random.key(seed)
    keys = iter(jax.random.split(key, 48))
    f32 = jnp.float32

    def w(shape, fan_in):
        return jax.random.normal(next(keys), shape, f32) * fan_in ** -0.5

    def gain(shape):
        return 1.0 + 0.02 * jax.random.normal(next(keys), shape, f32)

    def rnd(shape, scale=1.0):
        return scale * jax.random.normal(next(keys), shape, f32)

    n_pages = PAST_LEN // PAGE_SIZE
    n_used = DEC_BATCH * n_pages
    n_phys = n_used + max(1, n_used // 4)
    page_table = jax.random.permutation(next(keys), n_phys)[:n_used].reshape(DEC_BATCH, n_pages).astype(jnp.int32)

    return {
        'x_prompt': rnd((BATCH, SEQ, D_MODEL)),
        'x_sample': rnd((DEC_BATCH, DEC_SEQ, D_MODEL)),
        'cache_attn_k': rnd((N_ATTN_LAYERS, n_phys, PAGE_SIZE, A_HEADS, 2 * A_DK)),
        'cache_attn_v': rnd((N_ATTN_LAYERS, n_phys, PAGE_SIZE, A_HEADS, A_DV)),
        'state_hgrn': rnd((N_HGRN_LAYERS, DEC_BATCH, B_HEADS, B_DK, B_DV), 0.1),
        'cache_mem_k': rnd((DEPTH, DEC_BATCH, MEM_TOKENS, MEM_HEADS, MEM_HD)),
        'cache_mem_v': rnd((DEPTH, DEC_BATCH, MEM_TOKENS, MEM_HEADS, MEM_HD)),
        'page_table': page_table,
        'mem_prompt': rnd((BATCH, MEM_TOKENS, D_MODEL)),
        'norm_ffn': gain((DEPTH, 2, D_MODEL)),
        'w_ffn_gate': w((DEPTH, 2, D_MODEL, D_FF), D_MODEL),
        'w_ffn_up': w((DEPTH, 2, D_MODEL, D_FF), D_MODEL),
        'w_ffn_down': w((DEPTH, 2, D_FF, D_MODEL), D_FF),
        'norm_mix': gain((DEPTH, D_MODEL)),
        'norm_mem': gain((DEPTH, D_MODEL)),
        'w_mem_kv': w((DEPTH, D_MODEL, 2 * MEM_WIDTH), D_MODEL),
        'gq_mem': gain((DEPTH, MEM_HD)),
        'gk_mem': gain((DEPTH, MEM_HD)),
        'w_out': w((DEPTH, MIX_WIDTH, D_MODEL), MIX_WIDTH),
        'w_in_attn': w((N_ATTN_LAYERS, D_MODEL, A_IN), D_MODEL),
        'gq_attn': gain((N_ATTN_LAYERS, A_DK)),
        'gk_attn': gain((N_ATTN_LAYERS, A_DK)),
        'lam_q1': rnd((N_ATTN_LAYERS, A_DK), 0.1),
        'lam_k1': rnd((N_ATTN_LAYERS, A_DK), 0.1),
        'lam_q2': rnd((N_ATTN_LAYERS, A_DK), 0.1),
        'lam_k2': rnd((N_ATTN_LAYERS, A_DK), 0.1),
        'g_subln': gain((N_ATTN_LAYERS, A_DV)),
        'w_in_hgrn': w((N_HGRN_LAYERS, D_MODEL, B_IN), D_MODEL),
        'lb_logits': rnd((DEPTH, B_KW), 0.5),
        'g_hgrn_out': gain((N_HGRN_LAYERS, B_DV)),
    }


def reference(x_prompt, x_sample, cache_attn_k, cache_attn_v, state_hgrn, cache_mem_k, cache_mem_v,
              page_table, mem_prompt, norm_ffn, w_ffn_gate, w_ffn_up, w_ffn_down, norm_mix, norm_mem,
              w_mem_kv, gq_mem, gk_mem, w_out, w_in_attn, gq_attn, gk_attn, lam_q1, lam_k1, lam_q2,
              lam_k2, g_subln, w_in_hgrn, lb_logits, g_hgrn_out):
    bp, lp = x_prompt.shape[:2]
    bs, ls = x_sample.shape[:2]
    past_len = page_table.shape[1] * PAGE_SIZE
    pos_p = jnp.arange(lp)
    pos_s = past_len + jnp.arange(ls)
    pos_past = jnp.arange(past_len)

    lb_sm = jax.nn.softmax(lb_logits.astype(jnp.float32), axis=0)
    lower_bounds = jnp.cumsum(lb_sm, axis=0) - lb_sm[0]

    k_rows_p, v_rows_p, k_rows_s, v_rows_s = [], [], [], []
    st_p, st_s, mem_k_new, mem_v_new = [], [], [], []
    xp, xs = x_prompt, x_sample
    for i in range(DEPTH):
        xp = xp + swiglu_half(xp, norm_ffn[i, 0], w_ffn_gate[i, 0], w_ffn_up[i, 0], w_ffn_down[i, 0])
        xs = xs + swiglu_half(xs, norm_ffn[i, 0], w_ffn_gate[i, 0], w_ffn_up[i, 0], w_ffn_down[i, 0])
        mk_p, mv_p = memory_kv(mem_prompt, norm_mem[i], w_mem_kv[i], gk_mem[i])
        mem_k_new.append(mk_p)
        mem_v_new.append(mv_p)
        hp = rmsnorm(xp, norm_mix[i])
        hs = rmsnorm(xs, norm_mix[i])
        if i % N_MIXERS == 0:
            a = i // N_MIXERS
            lam_init = 0.8 - 0.6 * math.exp(-0.3 * i)
            lam = diff_lambda(lam_q1[a], lam_k1[a], lam_q2[a], lam_k2[a], lam_init)
            q, k, v, mq_p = diff_project(hp, pos_p, w_in_attn[a], gq_attn[a], gk_attn[a])
            o_p = diff_output(diff_attn_prompt(q, k, v, lam), g_subln[a], lam_init)
            k_rows_p.append(k.reshape(bp, lp, A_HEADS, 2 * A_DK))
            v_rows_p.append(v)
            q, k, v, mq_s = diff_project(hs, pos_s, w_in_attn[a], gq_attn[a], gk_attn[a])
            past_k = cache_attn_k[a, page_table].reshape(bs, past_len, A_HEADS, 2, A_DK)
            past_v = cache_attn_v[a, page_table].reshape(bs, past_len, A_HEADS, A_DV)
            o_s = diff_attend(q, pos_s, lam, ((past_k, past_v, pos_past), (k, v, pos_s)))
            o_s = diff_output(o_s, g_subln[a], lam_init)
            k_rows_s.append(k.reshape(bs, ls, A_HEADS, 2 * A_DK))
            v_rows_s.append(v)
        else:
            j = i // N_MIXERS
            q, kk, vv, logf, gate, mq_p = hgrn_project(hp, w_in_hgrn[j], lower_bounds[i])
            s0 = jnp.zeros((bp, B_HEADS, B_DK, B_DV), jnp.float32)
            o, S = hgrn2_chunked(q, kk, vv, logf, s0)
            o_p = hgrn_output(o, gate, g_hgrn_out[j])
            st_p.append(S.astype(state_hgrn.dtype))
            q, kk, vv, logf, gate, mq_s = hgrn_project(hs, w_in_hgrn[j], lower_bounds[i])
            o, S = hgrn2_chunked(q, kk, vv, logf, state_hgrn[j])
            o_s = hgrn_output(o, gate, g_hgrn_out[j])
            st_s.append(S.astype(state_hgrn.dtype))
        mix_p = jnp.concatenate([o_p, memory_attend(mq_p, mk_p, mv_p, gq_mem[i])], axis=-1)
        mix_s = jnp.concatenate([o_s, memory_attend(mq_s, cache_mem_k[i], cache_mem_v[i], gq_mem[i])], axis=-1)
        xp = xp + mix_p @ w_out[i]
        xs = xs + mix_s @ w_out[i]
        xp = xp + swiglu_half(xp, norm_ffn[i, 1], w_ffn_gate[i, 1], w_ffn_up[i, 1], w_ffn_down[i, 1])
        xs = xs + swiglu_half(xs, norm_ffn[i, 1], w_ffn_gate[i, 1], w_ffn_up[i, 1], w_ffn_down[i, 1])

    attn_k_prompt = jnp.stack(k_rows_p)
    attn_v_prompt = jnp.stack(v_rows_p)
    attn_k_sample = jnp.stack(k_rows_s)
    attn_v_sample = jnp.stack(v_rows_s)
    hgrn_state_prompt = jnp.stack(st_p)
    hgrn_state_sample = jnp.stack(st_s)
    mem_k_prompt = jnp.stack(mem_k_new)
    mem_v_prompt = jnp.stack(mem_v_new)
    return (xp, xs, attn_k_prompt, attn_v_prompt, attn_k_sample, attn_v_sample,
            hgrn_state_prompt, hgrn_state_sample, mem_k_prompt, mem_v_prompt)
```

```python
import functools
import math

import jax
import jax.numpy as jnp
from jax import lax
from jax.experimental import pallas as pl
from jax.experimental.pallas import tpu as pltpu

F32 = jnp.float32
BF16 = jnp.bfloat16

EPS = 1e-6
ROPE_THETA = 10000.0
A_HEADS = 12
A_DK = 64
A_DV = 128
B_HEADS = 12
B_DK = 128
B_DV = 128
B_CHUNK = 64
B_SUB = 16
MEM_HEADS = 4
MEM_HD = 128
PAGE_SIZE = 128
LANES = 128
SUBLANES_BF16 = 16
V7X_VMEM_LIMIT_BYTES = 56 * 1024 * 1024
ROW_TILE_TARGET = 832
DEC_PAGES_PER_STEP = 4
NT_DIMS = (((1,), (1,)), ((), ()))
TN_DIMS = (((0,), (0,)), ((), ()))


def _params(*semantics):
    return pltpu.CompilerParams(dimension_semantics=semantics,
                                vmem_limit_bytes=V7X_VMEM_LIMIT_BYTES)


def _pick_tile(n, target, align):
    best = None
    for t in range(align, min(n, target) + 1, align):
        if n % t == 0:
            best = t
    assert best is not None, (n, target, align)
    return best


def _rmsnorm_lanes(x, g):
    ms = jnp.mean(x * x, axis=-1, keepdims=True)
    return x * lax.rsqrt(ms + EPS) * g


def _norm_rows_into(h_ref, x_ref, g_ref, copy_ref=None):
    rows = x_ref.shape[0]
    chunk = SUBLANES_BF16 if rows % SUBLANES_BF16 == 0 else 8
    g = g_ref[...]

    def body(i, carry):
        r = pl.multiple_of(i * chunk, chunk)
        x = x_ref[pl.ds(r, chunk), :]
        h_ref[pl.ds(r, chunk), :] = _rmsnorm_lanes(x, g).astype(h_ref.dtype)
        if copy_ref is not None:
            copy_ref[pl.ds(r, chunk), :] = x
        return carry

    lax.fori_loop(0, rows // chunk, body, 0)


def _ffn_kernel(x_ref, g_ref, wg_ref, wu_ref, wd_ref, o_ref, h_ref, *, n_out_chunks):
    @pl.when(pl.program_id(1) == 0)
    def _():
        _norm_rows_into(h_ref, x_ref, g_ref, copy_ref=o_ref)

    h = h_ref[...]
    gate = jnp.dot(h, wg_ref[...].astype(BF16), preferred_element_type=F32)
    up = jnp.dot(h, wu_ref[...].astype(BF16), preferred_element_type=F32)
    act = (0.5 * (gate * jax.nn.sigmoid(gate)) * up).astype(BF16)
    wd = wd_ref[...].astype(BF16)
    width = o_ref.shape[1] // n_out_chunks
    for c in range(n_out_chunks):
        cols = slice(c * width, (c + 1) * width)
        o_ref[:, cols] += jnp.dot(act, wd[:, cols], preferred_element_type=F32)


def _ffn_half(x, norm_ffn, w_gate, w_up, w_down, layer, half):
    m, d = x.shape
    f = w_gate.shape[-1]
    tm = _pick_tile(m, ROW_TILE_TARGET, SUBLANES_BF16)
    tf = _pick_tile(f, 256, LANES)
    n_out_chunks = max(1, d // 512)
    g4 = norm_ffn.reshape(norm_ffn.shape[0], 2, 1, d)
    return pl.pallas_call(
        functools.partial(_ffn_kernel, n_out_chunks=n_out_chunks),
        grid=(m // tm, f // tf),
        in_specs=[
            pl.BlockSpec((tm, d), lambda i, j: (i, 0), pipeline_mode=pl.Buffered(1)),
            pl.BlockSpec((None, None, 1, d), lambda i, j: (layer, half, 0, 0)),
            pl.BlockSpec((None, None, d, tf), lambda i, j: (layer, half, 0, j)),
            pl.BlockSpec((None, None, d, tf), lambda i, j: (layer, half, 0, j)),
            pl.BlockSpec((None, None, tf, d), lambda i, j: (layer, half, j, 0)),
        ],
        out_specs=pl.BlockSpec((tm, d), lambda i, j: (i, 0)),
        out_shape=jax.ShapeDtypeStruct((m, d), F32),
        scratch_shapes=[pltpu.VMEM((tm, d), BF16)],
        compiler_params=_params("parallel", "arbitrary"),
        name="ffn_half",
    )(x, g4, w_gate, w_up, w_down)


def _norm_matmul_kernel(x_ref, g_ref, w_ref, o_ref, h_ref):
    @pl.when(pl.program_id(1) == 0)
    def _():
        _norm_rows_into(h_ref, x_ref, g_ref)

    o_ref[...] = jnp.dot(h_ref[...], w_ref[...].astype(BF16), preferred_element_type=F32)


def _norm_matmul(x, gains, g_idx, w, w_idx):
    m, d = x.shape
    n = w.shape[-1]
    tm = _pick_tile(m, ROW_TILE_TARGET, SUBLANES_BF16)
    tn = _pick_tile(n, 512, LANES)
    g3 = gains.reshape(gains.shape[0], 1, d)
    return pl.pallas_call(
        _norm_matmul_kernel,
        grid=(m // tm, n // tn),
        in_specs=[
            pl.BlockSpec((tm, d), lambda i, j: (i, 0), pipeline_mode=pl.Buffered(1)),
            pl.BlockSpec((None, 1, d), lambda i, j: (g_idx, 0, 0)),
            pl.BlockSpec((None, d, tn), lambda i, j: (w_idx, 0, j)),
        ],
        out_specs=pl.BlockSpec((tm, tn), lambda i, j: (i, j)),
        out_shape=jax.ShapeDtypeStruct((m, n), F32),
        scratch_shapes=[pltpu.VMEM((tm, d), BF16)],
        compiler_params=_params("parallel", "arbitrary"),
        name="norm_matmul",
    )(x, g3, w)


def _out_proj_kernel(x_ref, o_ref, m_ref, wa_ref, wb_ref, y_ref):
    acc = jnp.dot(o_ref[...], wa_ref[...].astype(BF16), preferred_element_type=F32)
    acc += jnp.dot(m_ref[...], wb_ref[...].astype(BF16), preferred_element_type=F32)
    y_ref[...] = x_ref[...] + acc


def _out_proj(x, o, mem, w_out, layer):
    m, d = x.shape
    wo, wm = o.shape[1], mem.shape[1]
    assert wo % wm == 0 and wo + wm == w_out.shape[1]
    tm = _pick_tile(m, ROW_TILE_TARGET, SUBLANES_BF16)
    tn = _pick_tile(d, 512, LANES)
    return pl.pallas_call(
        _out_proj_kernel,
        grid=(m // tm, d // tn),
        in_specs=[
            pl.BlockSpec((tm, tn), lambda i, j: (i, j)),
            pl.BlockSpec((tm, wo), lambda i, j: (i, 0)),
            pl.BlockSpec((tm, wm), lambda i, j: (i, 0)),
            pl.BlockSpec((None, wo, tn), lambda i, j: (layer, 0, j)),
            pl.BlockSpec((None, wm, tn), lambda i, j: (layer, wo // wm, j)),
        ],
        out_specs=pl.BlockSpec((tm, tn), lambda i, j: (i, j)),
        out_shape=jax.ShapeDtypeStruct((m, d), F32),
        compiler_params=_params("parallel", "arbitrary"),
        name="out_proj",
    )(x, o, mem, w_out, w_out)


def _group_mean(xsq, gm):
    hi = xsq.astype(BF16)
    lo = (xsq - hi.astype(F32)).astype(BF16)
    return (jnp.dot(hi, gm, preferred_element_type=F32)
            + jnp.dot(lo, gm, preferred_element_type=F32))


def _qk_prep_kernel(p_ref, cos_ref, sin_ref, g_ref, gm_ref, o_ref):
    x = p_ref[...]
    y = x * lax.rsqrt(_group_mean(x * x, gm_ref[...]) + EPS) * g_ref[...]
    lane = lax.broadcasted_iota(jnp.int32, y.shape, 1)
    lower_half = (lane % A_DK) < (A_DK // 2)
    partner = jnp.where(lower_half,
                        pltpu.roll(y, LANES - A_DK // 2, 1),
                        pltpu.roll(y, A_DK // 2, 1))
    o_ref[...] = y * cos_ref[...] + partner * sin_ref[...]


def _qk_prep(proj, cos, sin_signed, gq, gk):
    m = proj.shape[0]
    n_blocks = 2 * A_HEADS
    tm = _pick_tile(m, ROW_TILE_TARGET, 8)
    gains = jnp.stack([jnp.tile(gq, LANES // A_DK), jnp.tile(gk, LANES // A_DK)]).reshape(2, 1, LANES)
    lane = jnp.arange(LANES)
    gm = jnp.where((lane[:, None] // A_DK) == (lane[None, :] // A_DK), 1.0 / A_DK, 0.0).astype(BF16)
    return pl.pallas_call(
        _qk_prep_kernel,
        grid=(m // tm, n_blocks),
        in_specs=[
            pl.BlockSpec((tm, LANES), lambda i, j: (i, j)),
            pl.BlockSpec((tm, LANES), lambda i, j: (i, 0)),
            pl.BlockSpec((tm, LANES), lambda i, j: (i, 0)),
            pl.BlockSpec((None, 1, LANES), lambda i, j: (j // A_HEADS, 0, 0)),
            pl.BlockSpec((LANES, LANES), lambda i, j: (0, 0)),
        ],
        out_specs=pl.BlockSpec((tm, LANES), lambda i, j: (i, j)),
        out_shape=jax.ShapeDtypeStruct((m, n_blocks * LANES), F32),
        compiler_params=_params("parallel", "arbitrary"),
        name="qk_prep",
    )(proj, cos, sin_signed, gains, gm)


def _diff_lambda(lamv, lam_init):
    t1 = jnp.sum(lamv[0:1] * lamv[1:2], axis=-1, keepdims=True)
    t2 = jnp.sum(lamv[2:3] * lamv[3:4], axis=-1, keepdims=True)
    return jnp.exp(t1) - jnp.exp(t2) + lam_init


def _diff_attn_kernel(lamv_ref, q_ref, k_ref, v_ref, g_ref, o_ref, kb_ref, vb_ref, *, lam_init):
    tq = q_ref.shape[0]
    qi = pl.program_id(2)

    @pl.when(qi == 0)
    def _():
        kb_ref[...] = k_ref[...].astype(BF16)
        vb_ref[...] = v_ref[...].astype(BF16)

    q = q_ref[...] * (A_DK ** -0.5)
    lane = lax.broadcasted_iota(jnp.int32, q.shape, 1)
    qs = [jnp.where(lane < A_DK, q, 0.0).astype(BF16),
          jnp.where(lane >= A_DK, q, 0.0).astype(BF16)]
    row = lax.broadcasted_iota(jnp.int32, (tq, tq), 0)
    col = lax.broadcasted_iota(jnp.int32, (tq, tq), 1)

    def block(j, carry, diagonal):
        r = pl.multiple_of(j * tq, tq)
        kb = kb_ref[pl.ds(r, tq), :]
        vb = vb_ref[pl.ds(r, tq), :]
        new = []
        for c in range(2):
            m_prev, l_prev, acc = carry[c]
            s = lax.dot_general(qs[c], kb, NT_DIMS, preferred_element_type=F32)
            if diagonal:
                s = jnp.where(col <= row, s, -jnp.inf)
            m_new = jnp.maximum(m_prev, jnp.max(s, axis=-1, keepdims=True))
            alpha = jnp.exp(m_prev - m_new)
            p = jnp.exp(s - m_new)
            l_new = alpha * l_prev + jnp.sum(p, axis=-1, keepdims=True)
            acc = alpha * acc + jnp.dot(p.astype(BF16), vb, preferred_element_type=F32)
            new.append((m_new, l_new, acc))
        return tuple(new)

    init = tuple((jnp.full((tq, 1), -jnp.inf, F32), jnp.zeros((tq, 1), F32),
                  jnp.zeros((tq, A_DV), F32)) for _ in range(2))
    carry = lax.fori_loop(0, qi, lambda j, c: block(j, c, False), init)
    (_, l0, acc0), (_, l1, acc1) = block(qi, carry, True)

    lam = _diff_lambda(lamv_ref[...], lam_init)
    o = acc0 / l0 - lam * (acc1 / l1)
    o_ref[...] = (_rmsnorm_lanes(o, g_ref[...]) * (1.0 - lam_init)).astype(o_ref.dtype)


def _diff_attn_prompt(qk, proj, lamv, g_subln, bp, seq, lam_init):
    tq = _pick_tile(seq, 256, LANES)
    nq = seq // tq
    return pl.pallas_call(
        functools.partial(_diff_attn_kernel, lam_init=lam_init),
        grid=(bp, A_HEADS, nq),
        in_specs=[
            pl.BlockSpec(lamv.shape, lambda b, h, i: (0, 0)),
            pl.BlockSpec((tq, LANES), lambda b, h, i: (b * nq + i, h)),
            pl.BlockSpec((seq, LANES), lambda b, h, i: (b, A_HEADS + h)),
            pl.BlockSpec((seq, LANES), lambda b, h, i: (b, 2 * A_HEADS + h)),
            pl.BlockSpec((1, LANES), lambda b, h, i: (0, 0)),
        ],
        out_specs=pl.BlockSpec((tq, LANES), lambda b, h, i: (b * nq + i, h)),
        out_shape=jax.ShapeDtypeStruct((bp * seq, A_HEADS * A_DV), BF16),
        scratch_shapes=[pltpu.VMEM((seq, LANES), BF16), pltpu.VMEM((seq, LANES), BF16)],
        compiler_params=_params("parallel", "parallel", "arbitrary"),
        name="diff_attn_prompt",
    )(lamv, qk, qk, proj, g_subln.reshape(1, LANES))


def _diff_attn_decode_kernel(pt_ref, lamv_ref, q_ref, ks_ref, vs_ref, g_ref, *rest,
                             pages, lam_init, n_steps):
    del pt_ref
    k_refs = rest[:pages]
    v_refs = rest[pages:2 * pages]
    o_ref = rest[2 * pages]
    qbd_ref, m_ref, l_ref, acc_ref, kb_ref, vb_ref = rest[2 * pages + 1:]
    ls, width = q_ref.shape
    maps = 2 * A_HEADS
    rows = ls * maps
    step = pl.program_id(1)

    @pl.when(step == 0)
    def _():
        q = q_ref[...] * (A_DK ** -0.5)
        own = (lax.broadcasted_iota(jnp.int32, (maps, width), 1) // A_DK
               == lax.broadcasted_iota(jnp.int32, (maps, width), 0))
        pieces = [jnp.where(own, jnp.broadcast_to(q[i:i + 1], (maps, width)), 0.0)
                  for i in range(ls)]
        qbd_ref[...] = jnp.concatenate(pieces, axis=0).astype(BF16)
        m_ref[...] = jnp.full(m_ref.shape, -jnp.inf, F32)
        l_ref[...] = jnp.zeros(l_ref.shape, F32)
        acc_ref[...] = jnp.zeros(acc_ref.shape, F32)

    for p in range(pages):
        kb_ref[p * PAGE_SIZE:(p + 1) * PAGE_SIZE, :] = k_refs[p][...].astype(BF16)
        vb_ref[p * PAGE_SIZE:(p + 1) * PAGE_SIZE, :] = v_refs[p][...].astype(BF16)

    s = lax.dot_general(qbd_ref[...], kb_ref[...], NT_DIMS, preferred_element_type=F32)
    m_prev = m_ref[...]
    m_new = jnp.maximum(m_prev, jnp.max(s, axis=-1, keepdims=True))
    alpha = jnp.exp(m_prev - m_new)
    p_exp = jnp.exp(s - m_new)
    l_ref[...] = alpha * l_ref[...] + jnp.sum(p_exp, axis=-1, keepdims=True)
    acc_ref[...] = alpha * acc_ref[...] + jnp.dot(p_exp.astype(BF16), vb_ref[...],
                                                   preferred_element_type=F32)
    m_ref[...] = m_new

    @pl.when(step == n_steps - 1)
    def _():
        qbd = qbd_ref[...].astype(F32)
        ks = ks_ref[...].astype(BF16).astype(F32)
        vs = vs_ref[...].astype(BF16).astype(F32)
        q_of_row = lax.broadcasted_iota(jnp.int32, (rows, 1), 0) // maps
        s_new = []
        for t in range(ls):
            st = jnp.sum(qbd * ks[t:t + 1], axis=-1, keepdims=True)
            s_new.append(jnp.where(q_of_row >= t, st, -jnp.inf))
        m_old = m_ref[...]
        m_fin = functools.reduce(jnp.maximum, s_new, m_old)
        a_fin = jnp.exp(m_old - m_fin)
        l_fin = a_fin * l_ref[...]
        acc = a_fin * acc_ref[...]
        for t in range(ls):
            pt = jnp.exp(s_new[t] - m_fin)
            l_fin = l_fin + pt
            acc = acc + pt.astype(BF16).astype(F32) * vs[t:t + 1]

        lam = _diff_lambda(lamv_ref[...], lam_init)
        second = (lax.broadcasted_iota(jnp.int32, (rows, 1), 0) % 2) == 1
        weight = jnp.where(second, -lam, 1.0) / l_fin
        head_of_row = (lax.broadcasted_iota(jnp.int32, (rows, width), 0) % maps) // 2
        head_of_lane = lax.broadcasted_iota(jnp.int32, (rows, width), 1) // A_DV
        acc = jnp.where(head_of_row == head_of_lane, acc * weight, 0.0)
        g = g_ref[...]
        for i in range(ls):
            o_row = jnp.sum(acc[i * maps:(i + 1) * maps], axis=0, keepdims=True)
            for h in range(A_HEADS):
                cols = slice(h * A_DV, (h + 1) * A_DV)
                o_ref[i:i + 1, cols] = _rmsnorm_lanes(o_row[:, cols], g) * (1.0 - lam_init)


def _diff_attn_decode(q_s, k_s, v_s, cache_k, cache_v, page_table, lamv, g_subln, layer, lam_init):
    bs, ls, width = q_s.shape
    n_pages = page_table.shape[1]
    pages = _pick_tile(n_pages, DEC_PAGES_PER_STEP, 1)
    n_steps = n_pages // pages
    n_phys = cache_k.shape[1]
    ck = cache_k.reshape(cache_k.shape[0], n_phys, PAGE_SIZE, width)
    cv = cache_v.reshape(cache_v.shape[0], n_phys, PAGE_SIZE, width)
    rows = ls * 2 * A_HEADS

    def page_spec(p):
        return pl.BlockSpec((None, None, PAGE_SIZE, width),
                            lambda b, s, pt: (layer, pt[b * n_pages + s * pages + p], 0, 0))

    def per_batch():
        return pl.BlockSpec((None, ls, width), lambda b, s, pt: (b, 0, 0))

    grid_spec = pltpu.PrefetchScalarGridSpec(
        num_scalar_prefetch=1,
        grid=(bs, n_steps),
        in_specs=[pl.BlockSpec(lamv.shape, lambda b, s, pt: (0, 0)),
                  per_batch(), per_batch(), per_batch(),
                  pl.BlockSpec((1, LANES), lambda b, s, pt: (0, 0))]
                 + [page_spec(p) for p in range(pages)]
                 + [page_spec(p) for p in range(pages)],
        out_specs=per_batch(),
        scratch_shapes=[pltpu.VMEM((rows, width), BF16),
                        pltpu.VMEM((rows, 1), F32),
                        pltpu.VMEM((rows, 1), F32),
                        pltpu.VMEM((rows, width), F32),
                        pltpu.VMEM((pages * PAGE_SIZE, width), BF16),
                        pltpu.VMEM((pages * PAGE_SIZE, width), BF16)],
    )
    return pl.pallas_call(
        functools.partial(_diff_attn_decode_kernel, pages=pages, lam_init=lam_init, n_steps=n_steps),
        grid_spec=grid_spec,
        out_shape=jax.ShapeDtypeStruct((bs, ls, width), F32),
        compiler_params=_params("parallel", "arbitrary"),
        name="diff_attn_decode",
    )(page_table.reshape(-1), lamv, q_s, k_s, v_s, g_subln.reshape(1, LANES),
      *([ck] * pages), *([cv] * pages))


def _mem_attn_kernel(q_ref, k_ref, v_ref, g_ref, o_ref):
    q = _rmsnorm_lanes(q_ref[...], g_ref[...]).astype(BF16)
    s = lax.dot_general(q, k_ref[...].astype(BF16), NT_DIMS, preferred_element_type=F32)
    s = s * (MEM_HD ** -0.5)
    e = jnp.exp(s - jnp.max(s, axis=-1, keepdims=True))
    p = e / jnp.sum(e, axis=-1, keepdims=True)
    o_ref[...] = jnp.dot(p.astype(BF16), v_ref[...].astype(BF16),
                         preferred_element_type=F32).astype(o_ref.dtype)


def _mem_attn_prompt(proj, q_block0, k_norm, kv, gq, bp, seq):
    mem = k_norm.shape[0] // bp
    tq = _pick_tile(seq, 512, LANES)
    nq = seq // tq
    return pl.pallas_call(
        _mem_attn_kernel,
        grid=(bp, MEM_HEADS, nq),
        in_specs=[
            pl.BlockSpec((tq, LANES), lambda b, h, i: (b * nq + i, q_block0 + h)),
            pl.BlockSpec((mem, LANES), lambda b, h, i: (b, h)),
            pl.BlockSpec((mem, LANES), lambda b, h, i: (b, MEM_HEADS + h)),
            pl.BlockSpec((1, LANES), lambda b, h, i: (0, 0)),
        ],
        out_specs=pl.BlockSpec((tq, LANES), lambda b, h, i: (b * nq + i, h)),
        out_shape=jax.ShapeDtypeStruct((bp * seq, MEM_HEADS * MEM_HD), BF16),
        compiler_params=_params("parallel", "parallel", "arbitrary"),
        name="mem_attn_prompt",
    )(proj, k_norm, kv, gq.reshape(1, LANES))


def _mem_attn_sample(proj_s, q_block0, cache_k, cache_v, gq, layer):
    bs, ls, _ = proj_s.shape
    mem = cache_k.shape[2]
    return pl.pallas_call(
        _mem_attn_kernel,
        grid=(bs, MEM_HEADS),
        in_specs=[
            pl.BlockSpec((None, ls, LANES), lambda b, h: (b, 0, q_block0 + h)),
            pl.BlockSpec((None, None, mem, LANES), lambda b, h: (layer, b, 0, h)),
            pl.BlockSpec((None, None, mem, LANES), lambda b, h: (layer, b, 0, h)),
            pl.BlockSpec((1, LANES), lambda b, h: (0, 0)),
        ],
        out_specs=pl.BlockSpec((None, ls, LANES), lambda b, h: (b, 0, h)),
        out_shape=jax.ShapeDtypeStruct((bs, ls, MEM_HEADS * MEM_HD), F32),
        compiler_params=_params("parallel", "arbitrary"),
        name="mem_attn_sample",
    )(proj_s, cache_k, cache_v, gq.reshape(1, LANES))


def _mem_k_norm_kernel(k_ref, g_ref, o_ref):
    o_ref[...] = _rmsnorm_lanes(k_ref[...], g_ref[...])


def _mem_k_norm(kv, gk):
    rows = kv.shape[0]
    tm = _pick_tile(rows, 256, 8)
    return pl.pallas_call(
        _mem_k_norm_kernel,
        grid=(rows // tm, MEM_HEADS),
        in_specs=[pl.BlockSpec((tm, LANES), lambda i, h: (i, h)),
                  pl.BlockSpec((1, LANES), lambda i, h: (0, 0))],
        out_specs=pl.BlockSpec((tm, LANES), lambda i, h: (i, h)),
        out_shape=jax.ShapeDtypeStruct((rows, MEM_HEADS * MEM_HD), F32),
        compiler_params=_params("parallel", "arbitrary"),
        name="mem_k_norm",
    )(kv, gk.reshape(1, LANES))


def _split3(x):
    p1 = x.astype(BF16)
    r1 = x - p1.astype(F32)
    p2 = r1.astype(BF16)
    p3 = (r1 - p2.astype(F32)).astype(BF16)
    return p1, p2, p3


def _hgrn_kernel(*refs, chunk, sub, valid, layer, has_state):
    if has_state:
        (q_ref, f_ref, v_ref, gate_ref, lb_ref, g_ref, tri_ref, s0_ref,
         o_ref, s_out_ref, st_ref) = refs
    else:
        (q_ref, f_ref, v_ref, gate_ref, lb_ref, g_ref, tri_ref,
         o_ref, s_out_ref, st_ref) = refs
    tl = q_ref.shape[0]
    t = pl.program_id(2)

    @pl.when(t == 0)
    def _():
        if has_state:
            st_ref[...] = s0_ref[...].T
        else:
            st_ref[...] = jnp.zeros(st_ref.shape, F32)

    lb = lb_ref[...]
    e = jnp.exp(lb - jnp.max(lb, axis=0, keepdims=True))
    sm = e / jnp.sum(e, axis=0, keepdims=True)
    lower = jnp.zeros((1, sm.shape[1]), F32)
    for r in range(1, layer + 1):
        lower = lower + sm[r:r + 1]

    tri = tri_ref[...]
    g = g_ref[...]
    n_sub = chunk // sub
    sub_row = lax.broadcasted_iota(jnp.int32, (sub, 1), 0)
    chunk_row = lax.broadcasted_iota(jnp.int32, (chunk, 1), 0)

    def do_chunk(ci, carry):
        r0 = pl.multiple_of(ci * chunk, chunk)
        q = q_ref[pl.ds(r0, chunk), :]
        v = v_ref[pl.ds(r0, chunk), :]
        f = lower + (1.0 - lower) * jax.nn.sigmoid(f_ref[pl.ds(r0, chunk), :])
        kk = 1.0 - f
        p1, p2, p3 = _split3(jnp.log(f))
        b = (jnp.dot(tri, p1, preferred_element_type=F32)
             + jnp.dot(tri, p2, preferred_element_type=F32)
             + jnp.dot(tri, p3, preferred_element_type=F32))
        st = st_ref[...]
        o_inter = lax.dot_general((q * jnp.exp(b)).astype(BF16), st.astype(BF16), NT_DIMS,
                                  preferred_element_type=F32)
        v16 = v.astype(BF16)
        outs = []
        for i in range(n_sub):
            rs = slice(i * sub, (i + 1) * sub)
            qb, bb, kb, vb = q[rs], b[rs], kk[rs], v[rs]
            o_i = o_inter[rs]
            if i > 0:
                b_ref = b[i * sub - 1:i * sub]
                q_dec = (qb * jnp.exp(bb - b_ref)).astype(BF16)
                k_dec = (kk[:i * sub] * jnp.exp(b_ref - b[:i * sub])).astype(BF16)
                att = lax.dot_general(q_dec, k_dec, NT_DIMS, preferred_element_type=F32)
                o_i = o_i + jnp.dot(att.astype(BF16), v16[:i * sub], preferred_element_type=F32)
            for s in range(sub):
                d = jnp.where(sub_row >= s, bb - bb[s:s + 1], -jnp.inf)
                w = jnp.sum(qb * kb[s:s + 1] * jnp.exp(d), axis=-1, keepdims=True)
                o_i = o_i + w * vb[s:s + 1]
            outs.append(o_i)
        o = outs[0] if n_sub == 1 else jnp.concatenate(outs, axis=0)

        b_last = b[valid - 1:valid]
        k_dec = jnp.where(chunk_row < valid, kk * jnp.exp(b_last - b), 0.0)
        st_ref[...] = st * jnp.exp(b_last) + lax.dot_general(
            v16, k_dec.astype(BF16), TN_DIMS, preferred_element_type=F32)

        gate = gate_ref[pl.ds(r0, chunk), :]
        o_ref[pl.ds(r0, chunk), :] = (_rmsnorm_lanes(o, g)
                                      * (gate * jax.nn.sigmoid(gate))).astype(o_ref.dtype)
        return carry

    lax.fori_loop(0, tl // chunk, do_chunk, 0)

    @pl.when(t == pl.num_programs(2) - 1)
    def _():
        s_out_ref[...] = st_ref[...].T


def _tri(chunk):
    r = jnp.arange(chunk)
    return (r[:, None] >= r[None, :]).astype(BF16)


def _hgrn_prompt(proj, lb_logits, g_out, bp, seq, layer):
    chunk = math.gcd(seq, B_CHUNK)
    sub = math.gcd(chunk, B_SUB)
    tl = _pick_tile(seq, 256, chunk)
    nt = seq // tl
    h_ = B_HEADS

    def col(block0):
        return pl.BlockSpec((tl, LANES), lambda b, h, t: (b * nt + t, block0 + h))

    return pl.pallas_call(
        functools.partial(_hgrn_kernel, chunk=chunk, sub=sub, valid=chunk, layer=layer,
                          has_state=False),
        grid=(bp, h_, nt),
        in_specs=[col(0), col(h_), col(2 * h_), col(3 * h_),
                  pl.BlockSpec((lb_logits.shape[0], LANES), lambda b, h, t: (0, h)),
                  pl.BlockSpec((1, LANES), lambda b, h, t: (0, 0)),
                  pl.BlockSpec((chunk, chunk), lambda b, h, t: (0, 0))],
        out_specs=[pl.BlockSpec((tl, LANES), lambda b, h, t: (b * nt + t, h)),
                   pl.BlockSpec((None, None, B_DK, B_DV), lambda b, h, t: (b, h, 0, 0))],
        out_shape=[jax.ShapeDtypeStruct((bp * seq, h_ * B_DV), BF16),
                   jax.ShapeDtypeStruct((bp, h_, B_DK, B_DV), F32)],
        scratch_shapes=[pltpu.VMEM((B_DV, B_DK), F32)],
        compiler_params=_params("parallel", "parallel", "arbitrary"),
        name="hgrn_prompt",
    )(proj, proj, proj, proj, lb_logits, g_out.reshape(1, LANES), _tri(chunk))


def _hgrn_sample(proj_s, state, lb_logits, g_out, layer, state_layer):
    bs, ls, n = proj_s.shape
    chunk = 8
    assert ls <= chunk
    padded = jnp.pad(proj_s, ((0, 0), (0, chunk - ls), (0, 0)))
    h_ = B_HEADS

    def col(block0):
        return pl.BlockSpec((None, chunk, LANES), lambda b, h, t: (b, 0, block0 + h))

    o, s_new = pl.pallas_call(
        functools.partial(_hgrn_kernel, chunk=chunk, sub=chunk, valid=ls, layer=layer,
                          has_state=True),
        grid=(bs, h_, 1),
        in_specs=[col(0), col(h_), col(2 * h_), col(3 * h_),
                  pl.BlockSpec((lb_logits.shape[0], LANES), lambda b, h, t: (0, h)),
                  pl.BlockSpec((1, LANES), lambda b, h, t: (0, 0)),
                  pl.BlockSpec((chunk, chunk), lambda b, h, t: (0, 0)),
                  pl.BlockSpec((None, None, None, B_DK, B_DV),
                               lambda b, h, t: (state_layer, b, h, 0, 0))],
        out_specs=[pl.BlockSpec((None, chunk, LANES), lambda b, h, t: (b, 0, h)),
                   pl.BlockSpec((None, None, B_DK, B_DV), lambda b, h, t: (b, h, 0, 0))],
        out_shape=[jax.ShapeDtypeStruct((bs, chunk, h_ * B_DV), F32),
                   jax.ShapeDtypeStruct((bs, h_, B_DK, B_DV), F32)],
        scratch_shapes=[pltpu.VMEM((B_DV, B_DK), F32)],
        compiler_params=_params("parallel", "parallel", "arbitrary"),
        name="hgrn_sample",
    )(padded, padded, padded, padded, lb_logits, g_out.reshape(1, LANES), _tri(chunk), state)
    return o[:, :ls], s_new


def _rope_tables(pos):
    half = A_DK // 2
    inv_freq = ROPE_THETA ** (-jnp.arange(half, dtype=F32) / half)
    ang = pos.astype(F32)[:, None] * inv_freq[None, :]
    cos, sin = jnp.cos(ang), jnp.sin(ang)
    reps = LANES // A_DK
    return (jnp.tile(cos, (1, 2 * reps)), jnp.tile(jnp.concatenate([-sin, sin], axis=1), (1, reps)))


def kernel(x_prompt, x_sample, cache_attn_k, cache_attn_v, state_hgrn, cache_mem_k, cache_mem_v,
           page_table, mem_prompt, norm_ffn, w_ffn_gate, w_ffn_up, w_ffn_down, norm_mix, norm_mem,
           w_mem_kv, gq_mem, gk_mem, w_out, w_in_attn, gq_attn, gk_attn, lam_q1, lam_k1, lam_q2,
           lam_k2, g_subln, w_in_hgrn, lb_logits, g_hgrn_out):
    bp, seq, d = x_prompt.shape
    bs, ls, _ = x_sample.shape
    depth = norm_mix.shape[0]
    mem = mem_prompt.shape[1]
    mp = bp * seq
    past_len = page_table.shape[1] * PAGE_SIZE
    mem_w = MEM_HEADS * MEM_HD

    x = jnp.concatenate([x_prompt.reshape(mp, d), x_sample.reshape(bs * ls, d)], axis=0)
    pos = jnp.concatenate([jnp.tile(jnp.arange(seq), bp), jnp.tile(past_len + jnp.arange(ls), bs)])
    cos, sin_signed = _rope_tables(pos)
    mem_rows = mem_prompt.reshape(bp * mem, d)
    cmk = cache_mem_k.reshape(depth, bs, mem, mem_w)
    cmv = cache_mem_v.reshape(depth, bs, mem, mem_w)

    k_rows_p, v_rows_p, k_rows_s, v_rows_s = [], [], [], []
    st_p, st_s, mem_k_new, mem_v_new = [], [], [], []
    for i in range(depth):
        x = _ffn_half(x, norm_ffn, w_ffn_gate, w_ffn_up, w_ffn_down, i, 0)

        kv = _norm_matmul(mem_rows, norm_mem, i, w_mem_kv, i)
        k_norm = _mem_k_norm(kv, gk_mem[i])
        mem_k_new.append(k_norm.reshape(bp, mem, MEM_HEADS, MEM_HD))
        mem_v_new.append(kv[:, mem_w:].reshape(bp, mem, MEM_HEADS, MEM_HD))

        if i % 2 == 0:
            a = i // 2
            lam_init = 0.8 - 0.6 * math.exp(-0.3 * i)
            qk_w = A_HEADS * 2 * A_DK
            v_w = A_HEADS * A_DV
            proj = _norm_matmul(x, norm_mix, i, w_in_attn, a)
            qk = _qk_prep(proj, cos, sin_signed, gq_attn[a], gk_attn[a])
            lamv = jnp.stack([lam_q1[a], lam_k1[a], lam_q2[a], lam_k2[a]])
            o_p = _diff_attn_prompt(qk, proj, lamv, g_subln[a], bp, seq, lam_init)
            k_rows_p.append(qk[:mp, qk_w:].reshape(bp, seq, A_HEADS, 2 * A_DK))
            v_rows_p.append(proj[:mp, 2 * qk_w:2 * qk_w + v_w].reshape(bp, seq, A_HEADS, A_DV))
            qk_s = qk[mp:].reshape(bs, ls, 2 * qk_w)
            proj_s = proj[mp:].reshape(bs, ls, proj.shape[1])
            q_s, k_s = qk_s[..., :qk_w], qk_s[..., qk_w:]
            v_s = proj_s[..., 2 * qk_w:2 * qk_w + v_w]
            o_s = _diff_attn_decode(q_s, k_s, v_s, cache_attn_k, cache_attn_v, page_table, lamv,
                                    g_subln[a], a, lam_init)
            k_rows_s.append(k_s.reshape(bs, ls, A_HEADS, 2 * A_DK))
            v_rows_s.append(v_s.reshape(bs, ls, A_HEADS, A_DV))
            mq_block0 = (2 * qk_w + v_w) // LANES
        else:
            j = i // 2
            proj = _norm_matmul(x, norm_mix, i, w_in_hgrn, j)
            proj_s = proj[mp:].reshape(bs, ls, proj.shape[1])
            o_p, s_p = _hgrn_prompt(proj, lb_logits, g_hgrn_out[j], bp, seq, i)
            o_s, s_s = _hgrn_sample(proj_s, state_hgrn, lb_logits, g_hgrn_out[j], i, j)
            st_p.append(s_p)
            st_s.append(s_s)
            mq_block0 = (2 * B_HEADS * B_DK + 2 * B_HEADS * B_DV) // LANES

        m_p = _mem_attn_prompt(proj, mq_block0, k_norm, kv, gq_mem[i], bp, seq)
        m_s = _mem_attn_sample(proj_s, mq_block0, cmk, cmv, gq_mem[i], i)
        o_all = jnp.concatenate([o_p, o_s.reshape(bs * ls, -1).astype(BF16)], axis=0)
        m_all = jnp.concatenate([m_p, m_s.reshape(bs * ls, -1).astype(BF16)], axis=0)
        x = _out_proj(x, o_all, m_all, w_out, i)

        x = _ffn_half(x, norm_ffn, w_ffn_gate, w_ffn_up, w_ffn_down, i, 1)

    return (x[:mp].reshape(bp, seq, d), x[mp:].reshape(bs, ls, d),
            jnp.stack(k_rows_p), jnp.stack(v_rows_p), jnp.stack(k_rows_s), jnp.stack(v_rows_s),
            jnp.stack(st_p), jnp.stack(st_s), jnp.stack(mem_k_new), jnp.stack(mem_v_new))
```

```python
import functools
import math

import jax
import jax.numpy as jnp
from jax import lax
from jax.experimental import pallas as pl
from jax.experimental.pallas import tpu as pltpu

F32 = jnp.float32
BF16 = jnp.bfloat16

EPS = 1e-6
ROPE_THETA = 10000.0
A_HEADS = 12
A_DK = 64
A_DV = 128
B_HEADS = 12
B_DK = 128
B_DV = 128
B_CHUNK = 64
B_SUB = 8
MEM_HEADS = 4
MEM_HD = 128
PAGE_SIZE = 128
LANES = 128
SUBLANES_BF16 = 16
V7X_VMEM_LIMIT_BYTES = 56 * 1024 * 1024
ROW_TILE_TARGET = 1040
COL_TILE = 512
DEC_PAGES_PER_STEP = 4
ATTN_TILE = 512
ATTN_HEADS_PER_STEP = 2
NT_DIMS = (((1,), (1,)), ((), ()))
TN_DIMS = (((0,), (0,)), ((), ()))


def _params(*semantics):
    return pltpu.CompilerParams(dimension_semantics=semantics,
                                vmem_limit_bytes=V7X_VMEM_LIMIT_BYTES)


def _pick_tile(n, target, align):
    best = None
    for t in range(align, min(n, target) + 1, align):
        if n % t == 0:
            best = t
    assert best is not None, (n, target, align)
    return best


def _rmsnorm_lanes(x, g):
    ms = jnp.mean(x * x, axis=-1, keepdims=True)
    return x * lax.rsqrt(ms + EPS) * g


def _norm_rows_into(h_ref, x_ref, g_ref, copy_ref=None):
    rows = x_ref.shape[0]
    chunk = SUBLANES_BF16 if rows % SUBLANES_BF16 == 0 else 8
    g = g_ref[...]

    def body(i, carry):
        r = pl.multiple_of(i * chunk, chunk)
        x = x_ref[pl.ds(r, chunk), :]
        h_ref[pl.ds(r, chunk), :] = _rmsnorm_lanes(x, g).astype(h_ref.dtype)
        if copy_ref is not None:
            copy_ref[pl.ds(r, chunk), :] = x
        return carry

    lax.fori_loop(0, rows // chunk, body, 0)


def _cast_kernel(w_ref, o_ref):
    o_ref[...] = w_ref[...].astype(o_ref.dtype)


def _cast_col_blocked(w, tc):
    r, c = w.shape[-2:]
    w3 = w.reshape(-1, r, c)
    l = w3.shape[0]
    tr = _pick_tile(r, 2048, SUBLANES_BF16)
    return pl.pallas_call(
        _cast_kernel,
        grid=(l, c // tc, r // tr),
        in_specs=[pl.BlockSpec((None, tr, tc), lambda a, j, i: (a, i, j))],
        out_specs=pl.BlockSpec((None, None, tr, tc), lambda a, j, i: (a, j, i, 0)),
        out_shape=jax.ShapeDtypeStruct((l, c // tc, r, tc), BF16),
        compiler_params=_params("parallel", "parallel", "arbitrary"),
        name="cast_col_blocked",
    )(w3)


def _cast_rows(w):
    r, c = w.shape[-2:]
    w3 = w.reshape(-1, r, c)
    l = w3.shape[0]
    tr = _pick_tile(r, 512, SUBLANES_BF16)
    return pl.pallas_call(
        _cast_kernel,
        grid=(l, r // tr),
        in_specs=[pl.BlockSpec((None, tr, c), lambda a, i: (a, i, 0))],
        out_specs=pl.BlockSpec((None, tr, c), lambda a, i: (a, i, 0)),
        out_shape=jax.ShapeDtypeStruct((l, r, c), BF16),
        compiler_params=_params("parallel", "arbitrary"),
        name="cast_rows",
    )(w3)


def _ffn_kernel(x_ref, g_ref, wg_ref, wu_ref, wd_ref, o_ref, h_ref, *, n_out_chunks):
    @pl.when(pl.program_id(1) == 0)
    def _():
        _norm_rows_into(h_ref, x_ref, g_ref, copy_ref=o_ref)

    h = h_ref[...]
    gate = jnp.dot(h, wg_ref[...], preferred_element_type=F32)
    up = jnp.dot(h, wu_ref[...], preferred_element_type=F32)
    act = (0.5 * (gate * jax.nn.sigmoid(gate)) * up).astype(BF16)
    width = o_ref.shape[1] // n_out_chunks
    for c in range(n_out_chunks):
        cols = slice(c * width, (c + 1) * width)
        o_ref[:, cols] += jnp.dot(act, wd_ref[:, cols], preferred_element_type=F32)


def _ffn_half(x, norm_ffn, wg_b, wu_b, wd_b, idx):
    m, d = x.shape
    _, nf, _, tf = wg_b.shape
    tm = _pick_tile(m, ROW_TILE_TARGET, SUBLANES_BF16)
    n_out_chunks = max(1, d // 512)
    g3 = norm_ffn.reshape(-1, 1, d)
    return pl.pallas_call(
        functools.partial(_ffn_kernel, n_out_chunks=n_out_chunks),
        grid=(m // tm, nf),
        in_specs=[
            pl.BlockSpec((tm, d), lambda i, j: (i, 0), pipeline_mode=pl.Buffered(1)),
            pl.BlockSpec((None, 1, d), lambda i, j: (idx, 0, 0)),
            pl.BlockSpec((None, None, d, tf), lambda i, j: (idx, j, 0, 0)),
            pl.BlockSpec((None, None, d, tf), lambda i, j: (idx, j, 0, 0)),
            pl.BlockSpec((None, tf, d), lambda i, j: (idx, j, 0)),
        ],
        out_specs=pl.BlockSpec((tm, d), lambda i, j: (i, 0)),
        out_shape=jax.ShapeDtypeStruct((m, d), F32),
        scratch_shapes=[pltpu.VMEM((tm, d), BF16)],
        compiler_params=_params("parallel", "arbitrary"),
        name="ffn_half",
    )(x, g3, wg_b, wu_b, wd_b)


def _norm_matmul_kernel(x_ref, g_ref, w_ref, o_ref, h_ref):
    @pl.when(pl.program_id(1) == 0)
    def _():
        _norm_rows_into(h_ref, x_ref, g_ref)

    o_ref[...] = jnp.dot(h_ref[...], w_ref[...], preferred_element_type=F32)


def _norm_matmul(x, gains, g_idx, w_b, w_idx):
    m, d = x.shape
    _, nn, _, tn = w_b.shape
    tm = _pick_tile(m, ROW_TILE_TARGET, SUBLANES_BF16)
    g3 = gains.reshape(gains.shape[0], 1, d)
    return pl.pallas_call(
        _norm_matmul_kernel,
        grid=(m // tm, nn),
        in_specs=[
            pl.BlockSpec((tm, d), lambda i, j: (i, 0), pipeline_mode=pl.Buffered(1)),
            pl.BlockSpec((None, 1, d), lambda i, j: (g_idx, 0, 0)),
            pl.BlockSpec((None, None, d, tn), lambda i, j: (w_idx, j, 0, 0)),
        ],
        out_specs=pl.BlockSpec((tm, tn), lambda i, j: (i, j)),
        out_shape=jax.ShapeDtypeStruct((m, nn * tn), F32),
        scratch_shapes=[pltpu.VMEM((tm, d), BF16)],
        compiler_params=_params("parallel", "arbitrary"),
        name="norm_matmul",
    )(x, g3, w_b)


def _out_proj_kernel(x_ref, o_ref, m_ref, wa_ref, wb_ref, y_ref):
    acc = jnp.dot(o_ref[...], wa_ref[...], preferred_element_type=F32)
    acc += jnp.dot(m_ref[...], wb_ref[...], preferred_element_type=F32)
    y_ref[...] = x_ref[...] + acc


def _out_proj(x, o, mem, w_b, layer):
    m, d = x.shape
    wo, wm = o.shape[1], mem.shape[1]
    _, nn, mix, tn = w_b.shape
    assert wo % wm == 0 and wo + wm == mix
    tm = _pick_tile(m, ROW_TILE_TARGET, SUBLANES_BF16)
    return pl.pallas_call(
        _out_proj_kernel,
        grid=(m // tm, nn),
        in_specs=[
            pl.BlockSpec((tm, tn), lambda i, j: (i, j)),
            pl.BlockSpec((tm, wo), lambda i, j: (i, 0)),
            pl.BlockSpec((tm, wm), lambda i, j: (i, 0)),
            pl.BlockSpec((None, None, wo, tn), lambda i, j: (layer, j, 0, 0)),
            pl.BlockSpec((None, None, wm, tn), lambda i, j: (layer, j, wo // wm, 0)),
        ],
        out_specs=pl.BlockSpec((tm, tn), lambda i, j: (i, j)),
        out_shape=jax.ShapeDtypeStruct((m, d), F32),
        compiler_params=_params("parallel", "arbitrary"),
        name="out_proj",
    )(x, o, mem, w_b, w_b)


def _group_mean(xsq, gm):
    hi = xsq.astype(BF16)
    lo = (xsq - hi.astype(F32)).astype(BF16)
    return (jnp.dot(hi, gm, preferred_element_type=F32)
            + jnp.dot(lo, gm, preferred_element_type=F32))


def _rot_norm_kernel(p_ref, cos_ref, sin_ref, g_ref, gm_ref, o_ref, *, scale):
    x = p_ref[...]
    y = x * lax.rsqrt(_group_mean(x * x, gm_ref[...]) + EPS) * g_ref[...]
    lane = lax.broadcasted_iota(jnp.int32, y.shape, 1)
    lower_half = (lane % A_DK) < (A_DK // 2)
    partner = jnp.where(lower_half,
                        pltpu.roll(y, LANES - A_DK // 2, 1),
                        pltpu.roll(y, A_DK // 2, 1))
    o_ref[...] = ((y * cos_ref[...] + partner * sin_ref[...]) * scale).astype(o_ref.dtype)


def _group_mean_matrix():
    lane = jnp.arange(LANES)
    return jnp.where((lane[:, None] // A_DK) == (lane[None, :] // A_DK), 1.0 / A_DK, 0.0).astype(BF16)


def _rot_norm_rows(proj, block0, cos, sin_signed, gain, row0, rows, scale, dtype):
    tm = _pick_tile(rows, ROW_TILE_TARGET, SUBLANES_BF16)
    assert row0 % tm == 0
    r0 = row0 // tm
    g = jnp.tile(gain, LANES // A_DK).reshape(1, LANES)
    return pl.pallas_call(
        functools.partial(_rot_norm_kernel, scale=scale),
        grid=(rows // tm, A_HEADS),
        in_specs=[
            pl.BlockSpec((tm, LANES), lambda i, h: (r0 + i, block0 + h)),
            pl.BlockSpec((tm, LANES), lambda i, h: (r0 + i, 0)),
            pl.BlockSpec((tm, LANES), lambda i, h: (r0 + i, 0)),
            pl.BlockSpec((1, LANES), lambda i, h: (0, 0)),
            pl.BlockSpec((LANES, LANES), lambda i, h: (0, 0)),
        ],
        out_specs=pl.BlockSpec((tm, LANES), lambda i, h: (i, h)),
        out_shape=jax.ShapeDtypeStruct((rows, A_HEADS * LANES), dtype),
        compiler_params=_params("parallel", "arbitrary"),
        name="rot_norm_rows",
    )(proj, cos, sin_signed, g, _group_mean_matrix())


def _rot_norm_head_major(proj, block0, cos, sin_signed, gain, bp, seq):
    tr = _pick_tile(seq, 1024, 8)
    nt = seq // tr
    g = jnp.tile(gain, LANES // A_DK).reshape(1, LANES)
    return pl.pallas_call(
        functools.partial(_rot_norm_kernel, scale=1.0),
        grid=(bp, nt, A_HEADS),
        in_specs=[
            pl.BlockSpec((tr, LANES), lambda b, i, h: (b * nt + i, block0 + h)),
            pl.BlockSpec((tr, LANES), lambda b, i, h: (b * nt + i, 0)),
            pl.BlockSpec((tr, LANES), lambda b, i, h: (b * nt + i, 0)),
            pl.BlockSpec((1, LANES), lambda b, i, h: (0, 0)),
            pl.BlockSpec((LANES, LANES), lambda b, i, h: (0, 0)),
        ],
        out_specs=pl.BlockSpec((None, None, tr, LANES), lambda b, i, h: (b, h, i, 0)),
        out_shape=jax.ShapeDtypeStruct((bp, A_HEADS, seq, LANES), F32),
        compiler_params=_params("parallel", "parallel", "arbitrary"),
        name="rot_norm_head_major",
    )(proj, cos, sin_signed, g, _group_mean_matrix())


def _diff_lambda(lamv, lam_init):
    t1 = jnp.sum(lamv[0:1] * lamv[1:2], axis=-1, keepdims=True)
    t2 = jnp.sum(lamv[2:3] * lamv[3:4], axis=-1, keepdims=True)
    return jnp.exp(t1) - jnp.exp(t2) + lam_init


def _diff_attn_kernel(lamv_ref, q_ref, k_ref, v_ref, g_ref, o_ref, kb_ref, vt_ref, *, lam_init):
    tq = q_ref.shape[0]
    hps, n_kv = vt_ref.shape[:2]
    qi = pl.program_id(2)

    def head_cols(hh):
        return slice(hh * LANES, (hh + 1) * LANES)

    @pl.when(qi == 0)
    def _():
        for hh in range(hps):
            kb_ref[hh] = k_ref[hh].astype(BF16)
            for j in range(n_kv):
                vt_ref[hh, j] = v_ref[j * tq:(j + 1) * tq, head_cols(hh)].T.astype(BF16)

    lane = lax.broadcasted_iota(jnp.int32, (tq, LANES), 1)
    chains, qs = [], []
    for hh in range(hps):
        q = q_ref[:, head_cols(hh)]
        zero = jnp.zeros_like(q)
        for c in range(2):
            chains.append(hh)
            qs.append(jnp.where((lane < A_DK) if c == 0 else (lane >= A_DK), q, zero))
    key = lax.broadcasted_iota(jnp.int32, (tq, tq), 0)
    qry = lax.broadcasted_iota(jnp.int32, (tq, tq), 1)

    def block(j, carry, diagonal):
        r = pl.multiple_of(j * tq, tq)
        scores = [lax.dot_general(kb_ref[hh, pl.ds(r, tq), :], qc, NT_DIMS,
                                  preferred_element_type=F32)
                  for hh, qc in zip(chains, qs)]
        probs, stats = [], []
        for s, (m_prev, l_prev, _) in zip(scores, carry):
            if diagonal:
                s = jnp.where(key <= qry, s, -jnp.inf)
            m_new = jnp.maximum(m_prev, jnp.max(s, axis=0, keepdims=True))
            alpha = jnp.exp(m_prev - m_new)
            p = jnp.exp(s - m_new)
            stats.append((m_new, alpha * l_prev + jnp.sum(p, axis=0, keepdims=True), alpha))
            probs.append(p.astype(BF16))
        return tuple(
            (m_new, l_new, alpha * acc + jnp.dot(vt_ref[hh, j], p, preferred_element_type=F32))
            for hh, p, (m_new, l_new, alpha), (_, _, acc) in zip(chains, probs, stats, carry))

    init = tuple((jnp.full((1, tq), -jnp.inf, F32), jnp.zeros((1, tq), F32),
                  jnp.zeros((A_DV, tq), F32)) for _ in chains)
    carry = lax.fori_loop(0, qi, lambda j, c: block(j, c, False), init)
    carry = block(qi, carry, True)

    lam = _diff_lambda(lamv_ref[...], lam_init)
    for hh in range(hps):
        (_, l0, acc0), (_, l1, acc1) = carry[2 * hh], carry[2 * hh + 1]
        o = acc0 / l0 - lam * (acc1 / l1)
        ms = jnp.mean(o * o, axis=0, keepdims=True)
        o = o * lax.rsqrt(ms + EPS) * g_ref[...] * (1.0 - lam_init)
        o_ref[:, head_cols(hh)] = o.T.astype(o_ref.dtype)


def _diff_attn_prompt(q_all, k_hm, proj, v_block0, lamv, g_subln, bp, seq, lam_init):
    tq = _pick_tile(seq, ATTN_TILE, LANES)
    nq = seq // tq
    hps = ATTN_HEADS_PER_STEP
    assert A_HEADS % hps == 0 and v_block0 % hps == 0
    wide = hps * LANES
    return pl.pallas_call(
        functools.partial(_diff_attn_kernel, lam_init=lam_init),
        grid=(bp, A_HEADS // hps, nq),
        in_specs=[
            pl.BlockSpec(lamv.shape, lambda b, h, i: (0, 0)),
            pl.BlockSpec((tq, wide), lambda b, h, i: (b * nq + i, h)),
            pl.BlockSpec((None, hps, seq, LANES), lambda b, h, i: (b, h, 0, 0)),
            pl.BlockSpec((seq, wide), lambda b, h, i: (b, v_block0 // hps + h)),
            pl.BlockSpec((A_DV, 1), lambda b, h, i: (0, 0)),
        ],
        out_specs=pl.BlockSpec((tq, wide), lambda b, h, i: (b * nq + i, h)),
        out_shape=jax.ShapeDtypeStruct((bp * seq, A_HEADS * A_DV), BF16),
        scratch_shapes=[pltpu.VMEM((hps, seq, LANES), BF16),
                        pltpu.VMEM((hps, nq, A_DV, tq), BF16)],
        compiler_params=_params("parallel", "parallel", "arbitrary"),
        name="diff_attn_prompt",
    )(lamv, q_all, k_hm, proj, g_subln.reshape(A_DV, 1))


def _diff_attn_decode_kernel(pt_ref, lamv_ref, q_ref, ks_ref, vs_ref, g_ref, *rest,
                             pages, lam_init, n_steps):
    del pt_ref
    k_refs = rest[:pages]
    v_refs = rest[pages:2 * pages]
    o_ref = rest[2 * pages]
    qh_ref, m_ref, l_ref, acc_ref, kb_ref, vb_ref = rest[2 * pages + 1:]
    ls = q_ref.shape[0]
    rph = 2 * ls
    rows = A_HEADS * rph
    step = pl.program_id(1)

    def head_cols(h):
        return slice(h * A_DV, (h + 1) * A_DV)

    def per_head_rows(x):
        return jnp.concatenate([x[:, head_cols(h)] for h in range(A_HEADS) for _ in range(2)],
                               axis=0)

    @pl.when(step == 0)
    def _():
        q = q_ref[...].astype(F32)
        lane = lax.broadcasted_iota(jnp.int32, (ls, LANES), 1)
        for h in range(A_HEADS):
            qh = q[:, head_cols(h)]
            qh_ref[h * rph:h * rph + ls, :] = jnp.where(lane < A_DK, qh, 0.0)
            qh_ref[h * rph + ls:(h + 1) * rph, :] = jnp.where(lane >= A_DK, qh, 0.0)
        m_ref[...] = jnp.full(m_ref.shape, -jnp.inf, F32)
        l_ref[...] = jnp.zeros(l_ref.shape, F32)
        acc_ref[...] = jnp.zeros(acc_ref.shape, F32)

    for h in range(A_HEADS):
        for p in range(pages):
            tok = slice(p * PAGE_SIZE, (p + 1) * PAGE_SIZE)
            kb_ref[h, tok, :] = k_refs[p][h].astype(BF16)
            vb_ref[h, tok, :] = v_refs[p][h].astype(BF16)

    s = jnp.concatenate(
        [lax.dot_general(qh_ref[h * rph:(h + 1) * rph, :].astype(BF16), kb_ref[h], NT_DIMS,
                         preferred_element_type=F32) for h in range(A_HEADS)], axis=0)
    m_prev = m_ref[...]
    m_new = jnp.maximum(m_prev, jnp.max(s, axis=-1, keepdims=True))
    alpha = jnp.exp(m_prev - m_new)
    p_exp = jnp.exp(s - m_new)
    l_ref[...] = alpha * l_ref[...] + jnp.sum(p_exp, axis=-1, keepdims=True)
    pv = jnp.concatenate(
        [jnp.dot(p_exp[h * rph:(h + 1) * rph].astype(BF16), vb_ref[h],
                 preferred_element_type=F32) for h in range(A_HEADS)], axis=0)
    acc_ref[...] = alpha * acc_ref[...] + pv
    m_ref[...] = m_new

    @pl.when(step == n_steps - 1)
    def _():
        qf = qh_ref[...]
        ks = ks_ref[...].astype(BF16).astype(F32)
        vs = vs_ref[...].astype(BF16).astype(F32)
        row = lax.broadcasted_iota(jnp.int32, (rows, 1), 0)
        q_of_row = row % ls
        s_new = []
        for t in range(ls):
            k_t = per_head_rows(jnp.broadcast_to(ks[t:t + 1], ks.shape))
            st = jnp.sum(qf * k_t, axis=-1, keepdims=True)
            s_new.append(jnp.where(q_of_row >= t, st, -jnp.inf))
        m_old = m_ref[...]
        m_fin = functools.reduce(jnp.maximum, s_new, m_old)
        a_fin = jnp.exp(m_old - m_fin)
        l_fin = a_fin * l_ref[...]
        acc = a_fin * acc_ref[...]
        for t in range(ls):
            pt = jnp.exp(s_new[t] - m_fin)
            l_fin = l_fin + pt
            v_t = per_head_rows(jnp.broadcast_to(vs[t:t + 1], vs.shape))
            acc = acc + pt.astype(BF16).astype(F32) * v_t

        lam = _diff_lambda(lamv_ref[...], lam_init)
        second = (row % rph) >= ls
        acc = acc * (jnp.where(second, -lam, 1.0) / l_fin)
        g = g_ref[...]
        for h in range(A_HEADS):
            o = acc[h * rph:h * rph + ls] + acc[h * rph + ls:(h + 1) * rph]
            o_ref[:, head_cols(h)] = _rmsnorm_lanes(o, g) * (1.0 - lam_init)


def _diff_attn_decode(q_s, k_s, v_s, cache_k, cache_v, page_table, lamv, g_subln, layer, lam_init):
    bs, ls, width = q_s.shape
    n_pages = page_table.shape[1]
    pages = _pick_tile(n_pages, DEC_PAGES_PER_STEP, 1)
    n_steps = n_pages // pages
    rows = ls * 2 * A_HEADS
    assert (2 * ls) % 8 == 0 and cache_k.shape[2:] == (PAGE_SIZE, A_HEADS, 2 * A_DK)

    cache_k = jnp.transpose(cache_k, (0, 1, 3, 2, 4))
    cache_v = jnp.transpose(cache_v, (0, 1, 3, 2, 4))

    def page_spec(p):
        return pl.BlockSpec((None, None, A_HEADS, PAGE_SIZE, LANES),
                            lambda b, s, pt: (layer, pt[b * n_pages + s * pages + p], 0, 0, 0))

    def per_batch():
        return pl.BlockSpec((None, ls, width), lambda b, s, pt: (b, 0, 0))

    grid_spec = pltpu.PrefetchScalarGridSpec(
        num_scalar_prefetch=1,
        grid=(bs, n_steps),
        in_specs=[pl.BlockSpec(lamv.shape, lambda b, s, pt: (0, 0)),
                  per_batch(), per_batch(), per_batch(),
                  pl.BlockSpec((1, LANES), lambda b, s, pt: (0, 0))]
                 + [page_spec(p) for p in range(pages)]
                 + [page_spec(p) for p in range(pages)],
        out_specs=per_batch(),
        scratch_shapes=[pltpu.VMEM((rows, LANES), F32),
                        pltpu.VMEM((rows, 1), F32),
                        pltpu.VMEM((rows, 1), F32),
                        pltpu.VMEM((rows, A_DV), F32),
                        pltpu.VMEM((A_HEADS, pages * PAGE_SIZE, LANES), BF16),
                        pltpu.VMEM((A_HEADS, pages * PAGE_SIZE, A_DV), BF16)],
    )
    return pl.pallas_call(
        functools.partial(_diff_attn_decode_kernel, pages=pages, lam_init=lam_init, n_steps=n_steps),
        grid_spec=grid_spec,
        out_shape=jax.ShapeDtypeStruct((bs, ls, width), F32),
        compiler_params=_params("parallel", "arbitrary"),
        name="diff_attn_decode",
    )(page_table.reshape(-1), lamv, q_s, k_s, v_s, g_subln.reshape(1, LANES),
      *([cache_k] * pages), *([cache_v] * pages))


def _mem_attn_kernel(q_ref, k_ref, v_ref, g_ref, o_ref):
    q = _rmsnorm_lanes(q_ref[...], g_ref[...]).astype(BF16)
    s = lax.dot_general(q, k_ref[...].astype(BF16), NT_DIMS, preferred_element_type=F32)
    s = s * (MEM_HD ** -0.5)
    e = jnp.exp(s - jnp.max(s, axis=-1, keepdims=True))
    p = e / jnp.sum(e, axis=-1, keepdims=True)
    o_ref[...] = jnp.dot(p.astype(BF16), v_ref[...].astype(BF16),
                         preferred_element_type=F32).astype(o_ref.dtype)


def _mem_attn_prompt(proj, q_block0, k_norm, kv, gq, bp, seq):
    mem = k_norm.shape[0] // bp
    tq = _pick_tile(seq, 512, LANES)
    nq = seq // tq
    return pl.pallas_call(
        _mem_attn_kernel,
        grid=(bp, MEM_HEADS, nq),
        in_specs=[
            pl.BlockSpec((tq, LANES), lambda b, h, i: (b * nq + i, q_block0 + h)),
            pl.BlockSpec((mem, LANES), lambda b, h, i: (b, h)),
            pl.BlockSpec((mem, LANES), lambda b, h, i: (b, MEM_HEADS + h)),
            pl.BlockSpec((1, LANES), lambda b, h, i: (0, 0)),
        ],
        out_specs=pl.BlockSpec((tq, LANES), lambda b, h, i: (b * nq + i, h)),
        out_shape=jax.ShapeDtypeStruct((bp * seq, MEM_HEADS * MEM_HD), BF16),
        compiler_params=_params("parallel", "parallel", "arbitrary"),
        name="mem_attn_prompt",
    )(proj, k_norm, kv, gq.reshape(1, LANES))


def _mem_attn_sample(proj_s, q_block0, cache_k, cache_v, gq, layer):
    bs, ls, _ = proj_s.shape
    mem = cache_k.shape[2]
    return pl.pallas_call(
        _mem_attn_kernel,
        grid=(bs, MEM_HEADS),
        in_specs=[
            pl.BlockSpec((None, ls, LANES), lambda b, h: (b, 0, q_block0 + h)),
            pl.BlockSpec((None, None, mem, LANES), lambda b, h: (layer, b, 0, h)),
            pl.BlockSpec((None, None, mem, LANES), lambda b, h: (layer, b, 0, h)),
            pl.BlockSpec((1, LANES), lambda b, h: (0, 0)),
        ],
        out_specs=pl.BlockSpec((None, ls, LANES), lambda b, h: (b, 0, h)),
        out_shape=jax.ShapeDtypeStruct((bs, ls, MEM_HEADS * MEM_HD), F32),
        compiler_params=_params("parallel", "arbitrary"),
        name="mem_attn_sample",
    )(proj_s, cache_k, cache_v, gq.reshape(1, LANES))


def _mem_k_norm_kernel(k_ref, g_ref, o_ref):
    o_ref[...] = _rmsnorm_lanes(k_ref[...], g_ref[...])


def _mem_k_norm(kv, gk):
    rows = kv.shape[0]
    tm = _pick_tile(rows, 256, 8)
    return pl.pallas_call(
        _mem_k_norm_kernel,
        grid=(rows // tm, MEM_HEADS),
        in_specs=[pl.BlockSpec((tm, LANES), lambda i, h: (i, h)),
                  pl.BlockSpec((1, LANES), lambda i, h: (0, 0))],
        out_specs=pl.BlockSpec((tm, LANES), lambda i, h: (i, h)),
        out_shape=jax.ShapeDtypeStruct((rows, MEM_HEADS * MEM_HD), F32),
        compiler_params=_params("parallel", "arbitrary"),
        name="mem_k_norm",
    )(kv, gk.reshape(1, LANES))


def _split3(x):
    p1 = x.astype(BF16)
    r1 = x - p1.astype(F32)
    p2 = r1.astype(BF16)
    p3 = (r1 - p2.astype(F32)).astype(BF16)
    return p1, p2, p3


def _hgrn_kernel(*refs, chunk, sub, valid, layer, has_state):
    if has_state:
        (q_ref, f_ref, v_ref, gate_ref, lb_ref, g_ref, tri_ref, s0_ref,
         o_ref, s_out_ref, st_ref) = refs
    else:
        (q_ref, f_ref, v_ref, gate_ref, lb_ref, g_ref, tri_ref,
         o_ref, s_out_ref, st_ref) = refs
    tl = q_ref.shape[0]
    t = pl.program_id(2)

    @pl.when(t == 0)
    def _():
        if has_state:
            st_ref[...] = s0_ref[...].T
        else:
            st_ref[...] = jnp.zeros(st_ref.shape, F32)

    lb = lb_ref[...]
    e = jnp.exp(lb - jnp.max(lb, axis=0, keepdims=True))
    sm = e / jnp.sum(e, axis=0, keepdims=True)
    lower = jnp.zeros((1, sm.shape[1]), F32)
    for r in range(1, layer + 1):
        lower = lower + sm[r:r + 1]

    g = g_ref[...]
    n_chunks = tl // chunk
    n_sub = chunk // sub
    sub_row = lax.broadcasted_iota(jnp.int32, (sub, 1), 0)
    chunk_row = lax.broadcasted_iota(jnp.int32, (chunk, 1), 0)

    q_all = q_ref[...]
    v_all = v_ref[...]
    f_all = lower + (1.0 - lower) * jax.nn.sigmoid(f_ref[...])
    kk_all = 1.0 - f_all
    tri = tri_ref[...]
    p1, p2, p3 = _split3(jnp.log(f_all))
    b_all = (jnp.dot(tri, p1, preferred_element_type=F32)
             + jnp.dot(tri, p2, preferred_element_type=F32)
             + jnp.dot(tri, p3, preferred_element_type=F32))
    v16_all = v_all.astype(BF16)

    def chunk_rows(x, ci):
        return x[ci * chunk:(ci + 1) * chunk]

    incs, atts, b_lasts = [], [], []
    for ci in range(n_chunks):
        q, b, kk = chunk_rows(q_all, ci), chunk_rows(b_all, ci), chunk_rows(kk_all, ci)
        b_last = b[valid - 1:valid]
        k_dec = jnp.where(chunk_row < valid, kk * jnp.exp(b_last - b), 0.0)
        incs.append(lax.dot_general(chunk_rows(v16_all, ci), k_dec.astype(BF16), TN_DIMS,
                                    preferred_element_type=F32))
        b_lasts.append(b_last)
        for i in range(1, n_sub):
            rs = slice(i * sub, (i + 1) * sub)
            b_ref = b[i * sub - 1:i * sub]
            q_dec = (q[rs] * jnp.exp(b[rs] - b_ref)).astype(BF16)
            k_dec = (kk[:i * sub] * jnp.exp(b_ref - b[:i * sub])).astype(BF16)
            atts.append(lax.dot_general(q_dec, k_dec, NT_DIMS, preferred_element_type=F32))

    diag = []
    for ci in range(n_chunks):
        q, b, kk, v = (chunk_rows(x, ci) for x in (q_all, b_all, kk_all, v_all))
        for i in range(n_sub):
            rs = slice(i * sub, (i + 1) * sub)
            qb, bb, kb, vb = q[rs], b[rs], kk[rs], v[rs]
            terms = []
            for s in range(sub):
                d = jnp.where(sub_row >= s, bb - bb[s:s + 1], -jnp.inf)
                w = jnp.sum(qb * kb[s:s + 1] * jnp.exp(d), axis=-1, keepdims=True)
                terms.append(w * vb[s:s + 1])
            while len(terms) > 1:
                terms = [a + c for a, c in zip(terms[0::2], terms[1::2])]
            diag.append(terms[0])

    intra = []
    for ci in range(n_chunks):
        v16 = chunk_rows(v16_all, ci)
        for i in range(n_sub):
            o_i = diag[ci * n_sub + i]
            if i > 0:
                att = atts[ci * (n_sub - 1) + i - 1]
                o_i = o_i + jnp.dot(att.astype(BF16), v16[:i * sub], preferred_element_type=F32)
            intra.append(o_i)

    st = st_ref[...]
    for ci in range(n_chunks):
        q, b = chunk_rows(q_all, ci), chunk_rows(b_all, ci)
        o_inter = lax.dot_general((q * jnp.exp(b)).astype(BF16), st.astype(BF16), NT_DIMS,
                                  preferred_element_type=F32)
        st = st * jnp.exp(b_lasts[ci]) + incs[ci]
        parts = intra[ci * n_sub:(ci + 1) * n_sub]
        o = o_inter + (parts[0] if n_sub == 1 else jnp.concatenate(parts, axis=0))
        rows = slice(ci * chunk, (ci + 1) * chunk)
        gate = gate_ref[rows, :]
        o_ref[rows, :] = (_rmsnorm_lanes(o, g)
                          * (gate * jax.nn.sigmoid(gate))).astype(o_ref.dtype)
    st_ref[...] = st

    @pl.when(t == pl.num_programs(2) - 1)
    def _():
        s_out_ref[...] = st.T


def _tri(rows, chunk):
    r = jnp.arange(rows)
    same = (r[:, None] // chunk) == (r[None, :] // chunk)
    return (same & (r[:, None] >= r[None, :])).astype(BF16)


def _hgrn_prompt(proj, lb_logits, g_out, bp, seq, layer):
    chunk = math.gcd(seq, B_CHUNK)
    sub = math.gcd(chunk, B_SUB)
    tl = _pick_tile(seq, 256, chunk)
    nt = seq // tl
    h_ = B_HEADS

    def col(block0):
        return pl.BlockSpec((tl, LANES), lambda b, h, t: (b * nt + t, block0 + h))

    return pl.pallas_call(
        functools.partial(_hgrn_kernel, chunk=chunk, sub=sub, valid=chunk, layer=layer,
                          has_state=False),
        grid=(bp, h_, nt),
        in_specs=[col(0), col(h_), col(2 * h_), col(3 * h_),
                  pl.BlockSpec((lb_logits.shape[0], LANES), lambda b, h, t: (0, h)),
                  pl.BlockSpec((1, LANES), lambda b, h, t: (0, 0)),
                  pl.BlockSpec((tl, tl), lambda b, h, t: (0, 0))],
        out_specs=[pl.BlockSpec((tl, LANES), lambda b, h, t: (b * nt + t, h)),
                   pl.BlockSpec((None, None, B_DK, B_DV), lambda b, h, t: (b, h, 0, 0))],
        out_shape=[jax.ShapeDtypeStruct((bp * seq, h_ * B_DV), BF16),
                   jax.ShapeDtypeStruct((bp, h_, B_DK, B_DV), F32)],
        scratch_shapes=[pltpu.VMEM((B_DV, B_DK), F32)],
        compiler_params=_params("parallel", "parallel", "arbitrary"),
        name="hgrn_prompt",
    )(proj, proj, proj, proj, lb_logits, g_out.reshape(1, LANES), _tri(tl, chunk))


def _hgrn_sample(proj_s, state, lb_logits, g_out, layer, state_layer):
    bs, ls, n = proj_s.shape
    chunk = 8
    assert ls <= chunk
    padded = jnp.pad(proj_s, ((0, 0), (0, chunk - ls), (0, 0)))
    h_ = B_HEADS

    def col(block0):
        return pl.BlockSpec((None, chunk, LANES), lambda b, h, t: (b, 0, block0 + h))

    o, s_new = pl.pallas_call(
        functools.partial(_hgrn_kernel, chunk=chunk, sub=chunk, valid=ls, layer=layer,
                          has_state=True),
        grid=(bs, h_, 1),
        in_specs=[col(0), col(h_), col(2 * h_), col(3 * h_),
                  pl.BlockSpec((lb_logits.shape[0], LANES), lambda b, h, t: (0, h)),
                  pl.BlockSpec((1, LANES), lambda b, h, t: (0, 0)),
                  pl.BlockSpec((chunk, chunk), lambda b, h, t: (0, 0)),
                  pl.BlockSpec((None, None, None, B_DK, B_DV),
                               lambda b, h, t: (state_layer, b, h, 0, 0))],
        out_specs=[pl.BlockSpec((None, chunk, LANES), lambda b, h, t: (b, 0, h)),
                   pl.BlockSpec((None, None, B_DK, B_DV), lambda b, h, t: (b, h, 0, 0))],
        out_shape=[jax.ShapeDtypeStruct((bs, chunk, h_ * B_DV), F32),
                   jax.ShapeDtypeStruct((bs, h_, B_DK, B_DV), F32)],
        scratch_shapes=[pltpu.VMEM((B_DV, B_DK), F32)],
        compiler_params=_params("parallel", "parallel", "arbitrary"),
        name="hgrn_sample",
    )(padded, padded, padded, padded, lb_logits, g_out.reshape(1, LANES), _tri(chunk, chunk), state)
    return o[:, :ls], s_new


def _rope_tables(pos):
    half = A_DK // 2
    inv_freq = ROPE_THETA ** (-jnp.arange(half, dtype=F32) / half)
    ang = pos.astype(F32)[:, None] * inv_freq[None, :]
    cos, sin = jnp.cos(ang), jnp.sin(ang)
    reps = LANES // A_DK
    return (jnp.tile(cos, (1, 2 * reps)), jnp.tile(jnp.concatenate([-sin, sin], axis=1), (1, reps)))


def kernel(x_prompt, x_sample, cache_attn_k, cache_attn_v, state_hgrn, cache_mem_k, cache_mem_v,
           page_table, mem_prompt, norm_ffn, w_ffn_gate, w_ffn_up, w_ffn_down, norm_mix, norm_mem,
           w_mem_kv, gq_mem, gk_mem, w_out, w_in_attn, gq_attn, gk_attn, lam_q1, lam_k1, lam_q2,
           lam_k2, g_subln, w_in_hgrn, lb_logits, g_hgrn_out):
    bp, seq, d = x_prompt.shape
    bs, ls, _ = x_sample.shape
    depth = norm_mix.shape[0]
    mem = mem_prompt.shape[1]
    mp = bp * seq
    ms = bs * ls
    past_len = page_table.shape[1] * PAGE_SIZE
    mem_w = MEM_HEADS * MEM_HD
    qk_w = A_HEADS * 2 * A_DK
    v_w = A_HEADS * A_DV

    def col_tile(n):
        return _pick_tile(n, COL_TILE, LANES)

    wg_b = _cast_col_blocked(w_ffn_gate, col_tile(w_ffn_gate.shape[-1]))
    wu_b = _cast_col_blocked(w_ffn_up, col_tile(w_ffn_up.shape[-1]))
    wd_b = _cast_rows(w_ffn_down)
    w_in_attn_b = _cast_col_blocked(w_in_attn, col_tile(w_in_attn.shape[-1]))
    w_in_hgrn_b = _cast_col_blocked(w_in_hgrn, col_tile(w_in_hgrn.shape[-1]))
    w_mem_kv_b = _cast_col_blocked(w_mem_kv, col_tile(w_mem_kv.shape[-1]))
    w_out_b = _cast_col_blocked(w_out, col_tile(w_out.shape[-1]))
    norm_ffn2 = norm_ffn.reshape(-1, d)

    x = jnp.concatenate([x_prompt.reshape(mp, d), x_sample.reshape(ms, d)], axis=0)
    pos = jnp.concatenate([jnp.tile(jnp.arange(seq), bp), jnp.tile(past_len + jnp.arange(ls), bs)])
    cos, sin_signed = _rope_tables(pos)
    mem_rows = mem_prompt.reshape(bp * mem, d)
    cmk = cache_mem_k.reshape(depth, bs, mem, mem_w)
    cmv = cache_mem_v.reshape(depth, bs, mem, mem_w)

    k_rows_p, v_rows_p, k_rows_s, v_rows_s = [], [], [], []
    st_p, st_s, mem_k_new, mem_v_new = [], [], [], []
    for i in range(depth):
        x = _ffn_half(x, norm_ffn2, wg_b, wu_b, wd_b, 2 * i)

        kv = _norm_matmul(mem_rows, norm_mem, i, w_mem_kv_b, i)
        k_norm = _mem_k_norm(kv, gk_mem[i])
        mem_k_new.append(k_norm.reshape(bp, mem, MEM_HEADS, MEM_HD))
        mem_v_new.append(kv[:, mem_w:].reshape(bp, mem, MEM_HEADS, MEM_HD))

        if i % 2 == 0:
            a = i // 2
            lam_init = 0.8 - 0.6 * math.exp(-0.3 * i)
            proj = _norm_matmul(x, norm_mix, i, w_in_attn_b, a)
            q_all = _rot_norm_rows(proj, 0, cos, sin_signed, gq_attn[a], 0, mp + ms,
                                   A_DK ** -0.5, BF16)
            k_hm = _rot_norm_head_major(proj, A_HEADS, cos, sin_signed, gk_attn[a], bp, seq)
            k_s = _rot_norm_rows(proj, A_HEADS, cos, sin_signed, gk_attn[a], mp, ms, 1.0, F32)
            lamv = jnp.stack([lam_q1[a], lam_k1[a], lam_q2[a], lam_k2[a]])
            o_p = _diff_attn_prompt(q_all, k_hm, proj, 2 * A_HEADS, lamv, g_subln[a], bp, seq,
                                    lam_init)
            k_rows_p.append(jnp.transpose(k_hm, (0, 2, 1, 3)))
            v_rows_p.append(proj[:mp, 2 * qk_w:2 * qk_w + v_w].reshape(bp, seq, A_HEADS, A_DV))
            proj_s = proj[mp:].reshape(bs, ls, proj.shape[1])
            q_s = q_all[mp:].reshape(bs, ls, qk_w)
            k_s = k_s.reshape(bs, ls, qk_w)
            v_s = proj_s[..., 2 * qk_w:2 * qk_w + v_w]
            o_s = _diff_attn_decode(q_s, k_s, v_s, cache_attn_k, cache_attn_v, page_table, lamv,
                                    g_subln[a], a, lam_init)
            k_rows_s.append(k_s.reshape(bs, ls, A_HEADS, 2 * A_DK))
            v_rows_s.append(v_s.reshape(bs, ls, A_HEADS, A_DV))
            mq_block0 = (2 * qk_w + v_w) // LANES
        else:
            j = i // 2
            proj = _norm_matmul(x, norm_mix, i, w_in_hgrn_b, j)
            proj_s = proj[mp:].reshape(bs, ls, proj.shape[1])
            o_p, s_p = _hgrn_prompt(proj, lb_logits, g_hgrn_out[j], bp, seq, i)
            o_s, s_s = _hgrn_sample(proj_s, state_hgrn, lb_logits, g_hgrn_out[j], i, j)
            st_p.append(s_p)
            st_s.append(s_s)
            mq_block0 = (2 * B_HEADS * B_DK + 2 * B_HEADS * B_DV) // LANES

        m_p = _mem_attn_prompt(proj, mq_block0, k_norm, kv, gq_mem[i], bp, seq)
        m_s = _mem_attn_sample(proj_s, mq_block0, cmk, cmv, gq_mem[i], i)
        o_all = jnp.concatenate([o_p, o_s.reshape(ms, -1).astype(BF16)], axis=0)
        m_all = jnp.concatenate([m_p, m_s.reshape(ms, -1).astype(BF16)], axis=0)
        x = _out_proj(x, o_all, m_all, w_out_b, i)

        x = _ffn_half(x, norm_ffn2, wg_b, wu_b, wd_b, 2 * i + 1)

    return (x[:mp].reshape(bp, seq, d), x[mp:].reshape(bs, ls, d),
            jnp.stack(k_rows_p), jnp.stack(v_rows_p), jnp.stack(k_rows_s), jnp.stack(v_rows_s),
            jnp.stack(st_p), jnp.stack(st_s), jnp.stack(mem_k_new), jnp.stack(mem_v_new))
```

```python
import functools
import math

import jax
import jax.numpy as jnp
from jax import lax
from jax.experimental import pallas as pl
from jax.experimental.pallas import tpu as pltpu

F32 = jnp.float32
BF16 = jnp.bfloat16

EPS = 1e-6
ROPE_THETA = 10000.0
A_HEADS = 12
A_DK = 64
A_DV = 128
B_HEADS = 12
B_DK = 128
B_DV = 128
B_CHUNK = 64
B_SUB = 8
MEM_HEADS = 4
MEM_HD = 128
PAGE_SIZE = 128
LANES = 128
SUBLANES_BF16 = 16
V7X_VMEM_LIMIT_BYTES = 56 * 1024 * 1024
ROW_TILE_TARGET = 1040
PROJ_ROW_TILE_TARGET = 1664
COL_TILE = 512
FFN_COL_TILE = 256
NORM_ROWS_PER_ITER = 128
DEC_PAGES_PER_STEP = 8
ATTN_TILE = 512
ATTN_HEADS_PER_STEP = 2
HGRN_SAMPLE_HEADS_PER_STEP = 4
NT_DIMS = (((1,), (1,)), ((), ()))
TN_DIMS = (((0,), (0,)), ((), ()))


def _params(*semantics):
    return pltpu.CompilerParams(dimension_semantics=semantics,
                                vmem_limit_bytes=V7X_VMEM_LIMIT_BYTES)


def _pick_tile(n, target, align):
    best = None
    for t in range(align, min(n, target) + 1, align):
        if n % t == 0:
            best = t
    assert best is not None, (n, target, align)
    return best


def _rmsnorm_lanes(x, g):
    ms = jnp.mean(x * x, axis=-1, keepdims=True)
    return x * lax.rsqrt(ms + EPS) * g


def _norm_rows_into(h_ref, x_ref, g_ref, copy_ref=None):
    rows = x_ref.shape[0]
    chunk = _pick_tile(rows, NORM_ROWS_PER_ITER, SUBLANES_BF16 if rows % SUBLANES_BF16 == 0 else 8)
    g = g_ref[...]

    def body(i, carry):
        r = pl.multiple_of(i * chunk, chunk)
        x = x_ref[pl.ds(r, chunk), :]
        h_ref[pl.ds(r, chunk), :] = _rmsnorm_lanes(x, g).astype(h_ref.dtype)
        if copy_ref is not None:
            copy_ref[pl.ds(r, chunk), :] = x
        return carry

    lax.fori_loop(0, rows // chunk, body, 0)


def _ffn_kernel(x_ref, g_ref, wg_ref, wu_ref, wd_ref, o_ref, h_ref, *, n_out_chunks):
    @pl.when(pl.program_id(1) == 0)
    def _():
        _norm_rows_into(h_ref, x_ref, g_ref, copy_ref=o_ref)

    h = h_ref[...]
    gate = jnp.dot(h, wg_ref[...].astype(BF16), preferred_element_type=F32)
    up = jnp.dot(h, wu_ref[...].astype(BF16), preferred_element_type=F32)
    act = (0.5 * (gate * jax.nn.sigmoid(gate)) * up).astype(BF16)
    width = o_ref.shape[1] // n_out_chunks
    for c in range(n_out_chunks):
        cols = slice(c * width, (c + 1) * width)
        o_ref[:, cols] += jnp.dot(act, wd_ref[:, cols].astype(BF16), preferred_element_type=F32)


def _ffn_half(x, norm_ffn, w_gate, w_up, w_down, layer, half):
    m, d = x.shape
    f = w_gate.shape[-1]
    tm = _pick_tile(m, ROW_TILE_TARGET, SUBLANES_BF16)
    tf = _pick_tile(f, FFN_COL_TILE, LANES)
    n_out_chunks = max(1, d // 512)
    g4 = norm_ffn.reshape(norm_ffn.shape[0], 2, 1, d)
    return pl.pallas_call(
        functools.partial(_ffn_kernel, n_out_chunks=n_out_chunks),
        grid=(m // tm, f // tf),
        in_specs=[
            pl.BlockSpec((tm, d), lambda i, j: (i, 0), pipeline_mode=pl.Buffered(1)),
            pl.BlockSpec((None, None, 1, d), lambda i, j: (layer, half, 0, 0)),
            pl.BlockSpec((None, None, d, tf), lambda i, j: (layer, half, 0, j)),
            pl.BlockSpec((None, None, d, tf), lambda i, j: (layer, half, 0, j)),
            pl.BlockSpec((None, None, tf, d), lambda i, j: (layer, half, j, 0)),
        ],
        out_specs=pl.BlockSpec((tm, d), lambda i, j: (i, 0)),
        out_shape=jax.ShapeDtypeStruct((m, d), F32),
        scratch_shapes=[pltpu.VMEM((tm, d), BF16)],
        compiler_params=_params("parallel", "arbitrary"),
        name="ffn_half",
    )(x, g4, w_gate, w_up, w_down)


def _norm_matmul_kernel(x_ref, g_ref, w_ref, o_ref, h_ref):
    @pl.when(pl.program_id(1) == 0)
    def _():
        _norm_rows_into(h_ref, x_ref, g_ref)

    o_ref[...] = jnp.dot(h_ref[...], w_ref[...].astype(BF16), preferred_element_type=F32)


def _norm_matmul(x, gains, g_idx, w, w_idx):
    m, d = x.shape
    n = w.shape[-1]
    tm = _pick_tile(m, PROJ_ROW_TILE_TARGET, SUBLANES_BF16)
    tn = _pick_tile(n, COL_TILE, LANES)
    g3 = gains.reshape(gains.shape[0], 1, d)
    return pl.pallas_call(
        _norm_matmul_kernel,
        grid=(m // tm, n // tn),
        in_specs=[
            pl.BlockSpec((tm, d), lambda i, j: (i, 0), pipeline_mode=pl.Buffered(1)),
            pl.BlockSpec((None, 1, d), lambda i, j: (g_idx, 0, 0)),
            pl.BlockSpec((None, d, tn), lambda i, j: (w_idx, 0, j)),
        ],
        out_specs=pl.BlockSpec((tm, tn), lambda i, j: (i, j)),
        out_shape=jax.ShapeDtypeStruct((m, n), F32),
        scratch_shapes=[pltpu.VMEM((tm, d), BF16)],
        compiler_params=_params("parallel", "arbitrary"),
        name="norm_matmul",
    )(x, g3, w)


def _out_proj_kernel(x_ref, o_ref, m_ref, wa_ref, wb_ref, y_ref):
    acc = jnp.dot(o_ref[...], wa_ref[...].astype(BF16), preferred_element_type=F32)
    acc += jnp.dot(m_ref[...], wb_ref[...].astype(BF16), preferred_element_type=F32)
    y_ref[...] = x_ref[...] + acc


def _out_proj(x, o, mem, w_out, layer):
    m, d = x.shape
    wo, wm = o.shape[1], mem.shape[1]
    assert wo % wm == 0 and wo + wm == w_out.shape[1]
    tm = _pick_tile(m, PROJ_ROW_TILE_TARGET, SUBLANES_BF16)
    tn = _pick_tile(d, COL_TILE, LANES)
    return pl.pallas_call(
        _out_proj_kernel,
        grid=(m // tm, d // tn),
        in_specs=[
            pl.BlockSpec((tm, tn), lambda i, j: (i, j)),
            pl.BlockSpec((tm, wo), lambda i, j: (i, 0)),
            pl.BlockSpec((tm, wm), lambda i, j: (i, 0)),
            pl.BlockSpec((None, wo, tn), lambda i, j: (layer, 0, j)),
            pl.BlockSpec((None, wm, tn), lambda i, j: (layer, wo // wm, j)),
        ],
        out_specs=pl.BlockSpec((tm, tn), lambda i, j: (i, j)),
        out_shape=jax.ShapeDtypeStruct((m, d), F32),
        compiler_params=_params("parallel", "arbitrary"),
        name="out_proj",
    )(x, o, mem, w_out, w_out)


def _group_mean(xsq, gm):
    hi = xsq.astype(BF16)
    lo = (xsq - hi.astype(F32)).astype(BF16)
    return (jnp.dot(hi, gm, preferred_element_type=F32)
            + jnp.dot(lo, gm, preferred_element_type=F32))


def _rot_norm_kernel(p_ref, cos_ref, sin_ref, g_ref, gm_ref, o_ref, *, scale):
    x = p_ref[...]
    y = x * lax.rsqrt(_group_mean(x * x, gm_ref[...]) + EPS) * g_ref[...]
    lane = lax.broadcasted_iota(jnp.int32, y.shape, 1)
    lower_half = (lane % A_DK) < (A_DK // 2)
    partner = jnp.where(lower_half,
                        pltpu.roll(y, LANES - A_DK // 2, 1),
                        pltpu.roll(y, A_DK // 2, 1))
    o_ref[...] = ((y * cos_ref[...] + partner * sin_ref[...]) * scale).astype(o_ref.dtype)


def _group_mean_matrix():
    lane = jnp.arange(LANES)
    return jnp.where((lane[:, None] // A_DK) == (lane[None, :] // A_DK), 1.0 / A_DK, 0.0).astype(BF16)


def _rot_norm_rows(proj, block0, cos, sin_signed, gain, row0, rows, scale, dtype):
    tm = _pick_tile(rows, ROW_TILE_TARGET, SUBLANES_BF16)
    assert row0 % tm == 0
    r0 = row0 // tm
    g = jnp.tile(gain, LANES // A_DK).reshape(1, LANES)
    return pl.pallas_call(
        functools.partial(_rot_norm_kernel, scale=scale),
        grid=(rows // tm, A_HEADS),
        in_specs=[
            pl.BlockSpec((tm, LANES), lambda i, h: (r0 + i, block0 + h)),
            pl.BlockSpec((tm, LANES), lambda i, h: (r0 + i, 0)),
            pl.BlockSpec((tm, LANES), lambda i, h: (r0 + i, 0)),
            pl.BlockSpec((1, LANES), lambda i, h: (0, 0)),
            pl.BlockSpec((LANES, LANES), lambda i, h: (0, 0)),
        ],
        out_specs=pl.BlockSpec((tm, LANES), lambda i, h: (i, h)),
        out_shape=jax.ShapeDtypeStruct((rows, A_HEADS * LANES), dtype),
        compiler_params=_params("parallel", "arbitrary"),
        name="rot_norm_rows",
    )(proj, cos, sin_signed, g, _group_mean_matrix())


def _rot_norm_head_major(proj, block0, cos, sin_signed, gain, bp, seq):
    tr = _pick_tile(seq, 1024, 8)
    nt = seq // tr
    g = jnp.tile(gain, LANES // A_DK).reshape(1, LANES)
    return pl.pallas_call(
        functools.partial(_rot_norm_kernel, scale=1.0),
        grid=(bp, nt, A_HEADS),
        in_specs=[
            pl.BlockSpec((tr, LANES), lambda b, i, h: (b * nt + i, block0 + h)),
            pl.BlockSpec((tr, LANES), lambda b, i, h: (b * nt + i, 0)),
            pl.BlockSpec((tr, LANES), lambda b, i, h: (b * nt + i, 0)),
            pl.BlockSpec((1, LANES), lambda b, i, h: (0, 0)),
            pl.BlockSpec((LANES, LANES), lambda b, i, h: (0, 0)),
        ],
        out_specs=pl.BlockSpec((None, None, tr, LANES), lambda b, i, h: (b, h, i, 0)),
        out_shape=jax.ShapeDtypeStruct((bp, A_HEADS, seq, LANES), F32),
        compiler_params=_params("parallel", "parallel", "arbitrary"),
        name="rot_norm_head_major",
    )(proj, cos, sin_signed, g, _group_mean_matrix())


def _diff_lambda(lamv, lam_init):
    t1 = jnp.sum(lamv[0:1] * lamv[1:2], axis=-1, keepdims=True)
    t2 = jnp.sum(lamv[2:3] * lamv[3:4], axis=-1, keepdims=True)
    return jnp.exp(t1) - jnp.exp(t2) + lam_init


def _diff_attn_kernel(lamv_ref, q_ref, k_ref, v_ref, g_ref, o_ref, kb_ref, vt_ref, *, lam_init):
    tq = q_ref.shape[0]
    hps, n_kv = vt_ref.shape[:2]
    qi = pl.program_id(2)

    def head_cols(hh):
        return slice(hh * LANES, (hh + 1) * LANES)

    @pl.when(qi == 0)
    def _():
        for hh in range(hps):
            kb_ref[hh] = k_ref[hh].astype(BF16)
            for j in range(n_kv):
                vt_ref[hh, j] = v_ref[j * tq:(j + 1) * tq, head_cols(hh)].T.astype(BF16)

    lane = lax.broadcasted_iota(jnp.int32, (tq, LANES), 1)
    chains, qs = [], []
    for hh in range(hps):
        q = q_ref[:, head_cols(hh)]
        zero = jnp.zeros_like(q)
        for c in range(2):
            chains.append(hh)
            qs.append(jnp.where((lane < A_DK) if c == 0 else (lane >= A_DK), q, zero))
    key = lax.broadcasted_iota(jnp.int32, (tq, tq), 0)
    qry = lax.broadcasted_iota(jnp.int32, (tq, tq), 1)

    def block(j, carry, diagonal):
        r = pl.multiple_of(j * tq, tq)
        scores = [lax.dot_general(kb_ref[hh, pl.ds(r, tq), :], qc, NT_DIMS,
                                  preferred_element_type=F32)
                  for hh, qc in zip(chains, qs)]
        probs, stats = [], []
        for s, (m_prev, l_prev, _) in zip(scores, carry):
            if diagonal:
                s = jnp.where(key <= qry, s, -jnp.inf)
            m_new = jnp.maximum(m_prev, jnp.max(s, axis=0, keepdims=True))
            alpha = jnp.exp(m_prev - m_new)
            p = jnp.exp(s - m_new)
            stats.append((m_new, alpha * l_prev + jnp.sum(p, axis=0, keepdims=True), alpha))
            probs.append(p.astype(BF16))
        return tuple(
            (m_new, l_new, alpha * acc + jnp.dot(vt_ref[hh, j], p, preferred_element_type=F32))
            for hh, p, (m_new, l_new, alpha), (_, _, acc) in zip(chains, probs, stats, carry))

    init = tuple((jnp.full((1, tq), -jnp.inf, F32), jnp.zeros((1, tq), F32),
                  jnp.zeros((A_DV, tq), F32)) for _ in chains)
    carry = lax.fori_loop(0, qi, lambda j, c: block(j, c, False), init)
    carry = block(qi, carry, True)

    lam = _diff_lambda(lamv_ref[...], lam_init)
    for hh in range(hps):
        (_, l0, acc0), (_, l1, acc1) = carry[2 * hh], carry[2 * hh + 1]
        o = acc0 / l0 - lam * (acc1 / l1)
        ms = jnp.mean(o * o, axis=0, keepdims=True)
        o = o * lax.rsqrt(ms + EPS) * g_ref[...] * (1.0 - lam_init)
        o_ref[:, head_cols(hh)] = o.T.astype(o_ref.dtype)


def _diff_attn_prompt(q_all, k_hm, proj, v_block0, lamv, g_subln, bp, seq, lam_init):
    tq = _pick_tile(seq, ATTN_TILE, LANES)
    nq = seq // tq
    hps = ATTN_HEADS_PER_STEP
    assert A_HEADS % hps == 0 and v_block0 % hps == 0
    wide = hps * LANES
    return pl.pallas_call(
        functools.partial(_diff_attn_kernel, lam_init=lam_init),
        grid=(bp, A_HEADS // hps, nq),
        in_specs=[
            pl.BlockSpec(lamv.shape, lambda b, h, i: (0, 0)),
            pl.BlockSpec((tq, wide), lambda b, h, i: (b * nq + i, h)),
            pl.BlockSpec((None, hps, seq, LANES), lambda b, h, i: (b, h, 0, 0)),
            pl.BlockSpec((seq, wide), lambda b, h, i: (b, v_block0 // hps + h)),
            pl.BlockSpec((A_DV, 1), lambda b, h, i: (0, 0)),
        ],
        out_specs=pl.BlockSpec((tq, wide), lambda b, h, i: (b * nq + i, h)),
        out_shape=jax.ShapeDtypeStruct((bp * seq, A_HEADS * A_DV), BF16),
        scratch_shapes=[pltpu.VMEM((hps, seq, LANES), BF16),
                        pltpu.VMEM((hps, nq, A_DV, tq), BF16)],
        compiler_params=_params("parallel", "parallel", "arbitrary"),
        name="diff_attn_prompt",
    )(lamv, q_all, k_hm, proj, g_subln.reshape(A_DV, 1))


def _diff_attn_decode_kernel(pt_ref, lamv_ref, q_ref, ks_ref, vs_ref, g_ref, *rest,
                             pages, lam_init, n_steps):
    del pt_ref
    k_refs = rest[:pages]
    v_refs = rest[pages:2 * pages]
    o_ref = rest[2 * pages]
    qh_ref, m_ref, l_ref, acc_ref, kb_ref, vb_ref = rest[2 * pages + 1:]
    ls = q_ref.shape[0]
    rph = 2 * ls
    rows = A_HEADS * rph
    step = pl.program_id(1)

    def head_cols(h):
        return slice(h * A_DV, (h + 1) * A_DV)

    def per_head_rows(x):
        return jnp.concatenate([x[:, head_cols(h)] for h in range(A_HEADS) for _ in range(2)],
                               axis=0)

    @pl.when(step == 0)
    def _():
        q = q_ref[...].astype(F32)
        lane = lax.broadcasted_iota(jnp.int32, (ls, LANES), 1)
        for h in range(A_HEADS):
            qh = q[:, head_cols(h)]
            qh_ref[h * rph:h * rph + ls, :] = jnp.where(lane < A_DK, qh, 0.0)
            qh_ref[h * rph + ls:(h + 1) * rph, :] = jnp.where(lane >= A_DK, qh, 0.0)
        m_ref[...] = jnp.full(m_ref.shape, -jnp.inf, F32)
        l_ref[...] = jnp.zeros(l_ref.shape, F32)
        acc_ref[...] = jnp.zeros(acc_ref.shape, F32)

    for h in range(A_HEADS):
        for p in range(pages):
            tok = slice(p * PAGE_SIZE, (p + 1) * PAGE_SIZE)
            kb_ref[h, tok, :] = k_refs[p][h].astype(BF16)
            vb_ref[h, tok, :] = v_refs[p][h].astype(BF16)

    s = jnp.concatenate(
        [lax.dot_general(qh_ref[h * rph:(h + 1) * rph, :].astype(BF16), kb_ref[h], NT_DIMS,
                         preferred_element_type=F32) for h in range(A_HEADS)], axis=0)
    m_prev = m_ref[...]
    m_new = jnp.maximum(m_prev, jnp.max(s, axis=-1, keepdims=True))
    alpha = jnp.exp(m_prev - m_new)
    p_exp = jnp.exp(s - m_new)
    l_ref[...] = alpha * l_ref[...] + jnp.sum(p_exp, axis=-1, keepdims=True)
    pv = jnp.concatenate(
        [jnp.dot(p_exp[h * rph:(h + 1) * rph].astype(BF16), vb_ref[h],
                 preferred_element_type=F32) for h in range(A_HEADS)], axis=0)
    acc_ref[...] = alpha * acc_ref[...] + pv
    m_ref[...] = m_new

    @pl.when(step == n_steps - 1)
    def _():
        qf = qh_ref[...]
        ks = ks_ref[...].astype(BF16).astype(F32)
        vs = vs_ref[...].astype(BF16).astype(F32)
        row = lax.broadcasted_iota(jnp.int32, (rows, 1), 0)
        q_of_row = row % ls
        s_new = []
        for t in range(ls):
            k_t = per_head_rows(jnp.broadcast_to(ks[t:t + 1], ks.shape))
            st = jnp.sum(qf * k_t, axis=-1, keepdims=True)
            s_new.append(jnp.where(q_of_row >= t, st, -jnp.inf))
        m_old = m_ref[...]
        m_fin = functools.reduce(jnp.maximum, s_new, m_old)
        a_fin = jnp.exp(m_old - m_fin)
        l_fin = a_fin * l_ref[...]
        acc = a_fin * acc_ref[...]
        for t in range(ls):
            pt = jnp.exp(s_new[t] - m_fin)
            l_fin = l_fin + pt
            v_t = per_head_rows(jnp.broadcast_to(vs[t:t + 1], vs.shape))
            acc = acc + pt.astype(BF16).astype(F32) * v_t

        lam = _diff_lambda(lamv_ref[...], lam_init)
        second = (row % rph) >= ls
        acc = acc * (jnp.where(second, -lam, 1.0) / l_fin)
        g = g_ref[...]
        for h in range(A_HEADS):
            o = acc[h * rph:h * rph + ls] + acc[h * rph + ls:(h + 1) * rph]
            o_ref[:, head_cols(h)] = _rmsnorm_lanes(o, g) * (1.0 - lam_init)


def _diff_attn_decode(q_s, k_s, v_s, cache_k, cache_v, page_table, lamv, g_subln, layer, lam_init):
    bs, ls, width = q_s.shape
    n_pages = page_table.shape[1]
    pages = _pick_tile(n_pages, DEC_PAGES_PER_STEP, 1)
    n_steps = n_pages // pages
    rows = ls * 2 * A_HEADS
    assert (2 * ls) % 8 == 0 and cache_k.shape[2:] == (PAGE_SIZE, A_HEADS, 2 * A_DK)

    cache_k = jnp.transpose(cache_k, (0, 1, 3, 2, 4))
    cache_v = jnp.transpose(cache_v, (0, 1, 3, 2, 4))

    def page_spec(p):
        return pl.BlockSpec((None, None, A_HEADS, PAGE_SIZE, LANES),
                            lambda b, s, pt: (layer, pt[b * n_pages + s * pages + p], 0, 0, 0))

    def per_batch():
        return pl.BlockSpec((None, ls, width), lambda b, s, pt: (b, 0, 0))

    grid_spec = pltpu.PrefetchScalarGridSpec(
        num_scalar_prefetch=1,
        grid=(bs, n_steps),
        in_specs=[pl.BlockSpec(lamv.shape, lambda b, s, pt: (0, 0)),
                  per_batch(), per_batch(), per_batch(),
                  pl.BlockSpec((1, LANES), lambda b, s, pt: (0, 0))]
                 + [page_spec(p) for p in range(pages)]
                 + [page_spec(p) for p in range(pages)],
        out_specs=per_batch(),
        scratch_shapes=[pltpu.VMEM((rows, LANES), F32),
                        pltpu.VMEM((rows, 1), F32),
                        pltpu.VMEM((rows, 1), F32),
                        pltpu.VMEM((rows, A_DV), F32),
                        pltpu.VMEM((A_HEADS, pages * PAGE_SIZE, LANES), BF16),
                        pltpu.VMEM((A_HEADS, pages * PAGE_SIZE, A_DV), BF16)],
    )
    return pl.pallas_call(
        functools.partial(_diff_attn_decode_kernel, pages=pages, lam_init=lam_init, n_steps=n_steps),
        grid_spec=grid_spec,
        out_shape=jax.ShapeDtypeStruct((bs, ls, width), F32),
        compiler_params=_params("parallel", "arbitrary"),
        name="diff_attn_decode",
    )(page_table.reshape(-1), lamv, q_s, k_s, v_s, g_subln.reshape(1, LANES),
      *([cache_k] * pages), *([cache_v] * pages))


def _mem_attn_kernel(q_ref, k_ref, v_ref, g_ref, o_ref):
    g = g_ref[...]
    cols = [slice(h * MEM_HD, (h + 1) * MEM_HD) for h in range(MEM_HEADS)]
    scores = [lax.dot_general(_rmsnorm_lanes(q_ref[:, c], g).astype(BF16),
                              k_ref[:, c].astype(BF16), NT_DIMS, preferred_element_type=F32)
              for c in cols]
    probs = []
    for s in scores:
        s = s * (MEM_HD ** -0.5)
        e = jnp.exp(s - jnp.max(s, axis=-1, keepdims=True))
        probs.append((e / jnp.sum(e, axis=-1, keepdims=True)).astype(BF16))
    for c, p in zip(cols, probs):
        o_ref[:, c] = jnp.dot(p, v_ref[:, c].astype(BF16),
                              preferred_element_type=F32).astype(o_ref.dtype)


def _mem_attn_prompt(proj, q_block0, k_norm, kv, gq, bp, seq):
    mem = k_norm.shape[0] // bp
    tq = _pick_tile(seq, 512, LANES)
    nq = seq // tq
    wide = MEM_HEADS * MEM_HD
    assert q_block0 % MEM_HEADS == 0
    return pl.pallas_call(
        _mem_attn_kernel,
        grid=(bp, nq),
        in_specs=[
            pl.BlockSpec((tq, wide), lambda b, i: (b * nq + i, q_block0 // MEM_HEADS)),
            pl.BlockSpec((mem, wide), lambda b, i: (b, 0)),
            pl.BlockSpec((mem, wide), lambda b, i: (b, 1)),
            pl.BlockSpec((1, LANES), lambda b, i: (0, 0)),
        ],
        out_specs=pl.BlockSpec((tq, wide), lambda b, i: (b * nq + i, 0)),
        out_shape=jax.ShapeDtypeStruct((bp * seq, wide), BF16),
        compiler_params=_params("parallel", "arbitrary"),
        name="mem_attn_prompt",
    )(proj, k_norm, kv, gq.reshape(1, LANES))


def _mem_attn_sample(proj_s, q_block0, cache_k, cache_v, gq, layer):
    bs, ls, _ = proj_s.shape
    mem = cache_k.shape[2]
    wide = MEM_HEADS * MEM_HD
    assert q_block0 % MEM_HEADS == 0
    return pl.pallas_call(
        _mem_attn_kernel,
        grid=(bs,),
        in_specs=[
            pl.BlockSpec((None, ls, wide), lambda b: (b, 0, q_block0 // MEM_HEADS)),
            pl.BlockSpec((None, None, mem, wide), lambda b: (layer, b, 0, 0)),
            pl.BlockSpec((None, None, mem, wide), lambda b: (layer, b, 0, 0)),
            pl.BlockSpec((1, LANES), lambda b: (0, 0)),
        ],
        out_specs=pl.BlockSpec((None, ls, wide), lambda b: (b, 0, 0)),
        out_shape=jax.ShapeDtypeStruct((bs, ls, wide), F32),
        compiler_params=_params("arbitrary"),
        name="mem_attn_sample",
    )(proj_s, cache_k, cache_v, gq.reshape(1, LANES))


def _mem_k_norm_kernel(k_ref, g_ref, o_ref):
    o_ref[...] = _rmsnorm_lanes(k_ref[...], g_ref[...])


def _mem_k_norm(kv, gk):
    rows = kv.shape[0]
    tm = _pick_tile(rows, 256, 8)
    return pl.pallas_call(
        _mem_k_norm_kernel,
        grid=(rows // tm, MEM_HEADS),
        in_specs=[pl.BlockSpec((tm, LANES), lambda i, h: (i, h)),
                  pl.BlockSpec((1, LANES), lambda i, h: (0, 0))],
        out_specs=pl.BlockSpec((tm, LANES), lambda i, h: (i, h)),
        out_shape=jax.ShapeDtypeStruct((rows, MEM_HEADS * MEM_HD), F32),
        compiler_params=_params("parallel", "arbitrary"),
        name="mem_k_norm",
    )(kv, gk.reshape(1, LANES))


def _split3(x):
    p1 = x.astype(BF16)
    r1 = x - p1.astype(F32)
    p2 = r1.astype(BF16)
    p3 = (r1 - p2.astype(F32)).astype(BF16)
    return p1, p2, p3


def _hgrn_kernel(*refs, chunk, sub, valid, layer, has_state):
    if has_state:
        (q_ref, f_ref, v_ref, gate_ref, lb_ref, g_ref, tri_ref, s0_ref,
         o_ref, s_out_ref, st_ref) = refs
    else:
        (q_ref, f_ref, v_ref, gate_ref, lb_ref, g_ref, tri_ref,
         o_ref, s_out_ref, st_ref) = refs
    tl = q_ref.shape[0]
    heads = st_ref.shape[0]
    t = pl.program_id(2)

    @pl.when(t == 0)
    def _():
        for hh in range(heads):
            if has_state:
                st_ref[hh] = s0_ref[hh].T
            else:
                st_ref[hh] = jnp.zeros(st_ref.shape[1:], F32)

    lb = lb_ref[...]
    e = jnp.exp(lb - jnp.max(lb, axis=0, keepdims=True))
    sm = e / jnp.sum(e, axis=0, keepdims=True)
    lower = jnp.zeros((1, sm.shape[1]), F32)
    for r in range(1, layer + 1):
        lower = lower + sm[r:r + 1]

    g = g_ref[...]
    n_chunks = tl // chunk
    n_sub = chunk // sub
    sub_row = lax.broadcasted_iota(jnp.int32, (sub, 1), 0)
    chunk_row = lax.broadcasted_iota(jnp.int32, (chunk, 1), 0)

    q_all = q_ref[...]
    v_all = v_ref[...]
    f_all = lower + (1.0 - lower) * jax.nn.sigmoid(f_ref[...])
    kk_all = 1.0 - f_all
    tri = tri_ref[...]
    p1, p2, p3 = _split3(jnp.log(f_all))
    b_all = (jnp.dot(tri, p1, preferred_element_type=F32)
             + jnp.dot(tri, p2, preferred_element_type=F32)
             + jnp.dot(tri, p3, preferred_element_type=F32))
    v16_all = v_all.astype(BF16)

    units = [(hh, ci) for hh in range(heads) for ci in range(n_chunks)]

    def unit_block(x, unit):
        hh, ci = unit
        return x[ci * chunk:(ci + 1) * chunk, hh * LANES:(hh + 1) * LANES]

    incs, atts, b_lasts = [], [], []
    for u in units:
        q, b, kk = unit_block(q_all, u), unit_block(b_all, u), unit_block(kk_all, u)
        b_last = b[valid - 1:valid]
        k_dec = jnp.where(chunk_row < valid, kk * jnp.exp(b_last - b), 0.0)
        incs.append(lax.dot_general(unit_block(v16_all, u), k_dec.astype(BF16), TN_DIMS,
                                    preferred_element_type=F32))
        b_lasts.append(b_last)
        for i in range(1, n_sub):
            rs = slice(i * sub, (i + 1) * sub)
            b_ref = b[i * sub - 1:i * sub]
            q_dec = (q[rs] * jnp.exp(b[rs] - b_ref)).astype(BF16)
            k_dec = (kk[:i * sub] * jnp.exp(b_ref - b[:i * sub])).astype(BF16)
            atts.append(lax.dot_general(q_dec, k_dec, NT_DIMS, preferred_element_type=F32))

    diag = []
    for u in units:
        q, b, kk, v = (unit_block(x, u) for x in (q_all, b_all, kk_all, v_all))
        for i in range(n_sub):
            rs = slice(i * sub, (i + 1) * sub)
            qb, bb, kb, vb = q[rs], b[rs], kk[rs], v[rs]
            terms = []
            for s in range(sub):
                d = jnp.where(sub_row >= s, bb - bb[s:s + 1], -jnp.inf)
                w = jnp.sum(qb * kb[s:s + 1] * jnp.exp(d), axis=-1, keepdims=True)
                terms.append(w * vb[s:s + 1])
            while len(terms) > 1:
                terms = [a + c for a, c in zip(terms[0::2], terms[1::2])]
            diag.append(terms[0])

    intra = []
    for n, u in enumerate(units):
        v16 = unit_block(v16_all, u)
        for i in range(n_sub):
            o_i = diag[n * n_sub + i]
            if i > 0:
                att = atts[n * (n_sub - 1) + i - 1]
                o_i = o_i + jnp.dot(att.astype(BF16), v16[:i * sub], preferred_element_type=F32)
            intra.append(o_i)

    states = [st_ref[hh] for hh in range(heads)]
    for n, u in enumerate(units):
        hh, ci = u
        q, b = unit_block(q_all, u), unit_block(b_all, u)
        o_inter = lax.dot_general((q * jnp.exp(b)).astype(BF16), states[hh].astype(BF16),
                                  NT_DIMS, preferred_element_type=F32)
        states[hh] = states[hh] * jnp.exp(b_lasts[n]) + incs[n]
        parts = intra[n * n_sub:(n + 1) * n_sub]
        o = o_inter + (parts[0] if n_sub == 1 else jnp.concatenate(parts, axis=0))
        rows, cols = slice(ci * chunk, (ci + 1) * chunk), slice(hh * LANES, (hh + 1) * LANES)
        gate = gate_ref[rows, cols]
        o_ref[rows, cols] = (_rmsnorm_lanes(o, g)
                             * (gate * jax.nn.sigmoid(gate))).astype(o_ref.dtype)
    for hh in range(heads):
        st_ref[hh] = states[hh]

    @pl.when(t == pl.num_programs(2) - 1)
    def _():
        for hh in range(heads):
            s_out_ref[hh] = states[hh].T


def _tri(rows, chunk):
    r = jnp.arange(rows)
    same = (r[:, None] // chunk) == (r[None, :] // chunk)
    return (same & (r[:, None] >= r[None, :])).astype(BF16)


def _hgrn_prompt(proj, lb_logits, g_out, bp, seq, layer):
    chunk = math.gcd(seq, B_CHUNK)
    sub = math.gcd(chunk, B_SUB)
    tl = _pick_tile(seq, 256, chunk)
    nt = seq // tl
    h_ = B_HEADS

    def col(block0):
        return pl.BlockSpec((tl, LANES), lambda b, h, t: (b * nt + t, block0 + h))

    return pl.pallas_call(
        functools.partial(_hgrn_kernel, chunk=chunk, sub=sub, valid=chunk, layer=layer,
                          has_state=False),
        grid=(bp, h_, nt),
        in_specs=[col(0), col(h_), col(2 * h_), col(3 * h_),
                  pl.BlockSpec((lb_logits.shape[0], LANES), lambda b, h, t: (0, h)),
                  pl.BlockSpec((1, LANES), lambda b, h, t: (0, 0)),
                  pl.BlockSpec((tl, tl), lambda b, h, t: (0, 0))],
        out_specs=[pl.BlockSpec((tl, LANES), lambda b, h, t: (b * nt + t, h)),
                   pl.BlockSpec((None, 1, B_DK, B_DV), lambda b, h, t: (b, h, 0, 0))],
        out_shape=[jax.ShapeDtypeStruct((bp * seq, h_ * B_DV), BF16),
                   jax.ShapeDtypeStruct((bp, h_, B_DK, B_DV), F32)],
        scratch_shapes=[pltpu.VMEM((1, B_DV, B_DK), F32)],
        compiler_params=_params("parallel", "parallel", "arbitrary"),
        name="hgrn_prompt",
    )(proj, proj, proj, proj, lb_logits, g_out.reshape(1, LANES), _tri(tl, chunk))


def _hgrn_sample(proj_s, state, lb_logits, g_out, layer, state_layer):
    bs, ls, n = proj_s.shape
    chunk = 8
    assert ls <= chunk
    padded = jnp.pad(proj_s, ((0, 0), (0, chunk - ls), (0, 0)))
    h_ = B_HEADS
    hps = HGRN_SAMPLE_HEADS_PER_STEP
    assert h_ % hps == 0
    wide = hps * LANES

    def col(block0):
        return pl.BlockSpec((None, chunk, wide), lambda b, h, t: (b, 0, block0 // hps + h))

    o, s_new = pl.pallas_call(
        functools.partial(_hgrn_kernel, chunk=chunk, sub=chunk, valid=ls, layer=layer,
                          has_state=True),
        grid=(bs, h_ // hps, 1),
        in_specs=[col(0), col(h_), col(2 * h_), col(3 * h_),
                  pl.BlockSpec((lb_logits.shape[0], wide), lambda b, h, t: (0, h)),
                  pl.BlockSpec((1, LANES), lambda b, h, t: (0, 0)),
                  pl.BlockSpec((chunk, chunk), lambda b, h, t: (0, 0)),
                  pl.BlockSpec((None, None, hps, B_DK, B_DV),
                               lambda b, h, t: (state_layer, b, h, 0, 0))],
        out_specs=[pl.BlockSpec((None, chunk, wide), lambda b, h, t: (b, 0, h)),
                   pl.BlockSpec((None, hps, B_DK, B_DV), lambda b, h, t: (b, h, 0, 0))],
        out_shape=[jax.ShapeDtypeStruct((bs, chunk, h_ * B_DV), F32),
                   jax.ShapeDtypeStruct((bs, h_, B_DK, B_DV), F32)],
        scratch_shapes=[pltpu.VMEM((hps, B_DV, B_DK), F32)],
        compiler_params=_params("parallel", "parallel", "arbitrary"),
        name="hgrn_sample",
    )(padded, padded, padded, padded, lb_logits, g_out.reshape(1, LANES), _tri(chunk, chunk), state)
    return o[:, :ls], s_new


def _rope_tables(pos):
    half = A_DK // 2
    inv_freq = ROPE_THETA ** (-jnp.arange(half, dtype=F32) / half)
    ang = pos.astype(F32)[:, None] * inv_freq[None, :]
    cos, sin = jnp.cos(ang), jnp.sin(ang)
    reps = LANES // A_DK
    return (jnp.tile(cos, (1, 2 * reps)), jnp.tile(jnp.concatenate([-sin, sin], axis=1), (1, reps)))


def kernel(x_prompt, x_sample, cache_attn_k, cache_attn_v, state_hgrn, cache_mem_k, cache_mem_v,
           page_table, mem_prompt, norm_ffn, w_ffn_gate, w_ffn_up, w_ffn_down, norm_mix, norm_mem,
           w_mem_kv, gq_mem, gk_mem, w_out, w_in_attn, gq_attn, gk_attn, lam_q1, lam_k1, lam_q2,
           lam_k2, g_subln, w_in_hgrn, lb_logits, g_hgrn_out):
    bp, seq, d = x_prompt.shape
    bs, ls, _ = x_sample.shape
    depth = norm_mix.shape[0]
    mem = mem_prompt.shape[1]
    mp = bp * seq
    ms = bs * ls
    past_len = page_table.shape[1] * PAGE_SIZE
    mem_w = MEM_HEADS * MEM_HD
    qk_w = A_HEADS * 2 * A_DK
    v_w = A_HEADS * A_DV

    x = jnp.concatenate([x_prompt.reshape(mp, d), x_sample.reshape(ms, d)], axis=0)
    pos = jnp.concatenate([jnp.tile(jnp.arange(seq), bp), jnp.tile(past_len + jnp.arange(ls), bs)])
    cos, sin_signed = _rope_tables(pos)
    mem_rows = mem_prompt.reshape(bp * mem, d)
    cmk = cache_mem_k.reshape(depth, bs, mem, mem_w)
    cmv = cache_mem_v.reshape(depth, bs, mem, mem_w)

    k_rows_p, v_rows_p, k_rows_s, v_rows_s = [], [], [], []
    st_p, st_s, mem_k_new, mem_v_new = [], [], [], []
    for i in range(depth):
        x = _ffn_half(x, norm_ffn, w_ffn_gate, w_ffn_up, w_ffn_down, i, 0)

        kv = _norm_matmul(mem_rows, norm_mem, i, w_mem_kv, i)
        k_norm = _mem_k_norm(kv, gk_mem[i])
        mem_k_new.append(k_norm.reshape(bp, mem, MEM_HEADS, MEM_HD))
        mem_v_new.append(kv[:, mem_w:].reshape(bp, mem, MEM_HEADS, MEM_HD))

        if i % 2 == 0:
            a = i // 2
            lam_init = 0.8 - 0.6 * math.exp(-0.3 * i)
            proj = _norm_matmul(x, norm_mix, i, w_in_attn, a)
            q_all = _rot_norm_rows(proj, 0, cos, sin_signed, gq_attn[a], 0, mp + ms,
                                   A_DK ** -0.5, BF16)
            k_hm = _rot_norm_head_major(proj, A_HEADS, cos, sin_signed, gk_attn[a], bp, seq)
            k_s = _rot_norm_rows(proj, A_HEADS, cos, sin_signed, gk_attn[a], mp, ms, 1.0, F32)
            lamv = jnp.stack([lam_q1[a], lam_k1[a], lam_q2[a], lam_k2[a]])
            o_p = _diff_attn_prompt(q_all, k_hm, proj, 2 * A_HEADS, lamv, g_subln[a], bp, seq,
                                    lam_init)
            k_rows_p.append(jnp.transpose(k_hm, (0, 2, 1, 3)))
            v_rows_p.append(proj[:mp, 2 * qk_w:2 * qk_w + v_w].reshape(bp, seq, A_HEADS, A_DV))
            proj_s = proj[mp:].reshape(bs, ls, proj.shape[1])
            q_s = q_all[mp:].reshape(bs, ls, qk_w)
            k_s = k_s.reshape(bs, ls, qk_w)
            v_s = proj_s[..., 2 * qk_w:2 * qk_w + v_w]
            o_s = _diff_attn_decode(q_s, k_s, v_s, cache_attn_k, cache_attn_v, page_table, lamv,
                                    g_subln[a], a, lam_init)
            k_rows_s.append(k_s.reshape(bs, ls, A_HEADS, 2 * A_DK))
            v_rows_s.append(v_s.reshape(bs, ls, A_HEADS, A_DV))
            mq_block0 = (2 * qk_w + v_w) // LANES
        else:
            j = i // 2
            proj = _norm_matmul(x, norm_mix, i, w_in_hgrn, j)
            proj_s = proj[mp:].reshape(bs, ls, proj.shape[1])
            o_p, s_p = _hgrn_prompt(proj, lb_logits, g_hgrn_out[j], bp, seq, i)
            o_s, s_s = _hgrn_sample(proj_s, state_hgrn, lb_logits, g_hgrn_out[j], i, j)
            st_p.append(s_p)
            st_s.append(s_s)
            mq_block0 = (2 * B_HEADS * B_DK + 2 * B_HEADS * B_DV) // LANES

        m_p = _mem_attn_prompt(proj, mq_block0, k_norm, kv, gq_mem[i], bp, seq)
        m_s = _mem_attn_sample(proj_s, mq_block0, cmk, cmv, gq_mem[i], i)
        o_all = jnp.concatenate([o_p, o_s.reshape(ms, -1).astype(BF16)], axis=0)
        m_all = jnp.concatenate([m_p, m_s.reshape(ms, -1).astype(BF16)], axis=0)
        x = _out_proj(x, o_all, m_all, w_out, i)

        x = _ffn_half(x, norm_ffn, w_ffn_gate, w_ffn_up, w_ffn_down, i, 1)

    return (x[:mp].reshape(bp, seq, d), x[mp:].reshape(bs, ls, d),
            jnp.stack(k_rows_p), jnp.stack(v_rows_p), jnp.stack(k_rows_s), jnp.stack(v_rows_s),
            jnp.stack(st_p), jnp.stack(st_s), jnp.stack(mem_k_new), jnp.stack(mem_v_new))
```

```python
import functools
import math

import jax
import jax.numpy as jnp
from jax import lax
from jax.experimental import pallas as pl
from jax.experimental.pallas import tpu as pltpu

F32 = jnp.float32
BF16 = jnp.bfloat16

EPS = 1e-6
LOG2_E = math.log2(math.e)
ROPE_THETA = 10000.0
A_HEADS = 12
A_DK = 64
A_DV = 128
B_HEADS = 12
B_DK = 128
B_DV = 128
B_CHUNK = 64
B_SUB = 8
MEM_HEADS = 4
MEM_HD = 128
PAGE_SIZE = 128
LANES = 128
SUBLANES_BF16 = 16
V7X_VMEM_LIMIT_BYTES = 56 * 1024 * 1024
ROW_TILE_TARGET = 1040
PROJ_ROW_TILE_TARGET = 1664
COL_TILE = 512
FFN_COL_TILE = 256
NORM_ROWS_PER_ITER = 128
DEC_PAGES_PER_STEP = 8
ATTN_TILE = 512
ATTN_HEADS_PER_STEP = 2
HGRN_SAMPLE_HEADS_PER_STEP = 4
HGRN_PROMPT_HEADS_PER_STEP = 4
NT_DIMS = (((1,), (1,)), ((), ()))
TN_DIMS = (((0,), (0,)), ((), ()))


def _params(*semantics):
    return pltpu.CompilerParams(dimension_semantics=semantics,
                                vmem_limit_bytes=V7X_VMEM_LIMIT_BYTES)


def _pick_tile(n, target, align):
    best = None
    for t in range(align, min(n, target) + 1, align):
        if n % t == 0:
            best = t
    assert best is not None, (n, target, align)
    return best


def _rmsnorm_lanes(x, g):
    ms = jnp.mean(x * x, axis=-1, keepdims=True)
    return x * lax.rsqrt(ms + EPS) * g


def _norm_rows_into(h_ref, x_ref, g_ref, copy_ref=None):
    rows = x_ref.shape[0]
    chunk = _pick_tile(rows, NORM_ROWS_PER_ITER, SUBLANES_BF16 if rows % SUBLANES_BF16 == 0 else 8)
    g = g_ref[...]

    def body(i, carry):
        r = pl.multiple_of(i * chunk, chunk)
        x = x_ref[pl.ds(r, chunk), :]
        h_ref[pl.ds(r, chunk), :] = _rmsnorm_lanes(x, g).astype(h_ref.dtype)
        if copy_ref is not None:
            copy_ref[pl.ds(r, chunk), :] = x
        return carry

    lax.fori_loop(0, rows // chunk, body, 0)


def _ffn_kernel(x_ref, g_ref, wg_ref, wu_ref, wd_ref, o_ref, h_ref, *, n_out_chunks):
    @pl.when(pl.program_id(1) == 0)
    def _():
        _norm_rows_into(h_ref, x_ref, g_ref, copy_ref=o_ref)

    h = h_ref[...]
    gate = jnp.dot(h, wg_ref[...].astype(BF16), preferred_element_type=F32)
    up = jnp.dot(h, wu_ref[...].astype(BF16), preferred_element_type=F32)
    act = (0.5 * (gate * jax.nn.sigmoid(gate)) * up).astype(BF16)
    width = o_ref.shape[1] // n_out_chunks
    for c in range(n_out_chunks):
        cols = slice(c * width, (c + 1) * width)
        o_ref[:, cols] += jnp.dot(act, wd_ref[:, cols].astype(BF16), preferred_element_type=F32)


def _ffn_half(x, norm_ffn, w_gate, w_up, w_down, layer, half):
    m, d = x.shape
    f = w_gate.shape[-1]
    tm = _pick_tile(m, ROW_TILE_TARGET, SUBLANES_BF16)
    tf = _pick_tile(f, FFN_COL_TILE, LANES)
    n_out_chunks = max(1, d // 512)
    g4 = norm_ffn.reshape(norm_ffn.shape[0], 2, 1, d)
    return pl.pallas_call(
        functools.partial(_ffn_kernel, n_out_chunks=n_out_chunks),
        grid=(m // tm, f // tf),
        in_specs=[
            pl.BlockSpec((tm, d), lambda i, j: (i, 0), pipeline_mode=pl.Buffered(1)),
            pl.BlockSpec((None, None, 1, d), lambda i, j: (layer, half, 0, 0)),
            pl.BlockSpec((None, None, d, tf), lambda i, j: (layer, half, 0, j)),
            pl.BlockSpec((None, None, d, tf), lambda i, j: (layer, half, 0, j)),
            pl.BlockSpec((None, None, tf, d), lambda i, j: (layer, half, j, 0)),
        ],
        out_specs=pl.BlockSpec((tm, d), lambda i, j: (i, 0)),
        out_shape=jax.ShapeDtypeStruct((m, d), F32),
        scratch_shapes=[pltpu.VMEM((tm, d), BF16)],
        compiler_params=_params("parallel", "arbitrary"),
        name="ffn_half",
    )(x, g4, w_gate, w_up, w_down)


def _norm_matmul_kernel(x_ref, g_ref, w_ref, o_ref, h_ref):
    @pl.when(pl.program_id(1) == 0)
    def _():
        _norm_rows_into(h_ref, x_ref, g_ref)

    o_ref[...] = jnp.dot(h_ref[...], w_ref[...].astype(BF16), preferred_element_type=F32)


def _norm_matmul(x, gains, g_idx, w, w_idx):
    m, d = x.shape
    n = w.shape[-1]
    tm = _pick_tile(m, PROJ_ROW_TILE_TARGET, SUBLANES_BF16)
    tn = _pick_tile(n, COL_TILE, LANES)
    g3 = gains.reshape(gains.shape[0], 1, d)
    return pl.pallas_call(
        _norm_matmul_kernel,
        grid=(m // tm, n // tn),
        in_specs=[
            pl.BlockSpec((tm, d), lambda i, j: (i, 0), pipeline_mode=pl.Buffered(1)),
            pl.BlockSpec((None, 1, d), lambda i, j: (g_idx, 0, 0)),
            pl.BlockSpec((None, d, tn), lambda i, j: (w_idx, 0, j)),
        ],
        out_specs=pl.BlockSpec((tm, tn), lambda i, j: (i, j)),
        out_shape=jax.ShapeDtypeStruct((m, n), F32),
        scratch_shapes=[pltpu.VMEM((tm, d), BF16)],
        compiler_params=_params("parallel", "arbitrary"),
        name="norm_matmul",
    )(x, g3, w)


def _out_proj_kernel(x_ref, o_ref, m_ref, wa_ref, wb_ref, y_ref):
    acc = jnp.dot(o_ref[...], wa_ref[...].astype(BF16), preferred_element_type=F32)
    acc += jnp.dot(m_ref[...], wb_ref[...].astype(BF16), preferred_element_type=F32)
    y_ref[...] = x_ref[...] + acc


def _out_proj(x, o, mem, w_out, layer):
    m, d = x.shape
    wo, wm = o.shape[1], mem.shape[1]
    assert wo % wm == 0 and wo + wm == w_out.shape[1]
    tm = _pick_tile(m, PROJ_ROW_TILE_TARGET, SUBLANES_BF16)
    tn = _pick_tile(d, COL_TILE, LANES)
    return pl.pallas_call(
        _out_proj_kernel,
        grid=(m // tm, d // tn),
        in_specs=[
            pl.BlockSpec((tm, tn), lambda i, j: (i, j)),
            pl.BlockSpec((tm, wo), lambda i, j: (i, 0)),
            pl.BlockSpec((tm, wm), lambda i, j: (i, 0)),
            pl.BlockSpec((None, wo, tn), lambda i, j: (layer, 0, j)),
            pl.BlockSpec((None, wm, tn), lambda i, j: (layer, wo // wm, j)),
        ],
        out_specs=pl.BlockSpec((tm, tn), lambda i, j: (i, j)),
        out_shape=jax.ShapeDtypeStruct((m, d), F32),
        compiler_params=_params("parallel", "arbitrary"),
        name="out_proj",
    )(x, o, mem, w_out, w_out)


def _group_mean(xsq, gm):
    hi = xsq.astype(BF16)
    lo = (xsq - hi.astype(F32)).astype(BF16)
    return (jnp.dot(hi, gm, preferred_element_type=F32)
            + jnp.dot(lo, gm, preferred_element_type=F32))


def _rot_norm_kernel(p_ref, cos_ref, sin_ref, g_ref, gm_ref, o_ref, *, scale):
    x = p_ref[...]
    y = x * lax.rsqrt(_group_mean(x * x, gm_ref[...]) + EPS) * g_ref[...]
    lane = lax.broadcasted_iota(jnp.int32, y.shape, 1)
    lower_half = (lane % A_DK) < (A_DK // 2)
    partner = jnp.where(lower_half,
                        pltpu.roll(y, LANES - A_DK // 2, 1),
                        pltpu.roll(y, A_DK // 2, 1))
    o_ref[...] = ((y * cos_ref[...] + partner * sin_ref[...]) * scale).astype(o_ref.dtype)


def _group_mean_matrix():
    lane = jnp.arange(LANES)
    return jnp.where((lane[:, None] // A_DK) == (lane[None, :] // A_DK), 1.0 / A_DK, 0.0).astype(BF16)


def _rot_norm_rows(proj, block0, cos, sin_signed, gain, row0, rows, scale, dtype):
    tm = _pick_tile(rows, ROW_TILE_TARGET, SUBLANES_BF16)
    assert row0 % tm == 0
    r0 = row0 // tm
    g = jnp.tile(gain, LANES // A_DK).reshape(1, LANES)
    return pl.pallas_call(
        functools.partial(_rot_norm_kernel, scale=scale),
        grid=(rows // tm, A_HEADS),
        in_specs=[
            pl.BlockSpec((tm, LANES), lambda i, h: (r0 + i, block0 + h)),
            pl.BlockSpec((tm, LANES), lambda i, h: (r0 + i, 0)),
            pl.BlockSpec((tm, LANES), lambda i, h: (r0 + i, 0)),
            pl.BlockSpec((1, LANES), lambda i, h: (0, 0)),
            pl.BlockSpec((LANES, LANES), lambda i, h: (0, 0)),
        ],
        out_specs=pl.BlockSpec((tm, LANES), lambda i, h: (i, h)),
        out_shape=jax.ShapeDtypeStruct((rows, A_HEADS * LANES), dtype),
        compiler_params=_params("parallel", "arbitrary"),
        name="rot_norm_rows",
    )(proj, cos, sin_signed, g, _group_mean_matrix())


def _rot_norm_head_major(proj, block0, cos, sin_signed, gain, bp, seq):
    tr = _pick_tile(seq, 1024, 8)
    nt = seq // tr
    g = jnp.tile(gain, LANES // A_DK).reshape(1, LANES)
    return pl.pallas_call(
        functools.partial(_rot_norm_kernel, scale=1.0),
        grid=(bp, nt, A_HEADS),
        in_specs=[
            pl.BlockSpec((tr, LANES), lambda b, i, h: (b * nt + i, block0 + h)),
            pl.BlockSpec((tr, LANES), lambda b, i, h: (b * nt + i, 0)),
            pl.BlockSpec((tr, LANES), lambda b, i, h: (b * nt + i, 0)),
            pl.BlockSpec((1, LANES), lambda b, i, h: (0, 0)),
            pl.BlockSpec((LANES, LANES), lambda b, i, h: (0, 0)),
        ],
        out_specs=pl.BlockSpec((None, None, tr, LANES), lambda b, i, h: (b, h, i, 0)),
        out_shape=jax.ShapeDtypeStruct((bp, A_HEADS, seq, LANES), F32),
        compiler_params=_params("parallel", "parallel", "arbitrary"),
        name="rot_norm_head_major",
    )(proj, cos, sin_signed, g, _group_mean_matrix())


def _diff_lambda(lamv, lam_init):
    t1 = jnp.sum(lamv[0:1] * lamv[1:2], axis=-1, keepdims=True)
    t2 = jnp.sum(lamv[2:3] * lamv[3:4], axis=-1, keepdims=True)
    return jnp.exp(t1) - jnp.exp(t2) + lam_init


def _diff_attn_kernel(lamv_ref, q_ref, k_ref, v_ref, g_ref, o_ref, kb_ref, vt_ref, *, lam_init):
    tq = q_ref.shape[0]
    hps, n_kv = vt_ref.shape[:2]
    qi = pl.program_id(2)

    def head_cols(hh):
        return slice(hh * LANES, (hh + 1) * LANES)

    @pl.when(qi == 0)
    def _():
        for hh in range(hps):
            kb_ref[hh] = k_ref[hh].astype(BF16)
            for j in range(n_kv):
                vt_ref[hh, j] = v_ref[j * tq:(j + 1) * tq, head_cols(hh)].T.astype(BF16)

    lane = lax.broadcasted_iota(jnp.int32, (tq, LANES), 1)
    chains, qs = [], []
    for hh in range(hps):
        q = q_ref[:, head_cols(hh)]
        zero = jnp.zeros_like(q)
        for c in range(2):
            chains.append(hh)
            qs.append(jnp.where((lane < A_DK) if c == 0 else (lane >= A_DK), q, zero))
    key = lax.broadcasted_iota(jnp.int32, (tq, tq), 0)
    qry = lax.broadcasted_iota(jnp.int32, (tq, tq), 1)

    def block(j, carry, diagonal):
        r = pl.multiple_of(j * tq, tq)
        scores = [lax.dot_general(kb_ref[hh, pl.ds(r, tq), :], qc, NT_DIMS,
                                  preferred_element_type=F32)
                  for hh, qc in zip(chains, qs)]
        probs, stats = [], []
        for s, (m_prev, l_prev, _) in zip(scores, carry):
            if diagonal:
                s = jnp.where(key <= qry, s, -jnp.inf)
            m_new = jnp.maximum(m_prev, jnp.max(s, axis=0, keepdims=True))
            alpha = jnp.exp2(m_prev - m_new)
            p = jnp.exp2(s - m_new)
            stats.append((m_new, alpha * l_prev + jnp.sum(p, axis=0, keepdims=True), alpha))
            probs.append(p.astype(BF16))
        return tuple(
            (m_new, l_new, alpha * acc + jnp.dot(vt_ref[hh, j], p, preferred_element_type=F32))
            for hh, p, (m_new, l_new, alpha), (_, _, acc) in zip(chains, probs, stats, carry))

    init = tuple((jnp.full((1, tq), -jnp.inf, F32), jnp.zeros((1, tq), F32),
                  jnp.zeros((A_DV, tq), F32)) for _ in chains)
    carry = lax.fori_loop(0, qi, lambda j, c: block(j, c, False), init)
    carry = block(qi, carry, True)

    lam = _diff_lambda(lamv_ref[...], lam_init)
    for hh in range(hps):
        (_, l0, acc0), (_, l1, acc1) = carry[2 * hh], carry[2 * hh + 1]
        o = acc0 / l0 - lam * (acc1 / l1)
        ms = jnp.mean(o * o, axis=0, keepdims=True)
        o = o * lax.rsqrt(ms + EPS) * g_ref[...] * (1.0 - lam_init)
        o_ref[:, head_cols(hh)] = o.T.astype(o_ref.dtype)


def _diff_attn_prompt(q_all, k_hm, proj, v_block0, lamv, g_subln, bp, seq, lam_init):
    tq = _pick_tile(seq, ATTN_TILE, LANES)
    nq = seq // tq
    hps = ATTN_HEADS_PER_STEP
    assert A_HEADS % hps == 0 and v_block0 % hps == 0
    wide = hps * LANES
    return pl.pallas_call(
        functools.partial(_diff_attn_kernel, lam_init=lam_init),
        grid=(bp, A_HEADS // hps, nq),
        in_specs=[
            pl.BlockSpec(lamv.shape, lambda b, h, i: (0, 0)),
            pl.BlockSpec((tq, wide), lambda b, h, i: (b * nq + i, h)),
            pl.BlockSpec((None, hps, seq, LANES), lambda b, h, i: (b, h, 0, 0)),
            pl.BlockSpec((seq, wide), lambda b, h, i: (b, v_block0 // hps + h)),
            pl.BlockSpec((A_DV, 1), lambda b, h, i: (0, 0)),
        ],
        out_specs=pl.BlockSpec((tq, wide), lambda b, h, i: (b * nq + i, h)),
        out_shape=jax.ShapeDtypeStruct((bp * seq, A_HEADS * A_DV), BF16),
        scratch_shapes=[pltpu.VMEM((hps, seq, LANES), BF16),
                        pltpu.VMEM((hps, nq, A_DV, tq), BF16)],
        compiler_params=_params("parallel", "parallel", "arbitrary"),
        name="diff_attn_prompt",
    )(lamv, q_all, k_hm, proj, g_subln.reshape(A_DV, 1))


def _diff_attn_decode_kernel(pt_ref, lamv_ref, q_ref, ks_ref, vs_ref, g_ref, *rest,
                             pages, lam_init, n_steps):
    del pt_ref
    k_refs = rest[:pages]
    v_refs = rest[pages:2 * pages]
    o_ref = rest[2 * pages]
    qh_ref, m_ref, l_ref, acc_ref, kb_ref, vb_ref = rest[2 * pages + 1:]
    ls = q_ref.shape[0]
    rph = 2 * ls
    rows = A_HEADS * rph
    step = pl.program_id(1)

    def head_cols(h):
        return slice(h * A_DV, (h + 1) * A_DV)

    def per_head_rows(x):
        return jnp.concatenate([x[:, head_cols(h)] for h in range(A_HEADS) for _ in range(2)],
                               axis=0)

    @pl.when(step == 0)
    def _():
        q = q_ref[...].astype(F32)
        lane = lax.broadcasted_iota(jnp.int32, (ls, LANES), 1)
        for h in range(A_HEADS):
            qh = q[:, head_cols(h)]
            qh_ref[h * rph:h * rph + ls, :] = jnp.where(lane < A_DK, qh, 0.0)
            qh_ref[h * rph + ls:(h + 1) * rph, :] = jnp.where(lane >= A_DK, qh, 0.0)
        m_ref[...] = jnp.full(m_ref.shape, -jnp.inf, F32)
        l_ref[...] = jnp.zeros(l_ref.shape, F32)
        acc_ref[...] = jnp.zeros(acc_ref.shape, F32)

    for h in range(A_HEADS):
        for p in range(pages):
            tok = slice(p * PAGE_SIZE, (p + 1) * PAGE_SIZE)
            kb_ref[h, tok, :] = k_refs[p][h].astype(BF16)
            vb_ref[h, tok, :] = v_refs[p][h].astype(BF16)

    s = jnp.concatenate(
        [lax.dot_general(qh_ref[h * rph:(h + 1) * rph, :].astype(BF16), kb_ref[h], NT_DIMS,
                         preferred_element_type=F32) for h in range(A_HEADS)], axis=0)
    m_prev = m_ref[...]
    m_new = jnp.maximum(m_prev, jnp.max(s, axis=-1, keepdims=True))
    alpha = jnp.exp2(m_prev - m_new)
    p_exp = jnp.exp2(s - m_new)
    l_ref[...] = alpha * l_ref[...] + jnp.sum(p_exp, axis=-1, keepdims=True)
    pv = jnp.concatenate(
        [jnp.dot(p_exp[h * rph:(h + 1) * rph].astype(BF16), vb_ref[h],
                 preferred_element_type=F32) for h in range(A_HEADS)], axis=0)
    acc_ref[...] = alpha * acc_ref[...] + pv
    m_ref[...] = m_new

    @pl.when(step == n_steps - 1)
    def _():
        qf = qh_ref[...]
        ks = ks_ref[...].astype(BF16).astype(F32)
        vs = vs_ref[...].astype(BF16).astype(F32)
        row = lax.broadcasted_iota(jnp.int32, (rows, 1), 0)
        q_of_row = row % ls
        s_new = []
        for t in range(ls):
            k_t = per_head_rows(jnp.broadcast_to(ks[t:t + 1], ks.shape))
            st = jnp.sum(qf * k_t, axis=-1, keepdims=True)
            s_new.append(jnp.where(q_of_row >= t, st, -jnp.inf))
        m_old = m_ref[...]
        m_fin = functools.reduce(jnp.maximum, s_new, m_old)
        a_fin = jnp.exp2(m_old - m_fin)
        l_fin = a_fin * l_ref[...]
        acc = a_fin * acc_ref[...]
        for t in range(ls):
            pt = jnp.exp2(s_new[t] - m_fin)
            l_fin = l_fin + pt
            v_t = per_head_rows(jnp.broadcast_to(vs[t:t + 1], vs.shape))
            acc = acc + pt.astype(BF16).astype(F32) * v_t

        lam = _diff_lambda(lamv_ref[...], lam_init)
        second = (row % rph) >= ls
        acc = acc * (jnp.where(second, -lam, 1.0) / l_fin)
        g = g_ref[...]
        for h in range(A_HEADS):
            o = acc[h * rph:h * rph + ls] + acc[h * rph + ls:(h + 1) * rph]
            o_ref[:, head_cols(h)] = _rmsnorm_lanes(o, g) * (1.0 - lam_init)


def _diff_attn_decode(q_s, k_s, v_s, cache_k, cache_v, page_table, lamv, g_subln, layer, lam_init):
    bs, ls, width = q_s.shape
    n_pages = page_table.shape[1]
    pages = _pick_tile(n_pages, DEC_PAGES_PER_STEP, 1)
    n_steps = n_pages // pages
    rows = ls * 2 * A_HEADS
    assert (2 * ls) % 8 == 0 and cache_k.shape[2:] == (PAGE_SIZE, A_HEADS, 2 * A_DK)

    cache_k = jnp.transpose(cache_k, (0, 1, 3, 2, 4))
    cache_v = jnp.transpose(cache_v, (0, 1, 3, 2, 4))

    def page_spec(p):
        return pl.BlockSpec((None, None, A_HEADS, PAGE_SIZE, LANES),
                            lambda b, s, pt: (layer, pt[b * n_pages + s * pages + p], 0, 0, 0))

    def per_batch():
        return pl.BlockSpec((None, ls, width), lambda b, s, pt: (b, 0, 0))

    grid_spec = pltpu.PrefetchScalarGridSpec(
        num_scalar_prefetch=1,
        grid=(bs, n_steps),
        in_specs=[pl.BlockSpec(lamv.shape, lambda b, s, pt: (0, 0)),
                  per_batch(), per_batch(), per_batch(),
                  pl.BlockSpec((1, LANES), lambda b, s, pt: (0, 0))]
                 + [page_spec(p) for p in range(pages)]
                 + [page_spec(p) for p in range(pages)],
        out_specs=per_batch(),
        scratch_shapes=[pltpu.VMEM((rows, LANES), F32),
                        pltpu.VMEM((rows, 1), F32),
                        pltpu.VMEM((rows, 1), F32),
                        pltpu.VMEM((rows, A_DV), F32),
                        pltpu.VMEM((A_HEADS, pages * PAGE_SIZE, LANES), BF16),
                        pltpu.VMEM((A_HEADS, pages * PAGE_SIZE, A_DV), BF16)],
    )
    return pl.pallas_call(
        functools.partial(_diff_attn_decode_kernel, pages=pages, lam_init=lam_init, n_steps=n_steps),
        grid_spec=grid_spec,
        out_shape=jax.ShapeDtypeStruct((bs, ls, width), F32),
        compiler_params=_params("parallel", "arbitrary"),
        name="diff_attn_decode",
    )(page_table.reshape(-1), lamv, q_s, k_s, v_s, g_subln.reshape(1, LANES),
      *([cache_k] * pages), *([cache_v] * pages))


def _mem_attn_kernel(q_ref, k_ref, v_ref, g_ref, o_ref):
    g = g_ref[...]
    cols = [slice(h * MEM_HD, (h + 1) * MEM_HD) for h in range(MEM_HEADS)]
    scores = [lax.dot_general(_rmsnorm_lanes(q_ref[:, c], g).astype(BF16),
                              k_ref[:, c].astype(BF16), NT_DIMS, preferred_element_type=F32)
              for c in cols]
    probs = []
    for s in scores:
        s = s * (MEM_HD ** -0.5)
        e = jnp.exp(s - jnp.max(s, axis=-1, keepdims=True))
        probs.append((e / jnp.sum(e, axis=-1, keepdims=True)).astype(BF16))
    for c, p in zip(cols, probs):
        o_ref[:, c] = jnp.dot(p, v_ref[:, c].astype(BF16),
                              preferred_element_type=F32).astype(o_ref.dtype)


def _mem_attn_prompt(proj, q_block0, k_norm, kv, gq, bp, seq):
    mem = k_norm.shape[0] // bp
    tq = _pick_tile(seq, 512, LANES)
    nq = seq // tq
    wide = MEM_HEADS * MEM_HD
    assert q_block0 % MEM_HEADS == 0
    return pl.pallas_call(
        _mem_attn_kernel,
        grid=(bp, nq),
        in_specs=[
            pl.BlockSpec((tq, wide), lambda b, i: (b * nq + i, q_block0 // MEM_HEADS)),
            pl.BlockSpec((mem, wide), lambda b, i: (b, 0)),
            pl.BlockSpec((mem, wide), lambda b, i: (b, 1)),
            pl.BlockSpec((1, LANES), lambda b, i: (0, 0)),
        ],
        out_specs=pl.BlockSpec((tq, wide), lambda b, i: (b * nq + i, 0)),
        out_shape=jax.ShapeDtypeStruct((bp * seq, wide), BF16),
        compiler_params=_params("parallel", "arbitrary"),
        name="mem_attn_prompt",
    )(proj, k_norm, kv, gq.reshape(1, LANES))


def _mem_attn_interleaved_kernel(q_ref, k_ref, v_ref, g_ref, o_ref):
    g = g_ref[...]
    k = k_ref[...].astype(BF16)
    v = v_ref[...].astype(BF16)
    ls, rows = q_ref.shape[0], k.shape[0]
    head_of_row = lax.broadcasted_iota(jnp.int32, (ls, rows), 1) % MEM_HEADS
    cols = [slice(h * MEM_HD, (h + 1) * MEM_HD) for h in range(MEM_HEADS)]
    scores = [lax.dot_general(_rmsnorm_lanes(q_ref[:, c], g).astype(BF16), k, NT_DIMS,
                              preferred_element_type=F32) for c in cols]
    probs = []
    for h, s in enumerate(scores):
        s = jnp.where(head_of_row == h, s * (MEM_HD ** -0.5), -jnp.inf)
        e = jnp.exp(s - jnp.max(s, axis=-1, keepdims=True))
        probs.append((e / jnp.sum(e, axis=-1, keepdims=True)).astype(BF16))
    for c, p in zip(cols, probs):
        o_ref[:, c] = jnp.dot(p, v, preferred_element_type=F32).astype(o_ref.dtype)


def _mem_attn_sample(proj_s, q_block0, cache_k, cache_v, gq, layer):
    bs, ls, _ = proj_s.shape
    mem = cache_k.shape[2]
    wide = MEM_HEADS * MEM_HD
    assert q_block0 % MEM_HEADS == 0
    return pl.pallas_call(
        _mem_attn_interleaved_kernel,
        grid=(bs,),
        in_specs=[
            pl.BlockSpec((None, ls, wide), lambda b: (b, 0, q_block0 // MEM_HEADS)),
            pl.BlockSpec((None, None, mem, MEM_HD), lambda b: (layer, b, 0, 0)),
            pl.BlockSpec((None, None, mem, MEM_HD), lambda b: (layer, b, 0, 0)),
            pl.BlockSpec((1, LANES), lambda b: (0, 0)),
        ],
        out_specs=pl.BlockSpec((None, ls, wide), lambda b: (b, 0, 0)),
        out_shape=jax.ShapeDtypeStruct((bs, ls, wide), F32),
        compiler_params=_params("arbitrary"),
        name="mem_attn_sample",
    )(proj_s, cache_k, cache_v, gq.reshape(1, LANES))


def _mem_k_norm_kernel(k_ref, g_ref, o_ref):
    o_ref[...] = _rmsnorm_lanes(k_ref[...], g_ref[...])


def _mem_k_norm(kv, gk):
    rows = kv.shape[0]
    tm = _pick_tile(rows, 256, 8)
    return pl.pallas_call(
        _mem_k_norm_kernel,
        grid=(rows // tm, MEM_HEADS),
        in_specs=[pl.BlockSpec((tm, LANES), lambda i, h: (i, h)),
                  pl.BlockSpec((1, LANES), lambda i, h: (0, 0))],
        out_specs=pl.BlockSpec((tm, LANES), lambda i, h: (i, h)),
        out_shape=jax.ShapeDtypeStruct((rows, MEM_HEADS * MEM_HD), F32),
        compiler_params=_params("parallel", "arbitrary"),
        name="mem_k_norm",
    )(kv, gk.reshape(1, LANES))


def _split3(x):
    p1 = x.astype(BF16)
    r1 = x - p1.astype(F32)
    p2 = r1.astype(BF16)
    p3 = (r1 - p2.astype(F32)).astype(BF16)
    return p1, p2, p3


def _hgrn_kernel(*refs, chunk, sub, valid, layer, has_state):
    if has_state:
        (q_ref, f_ref, v_ref, gate_ref, lb_ref, g_ref, tri_ref, s0_ref,
         o_ref, s_out_ref, st_ref, b_scr, kk_scr) = refs
    else:
        (q_ref, f_ref, v_ref, gate_ref, lb_ref, g_ref, tri_ref,
         o_ref, s_out_ref, st_ref, b_scr, kk_scr) = refs
    tl = q_ref.shape[0]
    heads = st_ref.shape[0]
    t = pl.program_id(2)

    @pl.when(t == 0)
    def _():
        for hh in range(heads):
            if has_state:
                st_ref[hh] = s0_ref[hh].T
            else:
                st_ref[hh] = jnp.zeros(st_ref.shape[1:], F32)

    lb = lb_ref[...]
    e = jnp.exp(lb - jnp.max(lb, axis=0, keepdims=True))
    sm = e / jnp.sum(e, axis=0, keepdims=True)
    lower = jnp.zeros((1, sm.shape[1]), F32)
    for r in range(1, layer + 1):
        lower = lower + sm[r:r + 1]

    g = g_ref[...]
    n_chunks = tl // chunk
    n_sub = chunk // sub
    chunk_row = lax.broadcasted_iota(jnp.int32, (chunk, 1), 0)

    q_all = q_ref[...]
    v_all = v_ref[...]
    f_all = lower + (1.0 - lower) * jax.nn.sigmoid(f_ref[...])
    kk_all = 1.0 - f_all
    tri = tri_ref[...]
    p1, p2, p3 = _split3(jnp.log(f_all))
    b_all = (jnp.dot(tri, p1, preferred_element_type=F32)
             + jnp.dot(tri, p2, preferred_element_type=F32)
             + jnp.dot(tri, p3, preferred_element_type=F32))
    b_all = b_all * LOG2_E
    v16_all = v_all.astype(BF16)

    units = [(hh, ci) for hh in range(heads) for ci in range(n_chunks)]

    def unit_block(x, unit):
        hh, ci = unit
        return x[ci * chunk:(ci + 1) * chunk, hh * LANES:(hh + 1) * LANES]

    incs, atts, b_lasts = [], [], []
    for u in units:
        q, b, kk = unit_block(q_all, u), unit_block(b_all, u), unit_block(kk_all, u)
        b_last = b[valid - 1:valid]
        k_dec = jnp.where(chunk_row < valid, kk * jnp.exp2(b_last - b), 0.0)
        incs.append(lax.dot_general(unit_block(v16_all, u), k_dec.astype(BF16), TN_DIMS,
                                    preferred_element_type=F32))
        b_lasts.append(b_last)
        for i in range(1, n_sub):
            rs = slice(i * sub, (i + 1) * sub)
            b_ref = b[i * sub - 1:i * sub]
            q_dec = (q[rs] * jnp.exp2(b[rs] - b_ref)).astype(BF16)
            k_dec = (kk[:i * sub] * jnp.exp2(b_ref - b[:i * sub])).astype(BF16)
            atts.append(lax.dot_general(q_dec, k_dec, NT_DIMS, preferred_element_type=F32))

    b_scr[...] = b_all
    kk_scr[...] = kk_all
    causal = [jnp.where(lax.broadcasted_iota(jnp.int32, (sub, LANES), 0) >= s, 0.0, -jnp.inf)
              for s in range(sub)]
    diag = []
    for u in units:
        hh, ci = u
        cols = slice(hh * LANES, (hh + 1) * LANES)
        q, b = unit_block(q_all, u), unit_block(b_all, u)
        for i in range(n_sub):
            rs = slice(i * sub, (i + 1) * sub)
            qb, bb = q[rs], b[rs]
            terms = []
            for s in range(sub):
                row = ci * chunk + i * sub + s
                d = (bb - b_scr[row:row + 1, cols]) + causal[s]
                w = jnp.sum(qb * kk_scr[row:row + 1, cols] * jnp.exp2(d), axis=-1, keepdims=True)
                terms.append(w * v_ref[row:row + 1, cols])
            while len(terms) > 1:
                terms = [a + c for a, c in zip(terms[0::2], terms[1::2])]
            diag.append(terms[0])

    intra = []
    for n, u in enumerate(units):
        v16 = unit_block(v16_all, u)
        for i in range(n_sub):
            o_i = diag[n * n_sub + i]
            if i > 0:
                att = atts[n * (n_sub - 1) + i - 1]
                o_i = o_i + jnp.dot(att.astype(BF16), v16[:i * sub], preferred_element_type=F32)
            intra.append(o_i)

    states = [st_ref[hh] for hh in range(heads)]
    for n, u in enumerate(units):
        hh, ci = u
        q, b = unit_block(q_all, u), unit_block(b_all, u)
        o_inter = lax.dot_general((q * jnp.exp2(b)).astype(BF16), states[hh].astype(BF16),
                                  NT_DIMS, preferred_element_type=F32)
        states[hh] = states[hh] * jnp.exp2(b_lasts[n]) + incs[n]
        parts = intra[n * n_sub:(n + 1) * n_sub]
        o = o_inter + (parts[0] if n_sub == 1 else jnp.concatenate(parts, axis=0))
        rows, cols = slice(ci * chunk, (ci + 1) * chunk), slice(hh * LANES, (hh + 1) * LANES)
        gate = gate_ref[rows, cols]
        o_ref[rows, cols] = (_rmsnorm_lanes(o, g)
                             * (gate * jax.nn.sigmoid(gate))).astype(o_ref.dtype)
    for hh in range(heads):
        st_ref[hh] = states[hh]

    @pl.when(t == pl.num_programs(2) - 1)
    def _():
        for hh in range(heads):
            s_out_ref[hh] = states[hh].T


def _tri(rows, chunk):
    r = jnp.arange(rows)
    same = (r[:, None] // chunk) == (r[None, :] // chunk)
    return (same & (r[:, None] >= r[None, :])).astype(BF16)


def _hgrn_prompt(proj, lb_logits, g_out, bp, seq, layer):
    chunk = math.gcd(seq, B_CHUNK)
    sub = math.gcd(chunk, B_SUB)
    tl = _pick_tile(seq, 256, chunk)
    nt = seq // tl
    h_ = B_HEADS
    hps = HGRN_PROMPT_HEADS_PER_STEP
    assert h_ % hps == 0
    wide = hps * LANES

    def col(block0):
        return pl.BlockSpec((tl, wide), lambda b, h, t: (b * nt + t, block0 // hps + h))

    return pl.pallas_call(
        functools.partial(_hgrn_kernel, chunk=chunk, sub=sub, valid=chunk, layer=layer,
                          has_state=False),
        grid=(bp, h_ // hps, nt),
        in_specs=[col(0), col(h_), col(2 * h_), col(3 * h_),
                  pl.BlockSpec((lb_logits.shape[0], wide), lambda b, h, t: (0, h)),
                  pl.BlockSpec((1, LANES), lambda b, h, t: (0, 0)),
                  pl.BlockSpec((tl, tl), lambda b, h, t: (0, 0))],
        out_specs=[pl.BlockSpec((tl, wide), lambda b, h, t: (b * nt + t, h)),
                   pl.BlockSpec((None, hps, B_DK, B_DV), lambda b, h, t: (b, h, 0, 0))],
        out_shape=[jax.ShapeDtypeStruct((bp * seq, h_ * B_DV), BF16),
                   jax.ShapeDtypeStruct((bp, h_, B_DK, B_DV), F32)],
        scratch_shapes=[pltpu.VMEM((hps, B_DV, B_DK), F32),
                        pltpu.VMEM((tl, wide), F32), pltpu.VMEM((tl, wide), F32)],
        compiler_params=_params("parallel", "parallel", "arbitrary"),
        name="hgrn_prompt",
    )(proj, proj, proj, proj, lb_logits, g_out.reshape(1, LANES), _tri(tl, chunk))


def _hgrn_sample(proj_s, state, lb_logits, g_out, layer, state_layer):
    bs, ls, n = proj_s.shape
    chunk = 8
    assert ls <= chunk
    padded = jnp.pad(proj_s, ((0, 0), (0, chunk - ls), (0, 0)))
    h_ = B_HEADS
    hps = HGRN_SAMPLE_HEADS_PER_STEP
    assert h_ % hps == 0
    wide = hps * LANES

    def col(block0):
        return pl.BlockSpec((None, chunk, wide), lambda b, h, t: (b, 0, block0 // hps + h))

    o, s_new = pl.pallas_call(
        functools.partial(_hgrn_kernel, chunk=chunk, sub=chunk, valid=ls, layer=layer,
                          has_state=True),
        grid=(bs, h_ // hps, 1),
        in_specs=[col(0), col(h_), col(2 * h_), col(3 * h_),
                  pl.BlockSpec((lb_logits.shape[0], wide), lambda b, h, t: (0, h)),
                  pl.BlockSpec((1, LANES), lambda b, h, t: (0, 0)),
                  pl.BlockSpec((chunk, chunk), lambda b, h, t: (0, 0)),
                  pl.BlockSpec((None, None, hps, B_DK, B_DV),
                               lambda b, h, t: (state_layer, b, h, 0, 0))],
        out_specs=[pl.BlockSpec((None, chunk, wide), lambda b, h, t: (b, 0, h)),
                   pl.BlockSpec((None, hps, B_DK, B_DV), lambda b, h, t: (b, h, 0, 0))],
        out_shape=[jax.ShapeDtypeStruct((bs, chunk, h_ * B_DV), F32),
                   jax.ShapeDtypeStruct((bs, h_, B_DK, B_DV), F32)],
        scratch_shapes=[pltpu.VMEM((hps, B_DV, B_DK), F32),
                        pltpu.VMEM((chunk, wide), F32), pltpu.VMEM((chunk, wide), F32)],
        compiler_params=_params("parallel", "parallel", "arbitrary"),
        name="hgrn_sample",
    )(padded, padded, padded, padded, lb_logits, g_out.reshape(1, LANES), _tri(chunk, chunk), state)
    return o[:, :ls], s_new


def _rope_tables(pos):
    half = A_DK // 2
    inv_freq = ROPE_THETA ** (-jnp.arange(half, dtype=F32) / half)
    ang = pos.astype(F32)[:, None] * inv_freq[None, :]
    cos, sin = jnp.cos(ang), jnp.sin(ang)
    reps = LANES // A_DK
    return (jnp.tile(cos, (1, 2 * reps)), jnp.tile(jnp.concatenate([-sin, sin], axis=1), (1, reps)))


def kernel(x_prompt, x_sample, cache_attn_k, cache_attn_v, state_hgrn, cache_mem_k, cache_mem_v,
           page_table, mem_prompt, norm_ffn, w_ffn_gate, w_ffn_up, w_ffn_down, norm_mix, norm_mem,
           w_mem_kv, gq_mem, gk_mem, w_out, w_in_attn, gq_attn, gk_attn, lam_q1, lam_k1, lam_q2,
           lam_k2, g_subln, w_in_hgrn, lb_logits, g_hgrn_out):
    bp, seq, d = x_prompt.shape
    bs, ls, _ = x_sample.shape
    depth = norm_mix.shape[0]
    mem = mem_prompt.shape[1]
    mp = bp * seq
    ms = bs * ls
    past_len = page_table.shape[1] * PAGE_SIZE
    mem_w = MEM_HEADS * MEM_HD
    qk_w = A_HEADS * 2 * A_DK
    v_w = A_HEADS * A_DV

    x = jnp.concatenate([x_prompt.reshape(mp, d), x_sample.reshape(ms, d)], axis=0)
    pos = jnp.concatenate([jnp.tile(jnp.arange(seq), bp), jnp.tile(past_len + jnp.arange(ls), bs)])
    cos, sin_signed = _rope_tables(pos)
    mem_rows = mem_prompt.reshape(bp * mem, d)
    cmk = cache_mem_k.reshape(depth, bs, mem * MEM_HEADS, MEM_HD)
    cmv = cache_mem_v.reshape(depth, bs, mem * MEM_HEADS, MEM_HD)

    k_rows_p, v_rows_p, k_rows_s, v_rows_s = [], [], [], []
    st_p, st_s, mem_k_new, mem_v_new = [], [], [], []
    for i in range(depth):
        x = _ffn_half(x, norm_ffn, w_ffn_gate, w_ffn_up, w_ffn_down, i, 0)

        kv = _norm_matmul(mem_rows, norm_mem, i, w_mem_kv, i)
        k_norm = _mem_k_norm(kv, gk_mem[i])
        mem_k_new.append(k_norm.reshape(bp, mem, MEM_HEADS, MEM_HD))
        mem_v_new.append(kv[:, mem_w:].reshape(bp, mem, MEM_HEADS, MEM_HD))

        if i % 2 == 0:
            a = i // 2
            lam_init = 0.8 - 0.6 * math.exp(-0.3 * i)
            proj = _norm_matmul(x, norm_mix, i, w_in_attn, a)
            q_all = _rot_norm_rows(proj, 0, cos, sin_signed, gq_attn[a], 0, mp + ms,
                                   A_DK ** -0.5 * LOG2_E, BF16)
            k_hm = _rot_norm_head_major(proj, A_HEADS, cos, sin_signed, gk_attn[a], bp, seq)
            k_s = _rot_norm_rows(proj, A_HEADS, cos, sin_signed, gk_attn[a], mp, ms, 1.0, F32)
            lamv = jnp.stack([lam_q1[a], lam_k1[a], lam_q2[a], lam_k2[a]])
            o_p = _diff_attn_prompt(q_all, k_hm, proj, 2 * A_HEADS, lamv, g_subln[a], bp, seq,
                                    lam_init)
            k_rows_p.append(jnp.transpose(k_hm, (0, 2, 1, 3)))
            v_rows_p.append(proj[:mp, 2 * qk_w:2 * qk_w + v_w].reshape(bp, seq, A_HEADS, A_DV))
            proj_s = proj[mp:].reshape(bs, ls, proj.shape[1])
            q_s = q_all[mp:].reshape(bs, ls, qk_w)
            k_s = k_s.reshape(bs, ls, qk_w)
            v_s = proj_s[..., 2 * qk_w:2 * qk_w + v_w]
            o_s = _diff_attn_decode(q_s, k_s, v_s, cache_attn_k, cache_attn_v, page_table, lamv,
                                    g_subln[a], a, lam_init)
            k_rows_s.append(k_s.reshape(bs, ls, A_HEADS, 2 * A_DK))
            v_rows_s.append(v_s.reshape(bs, ls, A_HEADS, A_DV))
            mq_block0 = (2 * qk_w + v_w) // LANES
        else:
            j = i // 2
            proj = _norm_matmul(x, norm_mix, i, w_in_hgrn, j)
            proj_s = proj[mp:].reshape(bs, ls, proj.shape[1])
            o_p, s_p = _hgrn_prompt(proj, lb_logits, g_hgrn_out[j], bp, seq, i)
            o_s, s_s = _hgrn_sample(proj_s, state_hgrn, lb_logits, g_hgrn_out[j], i, j)
            st_p.append(s_p)
            st_s.append(s_s)
            mq_block0 = (2 * B_HEADS * B_DK + 2 * B_HEADS * B_DV) // LANES

        m_p = _mem_attn_prompt(proj, mq_block0, k_norm, kv, gq_mem[i], bp, seq)
        m_s = _mem_attn_sample(proj_s, mq_block0, cmk, cmv, gq_mem[i], i)
        o_all = jnp.concatenate([o_p, o_s.reshape(ms, -1).astype(BF16)], axis=0)
        m_all = jnp.concatenate([m_p, m_s.reshape(ms, -1).astype(BF16)], axis=0)
        x = _out_proj(x, o_all, m_all, w_out, i)

        x = _ffn_half(x, norm_ffn, w_ffn_gate, w_ffn_up, w_ffn_down, i, 1)

    return (x[:mp].reshape(bp, seq, d), x[mp:].reshape(bs, ls, d),
            jnp.stack(k_rows_p), jnp.stack(v_rows_p), jnp.stack(k_rows_s), jnp.stack(v_rows_s),
            jnp.stack(st_p), jnp.stack(st_s), jnp.stack(mem_k_new), jnp.stack(mem_v_new))
```

```python
import functools
import math

import jax
import jax.numpy as jnp
from jax import lax
from jax.experimental import pallas as pl
from jax.experimental.pallas import tpu as pltpu

F32 = jnp.float32
BF16 = jnp.bfloat16

EPS = 1e-6
LOG2_E = math.log2(math.e)
ROPE_THETA = 10000.0
A_HEADS = 12
A_DK = 64
A_DV = 128
B_HEADS = 12
B_DK = 128
B_DV = 128
B_CHUNK = 64
B_SUB = 8
MEM_HEADS = 4
MEM_HD = 128
PAGE_SIZE = 128
LANES = 128
SUBLANES_BF16 = 16
V7X_VMEM_LIMIT_BYTES = 56 * 1024 * 1024
FFN_ROW_TILE = 1024
PROJ_ROW_TILE = 2048
ROT_ROW_TILE = 1024
COL_TILE = 512
FFN_COL_TILE = 256
NORM_ROWS_PER_ITER = 128
DEC_PAGES_PER_STEP = 8
ATTN_TILE = 512
ATTN_HEADS_PER_STEP = 2
HGRN_SAMPLE_HEADS_PER_STEP = 4
HGRN_PROMPT_HEADS_PER_STEP = 4
NT_DIMS = (((1,), (1,)), ((), ()))
TN_DIMS = (((0,), (0,)), ((), ()))


def _params(*semantics):
    return pltpu.CompilerParams(dimension_semantics=semantics,
                                vmem_limit_bytes=V7X_VMEM_LIMIT_BYTES)


def _pick_tile(n, target, align):
    best = None
    for t in range(align, min(n, target) + 1, align):
        if n % t == 0:
            best = t
    assert best is not None, (n, target, align)
    return best


def _rmsnorm_lanes(x, g):
    ms = jnp.mean(x * x, axis=-1, keepdims=True)
    return x * lax.rsqrt(ms + EPS) * g


def _norm_rows_into(h_ref, x_ref, g_ref, copy_ref=None):
    rows = x_ref.shape[0]
    chunk = _pick_tile(rows, NORM_ROWS_PER_ITER, SUBLANES_BF16 if rows % SUBLANES_BF16 == 0 else 8)
    g = g_ref[...]

    def body(i, carry):
        r = pl.multiple_of(i * chunk, chunk)
        x = x_ref[pl.ds(r, chunk), :]
        h_ref[pl.ds(r, chunk), :] = _rmsnorm_lanes(x, g).astype(h_ref.dtype)
        if copy_ref is not None:
            copy_ref[pl.ds(r, chunk), :] = x
        return carry

    lax.fori_loop(0, rows // chunk, body, 0)


def _held_after_first_row_tile(n_col_blocks):
    return lambda i, j: (0, jnp.where(i == 0, j, n_col_blocks - 1))


def _ffn_kernel(xp_ref, xs_ref, g_ref, wg_ref, wu_ref, wd_ref, op_ref, os_ref, hp_ref, hs_ref,
                *, n_out_chunks):
    i, j = pl.program_id(0), pl.program_id(1)

    @pl.when(j == 0)
    def _():
        _norm_rows_into(hp_ref, xp_ref, g_ref, copy_ref=op_ref)

    @pl.when(jnp.logical_and(i == 0, j == 0))
    def _():
        _norm_rows_into(hs_ref, xs_ref, g_ref, copy_ref=os_ref)

    def accumulate(h_ref, o_ref):
        h = h_ref[...]
        gate = jnp.dot(h, wg_ref[...].astype(BF16), preferred_element_type=F32)
        up = jnp.dot(h, wu_ref[...].astype(BF16), preferred_element_type=F32)
        act = (0.5 * (gate * jax.nn.sigmoid(gate)) * up).astype(BF16)
        width = o_ref.shape[1] // n_out_chunks
        for c in range(n_out_chunks):
            cols = slice(c * width, (c + 1) * width)
            o_ref[:, cols] += jnp.dot(act, wd_ref[:, cols].astype(BF16),
                                      preferred_element_type=F32)

    accumulate(hp_ref, op_ref)

    @pl.when(i == 0)
    def _():
        accumulate(hs_ref, os_ref)


def _ffn_half(xp, xs, norm_ffn, w_gate, w_up, w_down, layer, half):
    mp, d = xp.shape
    ms = xs.shape[0]
    f = w_gate.shape[-1]
    tm = _pick_tile(mp, FFN_ROW_TILE, SUBLANES_BF16)
    tf = _pick_tile(f, FFN_COL_TILE, LANES)
    n_out_chunks = max(1, d // 512)
    g4 = norm_ffn.reshape(norm_ffn.shape[0], 2, 1, d)
    return pl.pallas_call(
        functools.partial(_ffn_kernel, n_out_chunks=n_out_chunks),
        grid=(mp // tm, f // tf),
        in_specs=[
            pl.BlockSpec((tm, d), lambda i, j: (i, 0), pipeline_mode=pl.Buffered(1)),
            pl.BlockSpec((ms, d), lambda i, j: (0, 0), pipeline_mode=pl.Buffered(1)),
            pl.BlockSpec((None, None, 1, d), lambda i, j: (layer, half, 0, 0)),
            pl.BlockSpec((None, None, d, tf), lambda i, j: (layer, half, 0, j)),
            pl.BlockSpec((None, None, d, tf), lambda i, j: (layer, half, 0, j)),
            pl.BlockSpec((None, None, tf, d), lambda i, j: (layer, half, j, 0)),
        ],
        out_specs=[pl.BlockSpec((tm, d), lambda i, j: (i, 0)),
                   pl.BlockSpec((ms, d), lambda i, j: (0, 0))],
        out_shape=[jax.ShapeDtypeStruct((mp, d), F32), jax.ShapeDtypeStruct((ms, d), F32)],
        scratch_shapes=[pltpu.VMEM((tm, d), BF16), pltpu.VMEM((ms, d), BF16)],
        compiler_params=_params("arbitrary", "arbitrary"),
        name="ffn_half",
    )(xp, xs, g4, w_gate, w_up, w_down)


def _norm_matmul_kernel(*refs, with_sample):
    if with_sample:
        xp_ref, xs_ref, g_ref, w_ref, op_ref, os_ref, hp_ref, hs_ref = refs
    else:
        xp_ref, g_ref, w_ref, op_ref, hp_ref = refs
    i, j = pl.program_id(0), pl.program_id(1)

    @pl.when(j == 0)
    def _():
        _norm_rows_into(hp_ref, xp_ref, g_ref)

    op_ref[...] = jnp.dot(hp_ref[...], w_ref[...].astype(BF16), preferred_element_type=F32)

    if with_sample:
        @pl.when(jnp.logical_and(i == 0, j == 0))
        def _():
            _norm_rows_into(hs_ref, xs_ref, g_ref)

        @pl.when(i == 0)
        def _():
            os_ref[...] = jnp.dot(hs_ref[...], w_ref[...].astype(BF16),
                                  preferred_element_type=F32)


def _norm_matmul(xp, xs, gains, g_idx, w, w_idx):
    mp, d = xp.shape
    n = w.shape[-1]
    tm = _pick_tile(mp, PROJ_ROW_TILE, SUBLANES_BF16)
    tn = _pick_tile(n, COL_TILE, LANES)
    nn = n // tn
    g3 = gains.reshape(gains.shape[0], 1, d)
    with_sample = xs is not None
    x_spec = pl.BlockSpec((tm, d), lambda i, j: (i, 0), pipeline_mode=pl.Buffered(1))
    g_spec = pl.BlockSpec((None, 1, d), lambda i, j: (g_idx, 0, 0))
    w_spec = pl.BlockSpec((None, d, tn), lambda i, j: (w_idx, 0, j))
    o_spec = pl.BlockSpec((tm, tn), lambda i, j: (i, j))
    o_shape = jax.ShapeDtypeStruct((mp, n), F32)
    if not with_sample:
        return pl.pallas_call(
            functools.partial(_norm_matmul_kernel, with_sample=False),
            grid=(mp // tm, nn),
            in_specs=[x_spec, g_spec, w_spec],
            out_specs=o_spec,
            out_shape=o_shape,
            scratch_shapes=[pltpu.VMEM((tm, d), BF16)],
            compiler_params=_params("parallel", "arbitrary"),
            name="norm_matmul",
        )(xp, g3, w)
    ms = xs.shape[0]
    return pl.pallas_call(
        functools.partial(_norm_matmul_kernel, with_sample=True),
        grid=(mp // tm, nn),
        in_specs=[x_spec,
                  pl.BlockSpec((ms, d), lambda i, j: (0, 0), pipeline_mode=pl.Buffered(1)),
                  g_spec, w_spec],
        out_specs=[o_spec, pl.BlockSpec((ms, tn), _held_after_first_row_tile(nn))],
        out_shape=[o_shape, jax.ShapeDtypeStruct((ms, n), F32)],
        scratch_shapes=[pltpu.VMEM((tm, d), BF16), pltpu.VMEM((ms, d), BF16)],
        compiler_params=_params("arbitrary", "arbitrary"),
        name="norm_matmul",
    )(xp, xs, g3, w)


def _out_proj_kernel(xp_ref, xs_ref, op_ref, os_ref, mp_ref, ms_ref, wa_ref, wb_ref,
                     yp_ref, ys_ref):
    def project(x_ref, o_ref, m_ref, y_ref):
        acc = jnp.dot(o_ref[...].astype(BF16), wa_ref[...].astype(BF16),
                      preferred_element_type=F32)
        acc += jnp.dot(m_ref[...].astype(BF16), wb_ref[...].astype(BF16),
                       preferred_element_type=F32)
        y_ref[...] = x_ref[...] + acc

    project(xp_ref, op_ref, mp_ref, yp_ref)

    @pl.when(pl.program_id(0) == 0)
    def _():
        project(xs_ref, os_ref, ms_ref, ys_ref)


def _out_proj(xp, xs, o_p, o_s, m_p, m_s, w_out, layer):
    mp, d = xp.shape
    ms = xs.shape[0]
    wo, wm = o_p.shape[1], m_p.shape[1]
    assert wo % wm == 0 and wo + wm == w_out.shape[1]
    tm = _pick_tile(mp, PROJ_ROW_TILE, SUBLANES_BF16)
    tn = _pick_tile(d, COL_TILE, LANES)
    nn = d // tn
    held = _held_after_first_row_tile(nn)
    return pl.pallas_call(
        _out_proj_kernel,
        grid=(mp // tm, nn),
        in_specs=[
            pl.BlockSpec((tm, tn), lambda i, j: (i, j)),
            pl.BlockSpec((ms, tn), held),
            pl.BlockSpec((tm, wo), lambda i, j: (i, 0)),
            pl.BlockSpec((ms, wo), lambda i, j: (0, 0)),
            pl.BlockSpec((tm, wm), lambda i, j: (i, 0)),
            pl.BlockSpec((ms, wm), lambda i, j: (0, 0)),
            pl.BlockSpec((None, wo, tn), lambda i, j: (layer, 0, j)),
            pl.BlockSpec((None, wm, tn), lambda i, j: (layer, wo // wm, j)),
        ],
        out_specs=[pl.BlockSpec((tm, tn), lambda i, j: (i, j)),
                   pl.BlockSpec((ms, tn), held)],
        out_shape=[jax.ShapeDtypeStruct((mp, d), F32), jax.ShapeDtypeStruct((ms, d), F32)],
        compiler_params=_params("arbitrary", "arbitrary"),
        name="out_proj",
    )(xp, xs, o_p, o_s, m_p, m_s, w_out, w_out)


def _group_mean(xsq, gm):
    hi = xsq.astype(BF16)
    lo = (xsq - hi.astype(F32)).astype(BF16)
    return (jnp.dot(hi, gm, preferred_element_type=F32)
            + jnp.dot(lo, gm, preferred_element_type=F32))


def _rot_norm_kernel(p_ref, cos_ref, sin_ref, g_ref, gm_ref, o_ref, *, scale):
    x = p_ref[...]
    y = x * lax.rsqrt(_group_mean(x * x, gm_ref[...]) + EPS) * g_ref[...]
    lane = lax.broadcasted_iota(jnp.int32, y.shape, 1)
    lower_half = (lane % A_DK) < (A_DK // 2)
    partner = jnp.where(lower_half,
                        pltpu.roll(y, LANES - A_DK // 2, 1),
                        pltpu.roll(y, A_DK // 2, 1))
    o_ref[...] = ((y * cos_ref[...] + partner * sin_ref[...]) * scale).astype(o_ref.dtype)


def _group_mean_matrix():
    lane = jnp.arange(LANES)
    return jnp.where((lane[:, None] // A_DK) == (lane[None, :] // A_DK), 1.0 / A_DK, 0.0).astype(BF16)


def _rot_norm_rows(proj, block0, cos, sin_signed, gain, scale, dtype):
    rows, period = proj.shape[0], cos.shape[0]
    assert rows % period == 0
    tm = _pick_tile(period, ROT_ROW_TILE, SUBLANES_BF16)
    nt = period // tm
    g = jnp.tile(gain, LANES // A_DK).reshape(1, LANES)
    return pl.pallas_call(
        functools.partial(_rot_norm_kernel, scale=scale),
        grid=(rows // tm, A_HEADS),
        in_specs=[
            pl.BlockSpec((tm, LANES), lambda i, h: (i, block0 + h)),
            pl.BlockSpec((tm, LANES), lambda i, h: (i % nt, 0)),
            pl.BlockSpec((tm, LANES), lambda i, h: (i % nt, 0)),
            pl.BlockSpec((1, LANES), lambda i, h: (0, 0)),
            pl.BlockSpec((LANES, LANES), lambda i, h: (0, 0)),
        ],
        out_specs=pl.BlockSpec((tm, LANES), lambda i, h: (i, h)),
        out_shape=jax.ShapeDtypeStruct((rows, A_HEADS * LANES), dtype),
        compiler_params=_params("parallel", "arbitrary"),
        name="rot_norm_rows",
    )(proj, cos, sin_signed, g, _group_mean_matrix())


def _rot_norm_head_major(proj, block0, cos, sin_signed, gain, bp, seq):
    tr = _pick_tile(seq, ROT_ROW_TILE, 8)
    nt = seq // tr
    g = jnp.tile(gain, LANES // A_DK).reshape(1, LANES)
    return pl.pallas_call(
        functools.partial(_rot_norm_kernel, scale=1.0),
        grid=(bp, nt, A_HEADS),
        in_specs=[
            pl.BlockSpec((tr, LANES), lambda b, i, h: (b * nt + i, block0 + h)),
            pl.BlockSpec((tr, LANES), lambda b, i, h: (i, 0)),
            pl.BlockSpec((tr, LANES), lambda b, i, h: (i, 0)),
            pl.BlockSpec((1, LANES), lambda b, i, h: (0, 0)),
            pl.BlockSpec((LANES, LANES), lambda b, i, h: (0, 0)),
        ],
        out_specs=pl.BlockSpec((None, None, tr, LANES), lambda b, i, h: (b, h, i, 0)),
        out_shape=jax.ShapeDtypeStruct((bp, A_HEADS, seq, LANES), F32),
        compiler_params=_params("parallel", "parallel", "arbitrary"),
        name="rot_norm_head_major",
    )(proj, cos, sin_signed, g, _group_mean_matrix())


def _diff_lambda(lamv, lam_init):
    t1 = jnp.sum(lamv[0:1] * lamv[1:2], axis=-1, keepdims=True)
    t2 = jnp.sum(lamv[2:3] * lamv[3:4], axis=-1, keepdims=True)
    return jnp.exp(t1) - jnp.exp(t2) + lam_init


def _diff_attn_kernel(lamv_ref, q_ref, k_ref, v_ref, g_ref, o_ref, kb_ref, vt_ref, *, lam_init):
    tq = q_ref.shape[0]
    hps, n_kv = vt_ref.shape[:2]
    qi = pl.program_id(2)

    def head_cols(hh):
        return slice(hh * LANES, (hh + 1) * LANES)

    @pl.when(qi == 0)
    def _():
        for hh in range(hps):
            kb_ref[hh] = k_ref[hh].astype(BF16)
            for j in range(n_kv):
                vt_ref[hh, j] = v_ref[j * tq:(j + 1) * tq, head_cols(hh)].T.astype(BF16)

    lane = lax.broadcasted_iota(jnp.int32, (tq, LANES), 1)
    chains, qs = [], []
    for hh in range(hps):
        q = q_ref[:, head_cols(hh)]
        zero = jnp.zeros_like(q)
        for c in range(2):
            chains.append(hh)
            qs.append(jnp.where((lane < A_DK) if c == 0 else (lane >= A_DK), q, zero))
    key = lax.broadcasted_iota(jnp.int32, (tq, tq), 0)
    qry = lax.broadcasted_iota(jnp.int32, (tq, tq), 1)

    def block(j, carry, diagonal):
        r = pl.multiple_of(j * tq, tq)
        scores = [lax.dot_general(kb_ref[hh, pl.ds(r, tq), :], qc, NT_DIMS,
                                  preferred_element_type=F32)
                  for hh, qc in zip(chains, qs)]
        probs, stats = [], []
        for s, (m_prev, l_prev, _) in zip(scores, carry):
            if diagonal:
                s = jnp.where(key <= qry, s, -jnp.inf)
            m_new = jnp.maximum(m_prev, jnp.max(s, axis=0, keepdims=True))
            alpha = jnp.exp2(m_prev - m_new)
            p = jnp.exp2(s - m_new)
            stats.append((m_new, alpha * l_prev + jnp.sum(p, axis=0, keepdims=True), alpha))
            probs.append(p.astype(BF16))
        return tuple(
            (m_new, l_new, alpha * acc + jnp.dot(vt_ref[hh, j], p, preferred_element_type=F32))
            for hh, p, (m_new, l_new, alpha), (_, _, acc) in zip(chains, probs, stats, carry))

    init = tuple((jnp.full((1, tq), -jnp.inf, F32), jnp.zeros((1, tq), F32),
                  jnp.zeros((A_DV, tq), F32)) for _ in chains)
    carry = lax.fori_loop(0, qi, lambda j, c: block(j, c, False), init)
    carry = block(qi, carry, True)

    lam = _diff_lambda(lamv_ref[...], lam_init)
    for hh in range(hps):
        (_, l0, acc0), (_, l1, acc1) = carry[2 * hh], carry[2 * hh + 1]
        o = acc0 / l0 - lam * (acc1 / l1)
        ms = jnp.mean(o * o, axis=0, keepdims=True)
        o = o * lax.rsqrt(ms + EPS) * g_ref[...] * (1.0 - lam_init)
        o_ref[:, head_cols(hh)] = o.T.astype(o_ref.dtype)


def _diff_attn_prompt(q_all, k_hm, proj, v_block0, lamv, g_subln, bp, seq, lam_init):
    tq = _pick_tile(seq, ATTN_TILE, LANES)
    nq = seq // tq
    hps = ATTN_HEADS_PER_STEP
    assert A_HEADS % hps == 0 and v_block0 % hps == 0
    wide = hps * LANES
    return pl.pallas_call(
        functools.partial(_diff_attn_kernel, lam_init=lam_init),
        grid=(bp, A_HEADS // hps, nq),
        in_specs=[
            pl.BlockSpec(lamv.shape, lambda b, h, i: (0, 0)),
            pl.BlockSpec((tq, wide), lambda b, h, i: (b * nq + i, h)),
            pl.BlockSpec((None, hps, seq, LANES), lambda b, h, i: (b, h, 0, 0)),
            pl.BlockSpec((seq, wide), lambda b, h, i: (b, v_block0 // hps + h)),
            pl.BlockSpec((A_DV, 1), lambda b, h, i: (0, 0)),
        ],
        out_specs=pl.BlockSpec((tq, wide), lambda b, h, i: (b * nq + i, h)),
        out_shape=jax.ShapeDtypeStruct((bp * seq, A_HEADS * A_DV), BF16),
        scratch_shapes=[pltpu.VMEM((hps, seq, LANES), BF16),
                        pltpu.VMEM((hps, nq, A_DV, tq), BF16)],
        compiler_params=_params("parallel", "parallel", "arbitrary"),
        name="diff_attn_prompt",
    )(lamv, q_all, k_hm, proj, g_subln.reshape(A_DV, 1))


def _diff_attn_decode_kernel(pt_ref, lamv_ref, q_ref, ks_ref, vs_ref, g_ref, *rest,
                             pages, lam_init, n_steps):
    del pt_ref
    k_refs = rest[:pages]
    v_refs = rest[pages:2 * pages]
    o_ref = rest[2 * pages]
    qh_ref, m_ref, l_ref, acc_ref, kb_ref, vb_ref = rest[2 * pages + 1:]
    ls = q_ref.shape[0]
    rph = 2 * ls
    rows = A_HEADS * rph
    step = pl.program_id(1)

    def head_cols(h):
        return slice(h * A_DV, (h + 1) * A_DV)

    def per_head_rows(x):
        return jnp.concatenate([x[:, head_cols(h)] for h in range(A_HEADS) for _ in range(2)],
                               axis=0)

    @pl.when(step == 0)
    def _():
        q = q_ref[...].astype(F32)
        lane = lax.broadcasted_iota(jnp.int32, (ls, LANES), 1)
        for h in range(A_HEADS):
            qh = q[:, head_cols(h)]
            qh_ref[h * rph:h * rph + ls, :] = jnp.where(lane < A_DK, qh, 0.0)
            qh_ref[h * rph + ls:(h + 1) * rph, :] = jnp.where(lane >= A_DK, qh, 0.0)
        m_ref[...] = jnp.full(m_ref.shape, -jnp.inf, F32)
        l_ref[...] = jnp.zeros(l_ref.shape, F32)
        acc_ref[...] = jnp.zeros(acc_ref.shape, F32)

    for h in range(A_HEADS):
        for p in range(pages):
            tok = slice(p * PAGE_SIZE, (p + 1) * PAGE_SIZE)
            kb_ref[h, tok, :] = k_refs[p][h].astype(BF16)
            vb_ref[h, tok, :] = v_refs[p][h].astype(BF16)

    s = jnp.concatenate(
        [lax.dot_general(qh_ref[h * rph:(h + 1) * rph, :].astype(BF16), kb_ref[h], NT_DIMS,
                         preferred_element_type=F32) for h in range(A_HEADS)], axis=0)
    m_prev = m_ref[...]
    m_new = jnp.maximum(m_prev, jnp.max(s, axis=-1, keepdims=True))
    alpha = jnp.exp2(m_prev - m_new)
    p_exp = jnp.exp2(s - m_new)
    l_ref[...] = alpha * l_ref[...] + jnp.sum(p_exp, axis=-1, keepdims=True)
    pv = jnp.concatenate(
        [jnp.dot(p_exp[h * rph:(h + 1) * rph].astype(BF16), vb_ref[h],
                 preferred_element_type=F32) for h in range(A_HEADS)], axis=0)
    acc_ref[...] = alpha * acc_ref[...] + pv
    m_ref[...] = m_new

    @pl.when(step == n_steps - 1)
    def _():
        qf = qh_ref[...]
        ks = ks_ref[...].astype(BF16).astype(F32)
        vs = vs_ref[...].astype(BF16).astype(F32)
        row = lax.broadcasted_iota(jnp.int32, (rows, 1), 0)
        q_of_row = row % ls
        s_new = []
        for t in range(ls):
            k_t = per_head_rows(jnp.broadcast_to(ks[t:t + 1], ks.shape))
            st = jnp.sum(qf * k_t, axis=-1, keepdims=True)
            s_new.append(jnp.where(q_of_row >= t, st, -jnp.inf))
        m_old = m_ref[...]
        m_fin = functools.reduce(jnp.maximum, s_new, m_old)
        a_fin = jnp.exp2(m_old - m_fin)
        l_fin = a_fin * l_ref[...]
        acc = a_fin * acc_ref[...]
        for t in range(ls):
            pt = jnp.exp2(s_new[t] - m_fin)
            l_fin = l_fin + pt
            v_t = per_head_rows(jnp.broadcast_to(vs[t:t + 1], vs.shape))
            acc = acc + pt.astype(BF16).astype(F32) * v_t

        lam = _diff_lambda(lamv_ref[...], lam_init)
        second = (row % rph) >= ls
        acc = acc * (jnp.where(second, -lam, 1.0) / l_fin)
        g = g_ref[...]
        for h in range(A_HEADS):
            o = acc[h * rph:h * rph + ls] + acc[h * rph + ls:(h + 1) * rph]
            o_ref[:, head_cols(h)] = _rmsnorm_lanes(o, g) * (1.0 - lam_init)


def _diff_attn_decode(q_s, k_s, v_s, cache_k, cache_v, page_table, lamv, g_subln, layer, lam_init):
    bs, ls, width = q_s.shape
    n_pages = page_table.shape[1]
    pages = _pick_tile(n_pages, DEC_PAGES_PER_STEP, 1)
    n_steps = n_pages // pages
    rows = ls * 2 * A_HEADS
    assert (2 * ls) % 8 == 0 and cache_k.shape[2:] == (PAGE_SIZE, A_HEADS, 2 * A_DK)

    cache_k = jnp.transpose(cache_k, (0, 1, 3, 2, 4))
    cache_v = jnp.transpose(cache_v, (0, 1, 3, 2, 4))

    def page_spec(p):
        return pl.BlockSpec((None, None, A_HEADS, PAGE_SIZE, LANES),
                            lambda b, s, pt: (layer, pt[b * n_pages + s * pages + p], 0, 0, 0))

    def per_batch():
        return pl.BlockSpec((None, ls, width), lambda b, s, pt: (b, 0, 0))

    grid_spec = pltpu.PrefetchScalarGridSpec(
        num_scalar_prefetch=1,
        grid=(bs, n_steps),
        in_specs=[pl.BlockSpec(lamv.shape, lambda b, s, pt: (0, 0)),
                  per_batch(), per_batch(), per_batch(),
                  pl.BlockSpec((1, LANES), lambda b, s, pt: (0, 0))]
                 + [page_spec(p) for p in range(pages)]
                 + [page_spec(p) for p in range(pages)],
        out_specs=per_batch(),
        scratch_shapes=[pltpu.VMEM((rows, LANES), F32),
                        pltpu.VMEM((rows, 1), F32),
                        pltpu.VMEM((rows, 1), F32),
                        pltpu.VMEM((rows, A_DV), F32),
                        pltpu.VMEM((A_HEADS, pages * PAGE_SIZE, LANES), BF16),
                        pltpu.VMEM((A_HEADS, pages * PAGE_SIZE, A_DV), BF16)],
    )
    return pl.pallas_call(
        functools.partial(_diff_attn_decode_kernel, pages=pages, lam_init=lam_init, n_steps=n_steps),
        grid_spec=grid_spec,
        out_shape=jax.ShapeDtypeStruct((bs, ls, width), F32),
        compiler_params=_params("parallel", "arbitrary"),
        name="diff_attn_decode",
    )(page_table.reshape(-1), lamv, q_s, k_s, v_s, g_subln.reshape(1, LANES),
      *([cache_k] * pages), *([cache_v] * pages))


def _mem_attn_kernel(q_ref, k_ref, v_ref, g_ref, o_ref):
    g = g_ref[...]
    cols = [slice(h * MEM_HD, (h + 1) * MEM_HD) for h in range(MEM_HEADS)]
    scores = [lax.dot_general(_rmsnorm_lanes(q_ref[:, c], g).astype(BF16),
                              k_ref[:, c].astype(BF16), NT_DIMS, preferred_element_type=F32)
              for c in cols]
    probs = []
    for s in scores:
        s = s * (MEM_HD ** -0.5)
        e = jnp.exp(s - jnp.max(s, axis=-1, keepdims=True))
        probs.append((e / jnp.sum(e, axis=-1, keepdims=True)).astype(BF16))
    for c, p in zip(cols, probs):
        o_ref[:, c] = jnp.dot(p, v_ref[:, c].astype(BF16),
                              preferred_element_type=F32).astype(o_ref.dtype)


def _mem_attn_prompt(proj, q_block0, k_norm, kv, gq, bp, seq):
    mem = k_norm.shape[0] // bp
    tq = _pick_tile(seq, 512, LANES)
    nq = seq // tq
    wide = MEM_HEADS * MEM_HD
    assert q_block0 % MEM_HEADS == 0
    return pl.pallas_call(
        _mem_attn_kernel,
        grid=(bp, nq),
        in_specs=[
            pl.BlockSpec((tq, wide), lambda b, i: (b * nq + i, q_block0 // MEM_HEADS)),
            pl.BlockSpec((mem, wide), lambda b, i: (b, 0)),
            pl.BlockSpec((mem, wide), lambda b, i: (b, 1)),
            pl.BlockSpec((1, LANES), lambda b, i: (0, 0)),
        ],
        out_specs=pl.BlockSpec((tq, wide), lambda b, i: (b * nq + i, 0)),
        out_shape=jax.ShapeDtypeStruct((bp * seq, wide), BF16),
        compiler_params=_params("parallel", "arbitrary"),
        name="mem_attn_prompt",
    )(proj, k_norm, kv, gq.reshape(1, LANES))


def _mem_attn_interleaved_kernel(q_ref, k_ref, v_ref, g_ref, o_ref):
    g = g_ref[...]
    k = k_ref[...].astype(BF16)
    v = v_ref[...].astype(BF16)
    ls, rows = q_ref.shape[0], k.shape[0]
    head_of_row = lax.broadcasted_iota(jnp.int32, (ls, rows), 1) % MEM_HEADS
    cols = [slice(h * MEM_HD, (h + 1) * MEM_HD) for h in range(MEM_HEADS)]
    scores = [lax.dot_general(_rmsnorm_lanes(q_ref[:, c], g).astype(BF16), k, NT_DIMS,
                              preferred_element_type=F32) for c in cols]
    probs = []
    for h, s in enumerate(scores):
        s = jnp.where(head_of_row == h, s * (MEM_HD ** -0.5), -jnp.inf)
        e = jnp.exp(s - jnp.max(s, axis=-1, keepdims=True))
        probs.append((e / jnp.sum(e, axis=-1, keepdims=True)).astype(BF16))
    for c, p in zip(cols, probs):
        o_ref[:, c] = jnp.dot(p, v, preferred_element_type=F32).astype(o_ref.dtype)


def _mem_attn_sample(proj_s, q_block0, cache_k, cache_v, gq, layer):
    bs, ls, _ = proj_s.shape
    mem = cache_k.shape[2]
    wide = MEM_HEADS * MEM_HD
    assert q_block0 % MEM_HEADS == 0
    return pl.pallas_call(
        _mem_attn_interleaved_kernel,
        grid=(bs,),
        in_specs=[
            pl.BlockSpec((None, ls, wide), lambda b: (b, 0, q_block0 // MEM_HEADS)),
            pl.BlockSpec((None, None, mem, MEM_HD), lambda b: (layer, b, 0, 0)),
            pl.BlockSpec((None, None, mem, MEM_HD), lambda b: (layer, b, 0, 0)),
            pl.BlockSpec((1, LANES), lambda b: (0, 0)),
        ],
        out_specs=pl.BlockSpec((None, ls, wide), lambda b: (b, 0, 0)),
        out_shape=jax.ShapeDtypeStruct((bs, ls, wide), F32),
        compiler_params=_params("arbitrary"),
        name="mem_attn_sample",
    )(proj_s, cache_k, cache_v, gq.reshape(1, LANES))


def _mem_k_norm_kernel(k_ref, g_ref, o_ref):
    o_ref[...] = _rmsnorm_lanes(k_ref[...], g_ref[...])


def _mem_k_norm(kv, gk):
    rows = kv.shape[0]
    tm = _pick_tile(rows, 256, 8)
    return pl.pallas_call(
        _mem_k_norm_kernel,
        grid=(rows // tm, MEM_HEADS),
        in_specs=[pl.BlockSpec((tm, LANES), lambda i, h: (i, h)),
                  pl.BlockSpec((1, LANES), lambda i, h: (0, 0))],
        out_specs=pl.BlockSpec((tm, LANES), lambda i, h: (i, h)),
        out_shape=jax.ShapeDtypeStruct((rows, MEM_HEADS * MEM_HD), F32),
        compiler_params=_params("parallel", "arbitrary"),
        name="mem_k_norm",
    )(kv, gk.reshape(1, LANES))


def _split3(x):
    p1 = x.astype(BF16)
    r1 = x - p1.astype(F32)
    p2 = r1.astype(BF16)
    p3 = (r1 - p2.astype(F32)).astype(BF16)
    return p1, p2, p3


def _hgrn_kernel(*refs, chunk, sub, valid, layer, has_state):
    if has_state:
        (q_ref, f_ref, v_ref, gate_ref, lb_ref, g_ref, tri_ref, s0_ref,
         o_ref, s_out_ref, st_ref, b_scr, kk_scr) = refs
    else:
        (q_ref, f_ref, v_ref, gate_ref, lb_ref, g_ref, tri_ref,
         o_ref, s_out_ref, st_ref, b_scr, kk_scr) = refs
    tl = q_ref.shape[0]
    heads = st_ref.shape[0]
    t = pl.program_id(2)

    @pl.when(t == 0)
    def _():
        for hh in range(heads):
            if has_state:
                st_ref[hh] = s0_ref[hh].T
            else:
                st_ref[hh] = jnp.zeros(st_ref.shape[1:], F32)

    lb = lb_ref[...]
    e = jnp.exp(lb - jnp.max(lb, axis=0, keepdims=True))
    sm = e / jnp.sum(e, axis=0, keepdims=True)
    lower = jnp.zeros((1, sm.shape[1]), F32)
    for r in range(1, layer + 1):
        lower = lower + sm[r:r + 1]

    g = g_ref[...]
    n_chunks = tl // chunk
    n_sub = chunk // sub
    chunk_row = lax.broadcasted_iota(jnp.int32, (chunk, 1), 0)

    q_all = q_ref[...]
    v_all = v_ref[...]
    f_all = lower + (1.0 - lower) * jax.nn.sigmoid(f_ref[...])
    kk_all = 1.0 - f_all
    tri = tri_ref[...]
    p1, p2, p3 = _split3(jnp.log(f_all))
    b_all = (jnp.dot(tri, p1, preferred_element_type=F32)
             + jnp.dot(tri, p2, preferred_element_type=F32)
             + jnp.dot(tri, p3, preferred_element_type=F32))
    b_all = b_all * LOG2_E
    v16_all = v_all.astype(BF16)

    units = [(hh, ci) for hh in range(heads) for ci in range(n_chunks)]

    def unit_block(x, unit):
        hh, ci = unit
        return x[ci * chunk:(ci + 1) * chunk, hh * LANES:(hh + 1) * LANES]

    incs, atts, b_lasts = [], [], []
    for u in units:
        q, b, kk = unit_block(q_all, u), unit_block(b_all, u), unit_block(kk_all, u)
        b_last = b[valid - 1:valid]
        k_dec = jnp.where(chunk_row < valid, kk * jnp.exp2(b_last - b), 0.0)
        incs.append(lax.dot_general(unit_block(v16_all, u), k_dec.astype(BF16), TN_DIMS,
                                    preferred_element_type=F32))
        b_lasts.append(b_last)
        for i in range(1, n_sub):
            rs = slice(i * sub, (i + 1) * sub)
            b_ref = b[i * sub - 1:i * sub]
            q_dec = (q[rs] * jnp.exp2(b[rs] - b_ref)).astype(BF16)
            k_dec = (kk[:i * sub] * jnp.exp2(b_ref - b[:i * sub])).astype(BF16)
            atts.append(lax.dot_general(q_dec, k_dec, NT_DIMS, preferred_element_type=F32))

    b_scr[...] = b_all
    kk_scr[...] = kk_all
    causal = [jnp.where(lax.broadcasted_iota(jnp.int32, (sub, LANES), 0) >= s, 0.0, -jnp.inf)
              for s in range(sub)]
    diag = []
    for u in units:
        hh, ci = u
        cols = slice(hh * LANES, (hh + 1) * LANES)
        q, b = unit_block(q_all, u), unit_block(b_all, u)
        for i in range(n_sub):
            rs = slice(i * sub, (i + 1) * sub)
            qb, bb = q[rs], b[rs]
            terms = []
            for s in range(sub):
                row = ci * chunk + i * sub + s
                d = (bb - b_scr[row:row + 1, cols]) + causal[s]
                w = jnp.sum(qb * kk_scr[row:row + 1, cols] * jnp.exp2(d), axis=-1, keepdims=True)
                terms.append(w * v_ref[row:row + 1, cols])
            while len(terms) > 1:
                terms = [a + c for a, c in zip(terms[0::2], terms[1::2])]
            diag.append(terms[0])

    intra = []
    for n, u in enumerate(units):
        v16 = unit_block(v16_all, u)
        for i in range(n_sub):
            o_i = diag[n * n_sub + i]
            if i > 0:
                att = atts[n * (n_sub - 1) + i - 1]
                o_i = o_i + jnp.dot(att.astype(BF16), v16[:i * sub], preferred_element_type=F32)
            intra.append(o_i)

    states = [st_ref[hh] for hh in range(heads)]
    for n, u in enumerate(units):
        hh, ci = u
        q, b = unit_block(q_all, u), unit_block(b_all, u)
        o_inter = lax.dot_general((q * jnp.exp2(b)).astype(BF16), states[hh].astype(BF16),
                                  NT_DIMS, preferred_element_type=F32)
        states[hh] = states[hh] * jnp.exp2(b_lasts[n]) + incs[n]
        parts = intra[n * n_sub:(n + 1) * n_sub]
        o = o_inter + (parts[0] if n_sub == 1 else jnp.concatenate(parts, axis=0))
        rows, cols = slice(ci * chunk, (ci + 1) * chunk), slice(hh * LANES, (hh + 1) * LANES)
        gate = gate_ref[rows, cols]
        o_ref[rows, cols] = (_rmsnorm_lanes(o, g)
                             * (gate * jax.nn.sigmoid(gate))).astype(o_ref.dtype)
    for hh in range(heads):
        st_ref[hh] = states[hh]

    @pl.when(t == pl.num_programs(2) - 1)
    def _():
        for hh in range(heads):
            s_out_ref[hh] = states[hh].T


def _tri(rows, chunk):
    r = jnp.arange(rows)
    same = (r[:, None] // chunk) == (r[None, :] // chunk)
    return (same & (r[:, None] >= r[None, :])).astype(BF16)


def _hgrn_prompt(proj, lb_logits, g_out, bp, seq, layer):
    chunk = math.gcd(seq, B_CHUNK)
    sub = math.gcd(chunk, B_SUB)
    tl = _pick_tile(seq, 256, chunk)
    nt = seq // tl
    h_ = B_HEADS
    hps = HGRN_PROMPT_HEADS_PER_STEP
    assert h_ % hps == 0
    wide = hps * LANES

    def col(block0):
        return pl.BlockSpec((tl, wide), lambda b, h, t: (b * nt + t, block0 // hps + h))

    return pl.pallas_call(
        functools.partial(_hgrn_kernel, chunk=chunk, sub=sub, valid=chunk, layer=layer,
                          has_state=False),
        grid=(bp, h_ // hps, nt),
        in_specs=[col(0), col(h_), col(2 * h_), col(3 * h_),
                  pl.BlockSpec((lb_logits.shape[0], wide), lambda b, h, t: (0, h)),
                  pl.BlockSpec((1, LANES), lambda b, h, t: (0, 0)),
                  pl.BlockSpec((tl, tl), lambda b, h, t: (0, 0))],
        out_specs=[pl.BlockSpec((tl, wide), lambda b, h, t: (b * nt + t, h)),
                   pl.BlockSpec((None, hps, B_DK, B_DV), lambda b, h, t: (b, h, 0, 0))],
        out_shape=[jax.ShapeDtypeStruct((bp * seq, h_ * B_DV), BF16),
                   jax.ShapeDtypeStruct((bp, h_, B_DK, B_DV), F32)],
        scratch_shapes=[pltpu.VMEM((hps, B_DV, B_DK), F32),
                        pltpu.VMEM((tl, wide), F32), pltpu.VMEM((tl, wide), F32)],
        compiler_params=_params("parallel", "parallel", "arbitrary"),
        name="hgrn_prompt",
    )(proj, proj, proj, proj, lb_logits, g_out.reshape(1, LANES), _tri(tl, chunk))


def _hgrn_sample(proj_s, state, lb_logits, g_out, layer, state_layer):
    bs, ls, n = proj_s.shape
    chunk = 8
    assert ls <= chunk
    padded = jnp.pad(proj_s, ((0, 0), (0, chunk - ls), (0, 0)))
    h_ = B_HEADS
    hps = HGRN_SAMPLE_HEADS_PER_STEP
    assert h_ % hps == 0
    wide = hps * LANES

    def col(block0):
        return pl.BlockSpec((None, chunk, wide), lambda b, h, t: (b, 0, block0 // hps + h))

    o, s_new = pl.pallas_call(
        functools.partial(_hgrn_kernel, chunk=chunk, sub=chunk, valid=ls, layer=layer,
                          has_state=True),
        grid=(bs, h_ // hps, 1),
        in_specs=[col(0), col(h_), col(2 * h_), col(3 * h_),
                  pl.BlockSpec((lb_logits.shape[0], wide), lambda b, h, t: (0, h)),
                  pl.BlockSpec((1, LANES), lambda b, h, t: (0, 0)),
                  pl.BlockSpec((chunk, chunk), lambda b, h, t: (0, 0)),
                  pl.BlockSpec((None, None, hps, B_DK, B_DV),
                               lambda b, h, t: (state_layer, b, h, 0, 0))],
        out_specs=[pl.BlockSpec((None, chunk, wide), lambda b, h, t: (b, 0, h)),
                   pl.BlockSpec((None, hps, B_DK, B_DV), lambda b, h, t: (b, h, 0, 0))],
        out_shape=[jax.ShapeDtypeStruct((bs, chunk, h_ * B_DV), F32),
                   jax.ShapeDtypeStruct((bs, h_, B_DK, B_DV), F32)],
        scratch_shapes=[pltpu.VMEM((hps, B_DV, B_DK), F32),
                        pltpu.VMEM((chunk, wide), F32), pltpu.VMEM((chunk, wide), F32)],
        compiler_params=_params("parallel", "parallel", "arbitrary"),
        name="hgrn_sample",
    )(padded, padded, padded, padded, lb_logits, g_out.reshape(1, LANES), _tri(chunk, chunk), state)
    return o[:, :ls], s_new


def _rope_tables(pos):
    half = A_DK // 2
    inv_freq = ROPE_THETA ** (-jnp.arange(half, dtype=F32) / half)
    ang = pos.astype(F32)[:, None] * inv_freq[None, :]
    cos, sin = jnp.cos(ang), jnp.sin(ang)
    reps = LANES // A_DK
    return (jnp.tile(cos, (1, 2 * reps)), jnp.tile(jnp.concatenate([-sin, sin], axis=1), (1, reps)))


def kernel(x_prompt, x_sample, cache_attn_k, cache_attn_v, state_hgrn, cache_mem_k, cache_mem_v,
           page_table, mem_prompt, norm_ffn, w_ffn_gate, w_ffn_up, w_ffn_down, norm_mix, norm_mem,
           w_mem_kv, gq_mem, gk_mem, w_out, w_in_attn, gq_attn, gk_attn, lam_q1, lam_k1, lam_q2,
           lam_k2, g_subln, w_in_hgrn, lb_logits, g_hgrn_out):
    bp, seq, d = x_prompt.shape
    bs, ls, _ = x_sample.shape
    depth = norm_mix.shape[0]
    mem = mem_prompt.shape[1]
    mp = bp * seq
    ms = bs * ls
    past_len = page_table.shape[1] * PAGE_SIZE
    mem_w = MEM_HEADS * MEM_HD
    qk_w = A_HEADS * 2 * A_DK
    v_w = A_HEADS * A_DV

    xp = x_prompt.reshape(mp, d)
    xs = x_sample.reshape(ms, d)
    cos_p, sin_p = _rope_tables(jnp.arange(seq))
    cos_s, sin_s = _rope_tables(jnp.tile(past_len + jnp.arange(ls), bs))
    mem_rows = mem_prompt.reshape(bp * mem, d)
    cmk = cache_mem_k.reshape(depth, bs, mem * MEM_HEADS, MEM_HD)
    cmv = cache_mem_v.reshape(depth, bs, mem * MEM_HEADS, MEM_HD)

    k_rows_p, v_rows_p, k_rows_s, v_rows_s = [], [], [], []
    st_p, st_s, mem_k_new, mem_v_new = [], [], [], []
    for i in range(depth):
        xp, xs = _ffn_half(xp, xs, norm_ffn, w_ffn_gate, w_ffn_up, w_ffn_down, i, 0)

        kv = _norm_matmul(mem_rows, None, norm_mem, i, w_mem_kv, i)
        k_norm = _mem_k_norm(kv, gk_mem[i])
        mem_k_new.append(k_norm.reshape(bp, mem, MEM_HEADS, MEM_HD))
        mem_v_new.append(kv[:, mem_w:].reshape(bp, mem, MEM_HEADS, MEM_HD))

        if i % 2 == 0:
            a = i // 2
            lam_init = 0.8 - 0.6 * math.exp(-0.3 * i)
            proj, proj_s2 = _norm_matmul(xp, xs, norm_mix, i, w_in_attn, a)
            q_scale = A_DK ** -0.5 * LOG2_E
            q_p = _rot_norm_rows(proj, 0, cos_p, sin_p, gq_attn[a], q_scale, BF16)
            q_s = _rot_norm_rows(proj_s2, 0, cos_s, sin_s, gq_attn[a], q_scale, BF16)
            k_hm = _rot_norm_head_major(proj, A_HEADS, cos_p, sin_p, gk_attn[a], bp, seq)
            k_s = _rot_norm_rows(proj_s2, A_HEADS, cos_s, sin_s, gk_attn[a], 1.0, F32)
            lamv = jnp.stack([lam_q1[a], lam_k1[a], lam_q2[a], lam_k2[a]])
            o_p = _diff_attn_prompt(q_p, k_hm, proj, 2 * A_HEADS, lamv, g_subln[a], bp, seq,
                                    lam_init)
            k_rows_p.append(jnp.transpose(k_hm, (0, 2, 1, 3)))
            v_rows_p.append(proj[:, 2 * qk_w:2 * qk_w + v_w].reshape(bp, seq, A_HEADS, A_DV))
            proj_s = proj_s2.reshape(bs, ls, proj_s2.shape[1])
            q_s = q_s.reshape(bs, ls, qk_w)
            k_s = k_s.reshape(bs, ls, qk_w)
            v_s = proj_s[..., 2 * qk_w:2 * qk_w + v_w]
            o_s = _diff_attn_decode(q_s, k_s, v_s, cache_attn_k, cache_attn_v, page_table, lamv,
                                    g_subln[a], a, lam_init)
            k_rows_s.append(k_s.reshape(bs, ls, A_HEADS, 2 * A_DK))
            v_rows_s.append(v_s.reshape(bs, ls, A_HEADS, A_DV))
            mq_block0 = (2 * qk_w + v_w) // LANES
        else:
            j = i // 2
            proj, proj_s2 = _norm_matmul(xp, xs, norm_mix, i, w_in_hgrn, j)
            proj_s = proj_s2.reshape(bs, ls, proj_s2.shape[1])
            o_p, s_p = _hgrn_prompt(proj, lb_logits, g_hgrn_out[j], bp, seq, i)
            o_s, s_s = _hgrn_sample(proj_s, state_hgrn, lb_logits, g_hgrn_out[j], i, j)
            st_p.append(s_p)
            st_s.append(s_s)
            mq_block0 = (2 * B_HEADS * B_DK + 2 * B_HEADS * B_DV) // LANES

        m_p = _mem_attn_prompt(proj, mq_block0, k_norm, kv, gq_mem[i], bp, seq)
        m_s = _mem_attn_sample(proj_s, mq_block0, cmk, cmv, gq_mem[i], i)
        xp, xs = _out_proj(xp, xs, o_p, o_s.reshape(ms, -1), m_p, m_s.reshape(ms, -1), w_out, i)

        xp, xs = _ffn_half(xp, xs, norm_ffn, w_ffn_gate, w_ffn_up, w_ffn_down, i, 1)

    return (xp.reshape(bp, seq, d), xs.reshape(bs, ls, d),
            jnp.stack(k_rows_p), jnp.stack(v_rows_p), jnp.stack(k_rows_s), jnp.stack(v_rows_s),
            jnp.stack(st_p), jnp.stack(st_s), jnp.stack(mem_k_new), jnp.stack(mem_v_new))
```

```python
import functools
import math

import jax
import jax.numpy as jnp
from jax import lax
from jax.experimental import pallas as pl
from jax.experimental.pallas import tpu as pltpu

F32 = jnp.float32
BF16 = jnp.bfloat16

EPS = 1e-6
LOG2_E = math.log2(math.e)
ROPE_THETA = 10000.0
A_HEADS = 12
A_DK = 64
A_DV = 128
B_HEADS = 12
B_DK = 128
B_DV = 128
B_CHUNK = 64
B_SUB = 8
MEM_HEADS = 4
MEM_HD = 128
PAGE_SIZE = 128
LANES = 128
SUBLANES_BF16 = 16
V7X_VMEM_LIMIT_BYTES = 56 * 1024 * 1024
FFN_ROW_TILE = 1024
PROJ_ROW_TILE = 2048
ROT_ROW_TILE = 1024
COL_TILE = 512
FFN_COL_TILE = 256
NORM_ROWS_PER_ITER = 128
DEC_PAGES_PER_STEP = 8
ATTN_TILE = 512
ATTN_HEADS_PER_STEP = 3
HGRN_SAMPLE_HEADS_PER_STEP = 12
HGRN_PROMPT_HEADS_PER_STEP = 4
NT_DIMS = (((1,), (1,)), ((), ()))
TN_DIMS = (((0,), (0,)), ((), ()))


def _params(*semantics):
    return pltpu.CompilerParams(dimension_semantics=semantics,
                                vmem_limit_bytes=V7X_VMEM_LIMIT_BYTES)


def _pick_tile(n, target, align):
    best = None
    for t in range(align, min(n, target) + 1, align):
        if n % t == 0:
            best = t
    assert best is not None, (n, target, align)
    return best


def _rmsnorm_lanes(x, g):
    ms = jnp.mean(x * x, axis=-1, keepdims=True)
    return x * lax.rsqrt(ms + EPS) * g


def _norm_rows_into(h_ref, x_ref, g_ref, copy_ref=None):
    rows = x_ref.shape[0]
    chunk = _pick_tile(rows, NORM_ROWS_PER_ITER, SUBLANES_BF16 if rows % SUBLANES_BF16 == 0 else 8)
    g = g_ref[...]

    def body(i, carry):
        r = pl.multiple_of(i * chunk, chunk)
        x = x_ref[pl.ds(r, chunk), :]
        h_ref[pl.ds(r, chunk), :] = _rmsnorm_lanes(x, g).astype(h_ref.dtype)
        if copy_ref is not None:
            copy_ref[pl.ds(r, chunk), :] = x
        return carry

    lax.fori_loop(0, rows // chunk, body, 0)


def _held_after_first_row_tile(n_col_blocks):
    return lambda i, j: (0, jnp.where(i == 0, j, n_col_blocks - 1))


def _ffn_kernel(xp_ref, xs_ref, g_ref, wg_ref, wu_ref, wd_ref, op_ref, os_ref, hp_ref, hs_ref,
                *, n_out_chunks):
    i, j = pl.program_id(0), pl.program_id(1)

    @pl.when(j == 0)
    def _():
        _norm_rows_into(hp_ref, xp_ref, g_ref, copy_ref=op_ref)

    @pl.when(jnp.logical_and(i == 0, j == 0))
    def _():
        _norm_rows_into(hs_ref, xs_ref, g_ref, copy_ref=os_ref)

    def accumulate(h_ref, o_ref):
        h = h_ref[...]
        gate = jnp.dot(h, wg_ref[...].astype(BF16), preferred_element_type=F32)
        up = jnp.dot(h, wu_ref[...].astype(BF16), preferred_element_type=F32)
        act = (0.5 * (gate * jax.nn.sigmoid(gate)) * up).astype(BF16)
        width = o_ref.shape[1] // n_out_chunks
        for c in range(n_out_chunks):
            cols = slice(c * width, (c + 1) * width)
            o_ref[:, cols] += jnp.dot(act, wd_ref[:, cols].astype(BF16),
                                      preferred_element_type=F32)

    accumulate(hp_ref, op_ref)

    @pl.when(i == 0)
    def _():
        accumulate(hs_ref, os_ref)


def _ffn_half(xp, xs, norm_ffn, w_gate, w_up, w_down, layer, half):
    mp, d = xp.shape
    ms = xs.shape[0]
    f = w_gate.shape[-1]
    tm = _pick_tile(mp, FFN_ROW_TILE, SUBLANES_BF16)
    tf = _pick_tile(f, FFN_COL_TILE, LANES)
    n_out_chunks = max(1, d // 512)
    g4 = norm_ffn.reshape(norm_ffn.shape[0], 2, 1, d)
    return pl.pallas_call(
        functools.partial(_ffn_kernel, n_out_chunks=n_out_chunks),
        grid=(mp // tm, f // tf),
        in_specs=[
            pl.BlockSpec((tm, d), lambda i, j: (i, 0)),
            pl.BlockSpec((ms, d), lambda i, j: (0, 0), pipeline_mode=pl.Buffered(1)),
            pl.BlockSpec((None, None, 1, d), lambda i, j: (layer, half, 0, 0)),
            pl.BlockSpec((None, None, d, tf), lambda i, j: (layer, half, 0, j)),
            pl.BlockSpec((None, None, d, tf), lambda i, j: (layer, half, 0, j)),
            pl.BlockSpec((None, None, tf, d), lambda i, j: (layer, half, j, 0)),
        ],
        out_specs=[pl.BlockSpec((tm, d), lambda i, j: (i, 0)),
                   pl.BlockSpec((ms, d), lambda i, j: (0, 0))],
        out_shape=[jax.ShapeDtypeStruct((mp, d), F32), jax.ShapeDtypeStruct((ms, d), F32)],
        scratch_shapes=[pltpu.VMEM((tm, d), BF16), pltpu.VMEM((ms, d), BF16)],
        compiler_params=_params("arbitrary", "arbitrary"),
        name="ffn_half",
    )(xp, xs, g4, w_gate, w_up, w_down)


def _norm_matmul_kernel(*refs, with_sample):
    if with_sample:
        xp_ref, xs_ref, g_ref, w_ref, op_ref, os_ref, hp_ref, hs_ref = refs
    else:
        xp_ref, g_ref, w_ref, op_ref, hp_ref = refs
    i, j = pl.program_id(0), pl.program_id(1)

    @pl.when(j == 0)
    def _():
        _norm_rows_into(hp_ref, xp_ref, g_ref)

    op_ref[...] = jnp.dot(hp_ref[...], w_ref[...].astype(BF16), preferred_element_type=F32)

    if with_sample:
        @pl.when(jnp.logical_and(i == 0, j == 0))
        def _():
            _norm_rows_into(hs_ref, xs_ref, g_ref)

        @pl.when(i == 0)
        def _():
            os_ref[...] = jnp.dot(hs_ref[...], w_ref[...].astype(BF16),
                                  preferred_element_type=F32)


def _norm_matmul(xp, xs, gains, g_idx, w, w_idx):
    mp, d = xp.shape
    n = w.shape[-1]
    tm = _pick_tile(mp, PROJ_ROW_TILE, SUBLANES_BF16)
    tn = _pick_tile(n, COL_TILE, LANES)
    nn = n // tn
    g3 = gains.reshape(gains.shape[0], 1, d)
    with_sample = xs is not None
    x_spec = pl.BlockSpec((tm, d), lambda i, j: (i, 0), pipeline_mode=pl.Buffered(1))
    g_spec = pl.BlockSpec((None, 1, d), lambda i, j: (g_idx, 0, 0))
    w_spec = pl.BlockSpec((None, d, tn), lambda i, j: (w_idx, 0, j))
    o_spec = pl.BlockSpec((tm, tn), lambda i, j: (i, j))
    o_shape = jax.ShapeDtypeStruct((mp, n), F32)
    if not with_sample:
        return pl.pallas_call(
            functools.partial(_norm_matmul_kernel, with_sample=False),
            grid=(mp // tm, nn),
            in_specs=[x_spec, g_spec, w_spec],
            out_specs=o_spec,
            out_shape=o_shape,
            scratch_shapes=[pltpu.VMEM((tm, d), BF16)],
            compiler_params=_params("parallel", "arbitrary"),
            name="norm_matmul",
        )(xp, g3, w)
    ms = xs.shape[0]
    return pl.pallas_call(
        functools.partial(_norm_matmul_kernel, with_sample=True),
        grid=(mp // tm, nn),
        in_specs=[x_spec,
                  pl.BlockSpec((ms, d), lambda i, j: (0, 0), pipeline_mode=pl.Buffered(1)),
                  g_spec, w_spec],
        out_specs=[o_spec, pl.BlockSpec((ms, tn), _held_after_first_row_tile(nn))],
        out_shape=[o_shape, jax.ShapeDtypeStruct((ms, n), F32)],
        scratch_shapes=[pltpu.VMEM((tm, d), BF16), pltpu.VMEM((ms, d), BF16)],
        compiler_params=_params("arbitrary", "arbitrary"),
        name="norm_matmul",
    )(xp, xs, g3, w)


def _out_proj_kernel(xp_ref, xs_ref, op_ref, os_ref, mp_ref, ms_ref, wa_ref, wb_ref,
                     yp_ref, ys_ref):
    def project(x_ref, o_ref, m_ref, y_ref):
        acc = jnp.dot(o_ref[...].astype(BF16), wa_ref[...].astype(BF16),
                      preferred_element_type=F32)
        acc += jnp.dot(m_ref[...].astype(BF16), wb_ref[...].astype(BF16),
                       preferred_element_type=F32)
        y_ref[...] = x_ref[...] + acc

    project(xp_ref, op_ref, mp_ref, yp_ref)

    @pl.when(pl.program_id(0) == 0)
    def _():
        project(xs_ref, os_ref, ms_ref, ys_ref)


def _out_proj(xp, xs, o_p, o_s, m_p, m_s, w_out, layer):
    mp, d = xp.shape
    ms = xs.shape[0]
    wo, wm = o_p.shape[1], m_p.shape[1]
    assert wo % wm == 0 and wo + wm == w_out.shape[1]
    tm = _pick_tile(mp, PROJ_ROW_TILE, SUBLANES_BF16)
    tn = _pick_tile(d, COL_TILE, LANES)
    nn = d // tn
    held = _held_after_first_row_tile(nn)
    return pl.pallas_call(
        _out_proj_kernel,
        grid=(mp // tm, nn),
        in_specs=[
            pl.BlockSpec((tm, tn), lambda i, j: (i, j)),
            pl.BlockSpec((ms, tn), held),
            pl.BlockSpec((tm, wo), lambda i, j: (i, 0)),
            pl.BlockSpec((ms, wo), lambda i, j: (0, 0)),
            pl.BlockSpec((tm, wm), lambda i, j: (i, 0)),
            pl.BlockSpec((ms, wm), lambda i, j: (0, 0)),
            pl.BlockSpec((None, wo, tn), lambda i, j: (layer, 0, j)),
            pl.BlockSpec((None, wm, tn), lambda i, j: (layer, wo // wm, j)),
        ],
        out_specs=[pl.BlockSpec((tm, tn), lambda i, j: (i, j)),
                   pl.BlockSpec((ms, tn), held)],
        out_shape=[jax.ShapeDtypeStruct((mp, d), F32), jax.ShapeDtypeStruct((ms, d), F32)],
        compiler_params=_params("arbitrary", "arbitrary"),
        name="out_proj",
    )(xp, xs, o_p, o_s, m_p, m_s, w_out, w_out)


def _group_mean(xsq, gm):
    hi = xsq.astype(BF16)
    lo = (xsq - hi.astype(F32)).astype(BF16)
    return (jnp.dot(hi, gm, preferred_element_type=F32)
            + jnp.dot(lo, gm, preferred_element_type=F32))


def _rot_norm_kernel(p_ref, cos_ref, sin_ref, g_ref, gm_ref, o_ref, *, scale):
    x = p_ref[...]
    y = x * lax.rsqrt(_group_mean(x * x, gm_ref[...]) + EPS) * g_ref[...]
    lane = lax.broadcasted_iota(jnp.int32, y.shape, 1)
    lower_half = (lane % A_DK) < (A_DK // 2)
    partner = jnp.where(lower_half,
                        pltpu.roll(y, LANES - A_DK // 2, 1),
                        pltpu.roll(y, A_DK // 2, 1))
    o_ref[...] = ((y * cos_ref[...] + partner * sin_ref[...]) * scale).astype(o_ref.dtype)


def _group_mean_matrix():
    lane = jnp.arange(LANES)
    return jnp.where((lane[:, None] // A_DK) == (lane[None, :] // A_DK), 1.0 / A_DK, 0.0).astype(BF16)


def _rot_norm_rows(proj, block0, cos, sin_signed, gain, scale, dtype):
    rows, period = proj.shape[0], cos.shape[0]
    assert rows % period == 0
    tm = _pick_tile(period, ROT_ROW_TILE, SUBLANES_BF16)
    nt = period // tm
    g = jnp.tile(gain, LANES // A_DK).reshape(1, LANES)
    return pl.pallas_call(
        functools.partial(_rot_norm_kernel, scale=scale),
        grid=(rows // tm, A_HEADS),
        in_specs=[
            pl.BlockSpec((tm, LANES), lambda i, h: (i, block0 + h)),
            pl.BlockSpec((tm, LANES), lambda i, h: (i % nt, 0)),
            pl.BlockSpec((tm, LANES), lambda i, h: (i % nt, 0)),
            pl.BlockSpec((1, LANES), lambda i, h: (0, 0)),
            pl.BlockSpec((LANES, LANES), lambda i, h: (0, 0)),
        ],
        out_specs=pl.BlockSpec((tm, LANES), lambda i, h: (i, h)),
        out_shape=jax.ShapeDtypeStruct((rows, A_HEADS * LANES), dtype),
        compiler_params=_params("parallel", "arbitrary"),
        name="rot_norm_rows",
    )(proj, cos, sin_signed, g, _group_mean_matrix())


def _key_value_head_major_kernel(p_ref, cos_ref, sin_ref, g_ref, gm_ref, v_ref, k_out_ref, v_out_ref):
    _rot_norm_kernel(p_ref, cos_ref, sin_ref, g_ref, gm_ref, k_out_ref, scale=1.0)
    v_out_ref[...] = v_ref[...]


def _key_value_head_major(proj, k_block0, v_block0, cos, sin_signed, gain, bp, seq):
    tr = _pick_tile(seq, ROT_ROW_TILE, 8)
    nt = seq // tr
    g = jnp.tile(gain, LANES // A_DK).reshape(1, LANES)
    hm_spec = pl.BlockSpec((None, None, tr, LANES), lambda b, i, h: (b, h, i, 0))
    hm_shape = jax.ShapeDtypeStruct((bp, A_HEADS, seq, LANES), F32)
    return pl.pallas_call(
        _key_value_head_major_kernel,
        grid=(bp, nt, A_HEADS),
        in_specs=[
            pl.BlockSpec((tr, LANES), lambda b, i, h: (b * nt + i, k_block0 + h)),
            pl.BlockSpec((tr, LANES), lambda b, i, h: (i, 0)),
            pl.BlockSpec((tr, LANES), lambda b, i, h: (i, 0)),
            pl.BlockSpec((1, LANES), lambda b, i, h: (0, 0)),
            pl.BlockSpec((LANES, LANES), lambda b, i, h: (0, 0)),
            pl.BlockSpec((tr, LANES), lambda b, i, h: (b * nt + i, v_block0 + h)),
        ],
        out_specs=[hm_spec, hm_spec],
        out_shape=[hm_shape, hm_shape],
        compiler_params=_params("parallel", "parallel", "arbitrary"),
        name="key_value_head_major",
    )(proj, cos, sin_signed, g, _group_mean_matrix(), proj)


def _diff_lambda(lamv, lam_init):
    t1 = jnp.sum(lamv[0:1] * lamv[1:2], axis=-1, keepdims=True)
    t2 = jnp.sum(lamv[2:3] * lamv[3:4], axis=-1, keepdims=True)
    return jnp.exp(t1) - jnp.exp(t2) + lam_init


def _diff_attn_kernel(lamv_ref, q_ref, k_ref, v_ref, g_ref, o_ref, kb_ref, vt_ref, *, lam_init):
    tq = q_ref.shape[0]
    hps, n_kv = vt_ref.shape[:2]
    qi = pl.program_id(2)

    def head_cols(hh):
        return slice(hh * LANES, (hh + 1) * LANES)

    @pl.when(qi == 0)
    def _():
        for hh in range(hps):
            kb_ref[hh] = k_ref[hh].astype(BF16)
            for j in range(n_kv):
                vt_ref[hh, j] = v_ref[hh, j * tq:(j + 1) * tq, :].T.astype(BF16)

    lane = lax.broadcasted_iota(jnp.int32, (tq, LANES), 1)
    chains, qs = [], []
    for hh in range(hps):
        q = q_ref[:, head_cols(hh)]
        zero = jnp.zeros_like(q)
        for c in range(2):
            chains.append(hh)
            qs.append(jnp.where((lane < A_DK) if c == 0 else (lane >= A_DK), q, zero))
    key = lax.broadcasted_iota(jnp.int32, (tq, tq), 0)
    qry = lax.broadcasted_iota(jnp.int32, (tq, tq), 1)

    def block(j, carry, diagonal):
        r = pl.multiple_of(j * tq, tq)
        scores = [lax.dot_general(kb_ref[hh, pl.ds(r, tq), :], qc, NT_DIMS,
                                  preferred_element_type=F32)
                  for hh, qc in zip(chains, qs)]
        probs, stats = [], []
        for s, (m_prev, l_prev, _) in zip(scores, carry):
            if diagonal:
                s = jnp.where(key <= qry, s, -jnp.inf)
            m_new = jnp.maximum(m_prev, jnp.max(s, axis=0, keepdims=True))
            alpha = jnp.exp2(m_prev - m_new)
            p = jnp.exp2(s - m_new)
            stats.append((m_new, alpha * l_prev + jnp.sum(p, axis=0, keepdims=True), alpha))
            probs.append(p.astype(BF16))
        return tuple(
            (m_new, l_new, alpha * acc + jnp.dot(vt_ref[hh, j], p, preferred_element_type=F32))
            for hh, p, (m_new, l_new, alpha), (_, _, acc) in zip(chains, probs, stats, carry))

    init = tuple((jnp.full((1, tq), -jnp.inf, F32), jnp.zeros((1, tq), F32),
                  jnp.zeros((A_DV, tq), F32)) for _ in chains)
    carry = lax.fori_loop(0, qi, lambda j, c: block(j, c, False), init)
    carry = block(qi, carry, True)

    lam = _diff_lambda(lamv_ref[...], lam_init)
    for hh in range(hps):
        (_, l0, acc0), (_, l1, acc1) = carry[2 * hh], carry[2 * hh + 1]
        o = acc0 / l0 - lam * (acc1 / l1)
        ms = jnp.mean(o * o, axis=0, keepdims=True)
        o = o * lax.rsqrt(ms + EPS) * g_ref[...] * (1.0 - lam_init)
        o_ref[:, head_cols(hh)] = o.T.astype(o_ref.dtype)


def _diff_attn_prompt(q_all, k_hm, v_hm, lamv, g_subln, bp, seq, lam_init):
    tq = _pick_tile(seq, ATTN_TILE, LANES)
    nq = seq // tq
    hps = ATTN_HEADS_PER_STEP
    assert A_HEADS % hps == 0
    wide = hps * LANES
    return pl.pallas_call(
        functools.partial(_diff_attn_kernel, lam_init=lam_init),
        grid=(bp, A_HEADS // hps, nq),
        in_specs=[
            pl.BlockSpec(lamv.shape, lambda b, h, i: (0, 0)),
            pl.BlockSpec((tq, wide), lambda b, h, i: (b * nq + i, h)),
            pl.BlockSpec((None, hps, seq, LANES), lambda b, h, i: (b, h, 0, 0)),
            pl.BlockSpec((None, hps, seq, LANES), lambda b, h, i: (b, h, 0, 0)),
            pl.BlockSpec((A_DV, 1), lambda b, h, i: (0, 0)),
        ],
        out_specs=pl.BlockSpec((tq, wide), lambda b, h, i: (b * nq + i, h)),
        out_shape=jax.ShapeDtypeStruct((bp * seq, A_HEADS * A_DV), BF16),
        scratch_shapes=[pltpu.VMEM((hps, seq, LANES), BF16),
                        pltpu.VMEM((hps, nq, A_DV, tq), BF16)],
        compiler_params=_params("parallel", "parallel", "arbitrary"),
        name="diff_attn_prompt",
    )(lamv, q_all, k_hm, v_hm, g_subln.reshape(A_DV, 1))


def _diff_attn_decode_kernel(pt_ref, lamv_ref, q_ref, ks_ref, vs_ref, g_ref, *rest,
                             pages, lam_init, n_steps):
    del pt_ref
    k_refs = rest[:pages]
    v_refs = rest[pages:2 * pages]
    o_ref = rest[2 * pages]
    qh_ref, m_ref, l_ref, acc_ref, kb_ref, vb_ref = rest[2 * pages + 1:]
    ls = q_ref.shape[0]
    rph = 2 * ls
    rows = A_HEADS * rph
    step = pl.program_id(1)

    def head_cols(h):
        return slice(h * A_DV, (h + 1) * A_DV)

    def per_head_rows(x):
        return jnp.concatenate([x[:, head_cols(h)] for h in range(A_HEADS) for _ in range(2)],
                               axis=0)

    @pl.when(step == 0)
    def _():
        q = q_ref[...].astype(F32)
        lane = lax.broadcasted_iota(jnp.int32, (ls, LANES), 1)
        for h in range(A_HEADS):
            qh = q[:, head_cols(h)]
            qh_ref[h * rph:h * rph + ls, :] = jnp.where(lane < A_DK, qh, 0.0)
            qh_ref[h * rph + ls:(h + 1) * rph, :] = jnp.where(lane >= A_DK, qh, 0.0)
        m_ref[...] = jnp.full(m_ref.shape, -jnp.inf, F32)
        l_ref[...] = jnp.zeros(l_ref.shape, F32)
        acc_ref[...] = jnp.zeros(acc_ref.shape, F32)

    for h in range(A_HEADS):
        for p in range(pages):
            tok = slice(p * PAGE_SIZE, (p + 1) * PAGE_SIZE)
            kb_ref[h, tok, :] = k_refs[p][h].astype(BF16)
            vb_ref[h, tok, :] = v_refs[p][h].astype(BF16)

    s = jnp.concatenate(
        [lax.dot_general(qh_ref[h * rph:(h + 1) * rph, :].astype(BF16), kb_ref[h], NT_DIMS,
                         preferred_element_type=F32) for h in range(A_HEADS)], axis=0)
    m_prev = m_ref[...]
    m_new = jnp.maximum(m_prev, jnp.max(s, axis=-1, keepdims=True))
    alpha = jnp.exp2(m_prev - m_new)
    p_exp = jnp.exp2(s - m_new)
    l_ref[...] = alpha * l_ref[...] + jnp.sum(p_exp, axis=-1, keepdims=True)
    pv = jnp.concatenate(
        [jnp.dot(p_exp[h * rph:(h + 1) * rph].astype(BF16), vb_ref[h],
                 preferred_element_type=F32) for h in range(A_HEADS)], axis=0)
    acc_ref[...] = alpha * acc_ref[...] + pv
    m_ref[...] = m_new

    @pl.when(step == n_steps - 1)
    def _():
        qf = qh_ref[...]
        ks = ks_ref[...].astype(BF16).astype(F32)
        vs = vs_ref[...].astype(BF16).astype(F32)
        row = lax.broadcasted_iota(jnp.int32, (rows, 1), 0)
        q_of_row = row % ls
        s_new = []
        for t in range(ls):
            k_t = per_head_rows(jnp.broadcast_to(ks[t:t + 1], ks.shape))
            st = jnp.sum(qf * k_t, axis=-1, keepdims=True)
            s_new.append(jnp.where(q_of_row >= t, st, -jnp.inf))
        m_old = m_ref[...]
        m_fin = functools.reduce(jnp.maximum, s_new, m_old)
        a_fin = jnp.exp2(m_old - m_fin)
        l_fin = a_fin * l_ref[...]
        acc = a_fin * acc_ref[...]
        for t in range(ls):
            pt = jnp.exp2(s_new[t] - m_fin)
            l_fin = l_fin + pt
            v_t = per_head_rows(jnp.broadcast_to(vs[t:t + 1], vs.shape))
            acc = acc + pt.astype(BF16).astype(F32) * v_t

        lam = _diff_lambda(lamv_ref[...], lam_init)
        second = (row % rph) >= ls
        acc = acc * (jnp.where(second, -lam, 1.0) / l_fin)
        g = g_ref[...]
        for h in range(A_HEADS):
            o = acc[h * rph:h * rph + ls] + acc[h * rph + ls:(h + 1) * rph]
            o_ref[:, head_cols(h)] = _rmsnorm_lanes(o, g) * (1.0 - lam_init)


def _diff_attn_decode(q_s, k_s, v_s, cache_k, cache_v, page_table, lamv, g_subln, layer, lam_init):
    bs, ls, width = q_s.shape
    n_pages = page_table.shape[1]
    pages = _pick_tile(n_pages, DEC_PAGES_PER_STEP, 1)
    n_steps = n_pages // pages
    rows = ls * 2 * A_HEADS
    assert (2 * ls) % 8 == 0 and cache_k.shape[2:] == (PAGE_SIZE, A_HEADS, 2 * A_DK)

    cache_k = jnp.transpose(cache_k, (0, 1, 3, 2, 4))
    cache_v = jnp.transpose(cache_v, (0, 1, 3, 2, 4))

    def page_spec(p):
        return pl.BlockSpec((None, None, A_HEADS, PAGE_SIZE, LANES),
                            lambda b, s, pt: (layer, pt[b * n_pages + s * pages + p], 0, 0, 0))

    def per_batch():
        return pl.BlockSpec((None, ls, width), lambda b, s, pt: (b, 0, 0))

    grid_spec = pltpu.PrefetchScalarGridSpec(
        num_scalar_prefetch=1,
        grid=(bs, n_steps),
        in_specs=[pl.BlockSpec(lamv.shape, lambda b, s, pt: (0, 0)),
                  per_batch(), per_batch(), per_batch(),
                  pl.BlockSpec((1, LANES), lambda b, s, pt: (0, 0))]
                 + [page_spec(p) for p in range(pages)]
                 + [page_spec(p) for p in range(pages)],
        out_specs=per_batch(),
        scratch_shapes=[pltpu.VMEM((rows, LANES), F32),
                        pltpu.VMEM((rows, 1), F32),
                        pltpu.VMEM((rows, 1), F32),
                        pltpu.VMEM((rows, A_DV), F32),
                        pltpu.VMEM((A_HEADS, pages * PAGE_SIZE, LANES), BF16),
                        pltpu.VMEM((A_HEADS, pages * PAGE_SIZE, A_DV), BF16)],
    )
    return pl.pallas_call(
        functools.partial(_diff_attn_decode_kernel, pages=pages, lam_init=lam_init, n_steps=n_steps),
        grid_spec=grid_spec,
        out_shape=jax.ShapeDtypeStruct((bs, ls, width), F32),
        compiler_params=_params("parallel", "arbitrary"),
        name="diff_attn_decode",
    )(page_table.reshape(-1), lamv, q_s, k_s, v_s, g_subln.reshape(1, LANES),
      *([cache_k] * pages), *([cache_v] * pages))


def _mem_attn_kernel(q_ref, k_ref, v_ref, g_ref, o_ref):
    g = g_ref[...]
    cols = [slice(h * MEM_HD, (h + 1) * MEM_HD) for h in range(MEM_HEADS)]
    scores = [lax.dot_general(_rmsnorm_lanes(q_ref[:, c], g).astype(BF16),
                              k_ref[:, c].astype(BF16), NT_DIMS, preferred_element_type=F32)
              for c in cols]
    probs = []
    for s in scores:
        s = s * (MEM_HD ** -0.5)
        e = jnp.exp(s - jnp.max(s, axis=-1, keepdims=True))
        probs.append((e / jnp.sum(e, axis=-1, keepdims=True)).astype(BF16))
    for c, p in zip(cols, probs):
        o_ref[:, c] = jnp.dot(p, v_ref[:, c].astype(BF16),
                              preferred_element_type=F32).astype(o_ref.dtype)


def _mem_attn_prompt(proj, q_block0, k_norm, kv, gq, bp, seq):
    mem = k_norm.shape[0] // bp
    tq = _pick_tile(seq, 512, LANES)
    nq = seq // tq
    wide = MEM_HEADS * MEM_HD
    assert q_block0 % MEM_HEADS == 0
    return pl.pallas_call(
        _mem_attn_kernel,
        grid=(bp, nq),
        in_specs=[
            pl.BlockSpec((tq, wide), lambda b, i: (b * nq + i, q_block0 // MEM_HEADS)),
            pl.BlockSpec((mem, wide), lambda b, i: (b, 0)),
            pl.BlockSpec((mem, wide), lambda b, i: (b, 1)),
            pl.BlockSpec((1, LANES), lambda b, i: (0, 0)),
        ],
        out_specs=pl.BlockSpec((tq, wide), lambda b, i: (b * nq + i, 0)),
        out_shape=jax.ShapeDtypeStruct((bp * seq, wide), BF16),
        compiler_params=_params("parallel", "arbitrary"),
        name="mem_attn_prompt",
    )(proj, k_norm, kv, gq.reshape(1, LANES))


def _mem_attn_interleaved_kernel(q_ref, k_ref, v_ref, g_ref, o_ref):
    g = g_ref[...]
    k = k_ref[...].astype(BF16)
    v = v_ref[...].astype(BF16)
    ls, rows = q_ref.shape[0], k.shape[0]
    head_of_row = lax.broadcasted_iota(jnp.int32, (ls, rows), 1) % MEM_HEADS
    cols = [slice(h * MEM_HD, (h + 1) * MEM_HD) for h in range(MEM_HEADS)]
    scores = [lax.dot_general(_rmsnorm_lanes(q_ref[:, c], g).astype(BF16), k, NT_DIMS,
                              preferred_element_type=F32) for c in cols]
    probs = []
    for h, s in enumerate(scores):
        s = jnp.where(head_of_row == h, s * (MEM_HD ** -0.5), -jnp.inf)
        e = jnp.exp(s - jnp.max(s, axis=-1, keepdims=True))
        probs.append((e / jnp.sum(e, axis=-1, keepdims=True)).astype(BF16))
    for c, p in zip(cols, probs):
        o_ref[:, c] = jnp.dot(p, v, preferred_element_type=F32).astype(o_ref.dtype)


def _mem_attn_sample(proj_s, q_block0, cache_k, cache_v, gq, layer):
    bs, ls, _ = proj_s.shape
    mem = cache_k.shape[2]
    wide = MEM_HEADS * MEM_HD
    assert q_block0 % MEM_HEADS == 0
    return pl.pallas_call(
        _mem_attn_interleaved_kernel,
        grid=(bs,),
        in_specs=[
            pl.BlockSpec((None, ls, wide), lambda b: (b, 0, q_block0 // MEM_HEADS)),
            pl.BlockSpec((None, None, mem, MEM_HD), lambda b: (layer, b, 0, 0)),
            pl.BlockSpec((None, None, mem, MEM_HD), lambda b: (layer, b, 0, 0)),
            pl.BlockSpec((1, LANES), lambda b: (0, 0)),
        ],
        out_specs=pl.BlockSpec((None, ls, wide), lambda b: (b, 0, 0)),
        out_shape=jax.ShapeDtypeStruct((bs, ls, wide), F32),
        compiler_params=_params("arbitrary"),
        name="mem_attn_sample",
    )(proj_s, cache_k, cache_v, gq.reshape(1, LANES))


def _mem_k_norm_kernel(k_ref, g_ref, o_ref):
    o_ref[...] = _rmsnorm_lanes(k_ref[...], g_ref[...])


def _mem_k_norm(kv, gk):
    rows = kv.shape[0]
    tm = _pick_tile(rows, 256, 8)
    return pl.pallas_call(
        _mem_k_norm_kernel,
        grid=(rows // tm, MEM_HEADS),
        in_specs=[pl.BlockSpec((tm, LANES), lambda i, h: (i, h)),
                  pl.BlockSpec((1, LANES), lambda i, h: (0, 0))],
        out_specs=pl.BlockSpec((tm, LANES), lambda i, h: (i, h)),
        out_shape=jax.ShapeDtypeStruct((rows, MEM_HEADS * MEM_HD), F32),
        compiler_params=_params("parallel", "arbitrary"),
        name="mem_k_norm",
    )(kv, gk.reshape(1, LANES))


def _split3(x):
    p1 = x.astype(BF16)
    r1 = x - p1.astype(F32)
    p2 = r1.astype(BF16)
    p3 = (r1 - p2.astype(F32)).astype(BF16)
    return p1, p2, p3


def _hgrn_kernel(*refs, chunk, sub, valid, layer, has_state):
    if has_state:
        (q_ref, f_ref, v_ref, gate_ref, lb_ref, g_ref, tri_ref, s0_ref,
         o_ref, s_out_ref, st_ref, b_scr, kk_scr) = refs
    else:
        (q_ref, f_ref, v_ref, gate_ref, lb_ref, g_ref, tri_ref,
         o_ref, s_out_ref, st_ref, b_scr, kk_scr) = refs
    tl = q_ref.shape[0]
    heads = st_ref.shape[0]
    t = pl.program_id(2)

    @pl.when(t == 0)
    def _():
        for hh in range(heads):
            if has_state:
                st_ref[hh] = s0_ref[hh].T
            else:
                st_ref[hh] = jnp.zeros(st_ref.shape[1:], F32)

    lb = lb_ref[...]
    e = jnp.exp(lb - jnp.max(lb, axis=0, keepdims=True))
    sm = e / jnp.sum(e, axis=0, keepdims=True)
    lower = jnp.zeros((1, sm.shape[1]), F32)
    for r in range(1, layer + 1):
        lower = lower + sm[r:r + 1]

    g = g_ref[...]
    n_chunks = tl // chunk
    n_sub = chunk // sub
    chunk_row = lax.broadcasted_iota(jnp.int32, (chunk, 1), 0)

    q_all = q_ref[...]
    v_all = v_ref[...]
    f_all = lower + (1.0 - lower) * jax.nn.sigmoid(f_ref[...])
    kk_all = 1.0 - f_all
    tri = tri_ref[...]
    p1, p2, p3 = _split3(jnp.log(f_all))
    b_all = (jnp.dot(tri, p1, preferred_element_type=F32)
             + jnp.dot(tri, p2, preferred_element_type=F32)
             + jnp.dot(tri, p3, preferred_element_type=F32))
    b_all = b_all * LOG2_E
    v16_all = v_all.astype(BF16)

    units = [(hh, ci) for hh in range(heads) for ci in range(n_chunks)]

    def unit_block(x, unit):
        hh, ci = unit
        return x[ci * chunk:(ci + 1) * chunk, hh * LANES:(hh + 1) * LANES]

    incs, atts, b_lasts = [], [], []
    for u in units:
        q, b, kk = unit_block(q_all, u), unit_block(b_all, u), unit_block(kk_all, u)
        b_last = b[valid - 1:valid]
        k_dec = jnp.where(chunk_row < valid, kk * jnp.exp2(b_last - b), 0.0)
        incs.append(lax.dot_general(unit_block(v16_all, u), k_dec.astype(BF16), TN_DIMS,
                                    preferred_element_type=F32))
        b_lasts.append(b_last)
        for i in range(1, n_sub):
            rs = slice(i * sub, (i + 1) * sub)
            b_ref = b[i * sub - 1:i * sub]
            q_dec = (q[rs] * jnp.exp2(b[rs] - b_ref)).astype(BF16)
            k_dec = (kk[:i * sub] * jnp.exp2(b_ref - b[:i * sub])).astype(BF16)
            atts.append(lax.dot_general(q_dec, k_dec, NT_DIMS, preferred_element_type=F32))

    b_scr[...] = b_all
    kk_scr[...] = kk_all
    causal = [jnp.where(lax.broadcasted_iota(jnp.int32, (sub, LANES), 0) >= s, 0.0, -jnp.inf)
              for s in range(sub)]
    diag = []
    for u in units:
        hh, ci = u
        cols = slice(hh * LANES, (hh + 1) * LANES)
        q, b = unit_block(q_all, u), unit_block(b_all, u)
        for i in range(n_sub):
            rs = slice(i * sub, (i + 1) * sub)
            qb, bb = q[rs], b[rs]
            terms = []
            for s in range(sub):
                row = ci * chunk + i * sub + s
                d = (bb - b_scr[row:row + 1, cols]) + causal[s]
                w = jnp.sum(qb * kk_scr[row:row + 1, cols] * jnp.exp2(d), axis=-1, keepdims=True)
                terms.append(w * v_ref[row:row + 1, cols])
            while len(terms) > 1:
                terms = [a + c for a, c in zip(terms[0::2], terms[1::2])]
            diag.append(terms[0])

    intra = []
    for n, u in enumerate(units):
        v16 = unit_block(v16_all, u)
        for i in range(n_sub):
            o_i = diag[n * n_sub + i]
            if i > 0:
                att = atts[n * (n_sub - 1) + i - 1]
                o_i = o_i + jnp.dot(att.astype(BF16), v16[:i * sub], preferred_element_type=F32)
            intra.append(o_i)

    states = [st_ref[hh] for hh in range(heads)]
    for n, u in enumerate(units):
        hh, ci = u
        q, b = unit_block(q_all, u), unit_block(b_all, u)
        o_inter = lax.dot_general((q * jnp.exp2(b)).astype(BF16), states[hh].astype(BF16),
                                  NT_DIMS, preferred_element_type=F32)
        states[hh] = states[hh] * jnp.exp2(b_lasts[n]) + incs[n]
        parts = intra[n * n_sub:(n + 1) * n_sub]
        o = o_inter + (parts[0] if n_sub == 1 else jnp.concatenate(parts, axis=0))
        rows, cols = slice(ci * chunk, (ci + 1) * chunk), slice(hh * LANES, (hh + 1) * LANES)
        gate = gate_ref[rows, cols]
        o_ref[rows, cols] = (_rmsnorm_lanes(o, g)
                             * (gate * jax.nn.sigmoid(gate))).astype(o_ref.dtype)
    for hh in range(heads):
        st_ref[hh] = states[hh]

    @pl.when(t == pl.num_programs(2) - 1)
    def _():
        for hh in range(heads):
            s_out_ref[hh] = states[hh].T


def _tri(rows, chunk):
    r = jnp.arange(rows)
    same = (r[:, None] // chunk) == (r[None, :] // chunk)
    return (same & (r[:, None] >= r[None, :])).astype(BF16)


def _hgrn_prompt(proj, lb_logits, g_out, bp, seq, layer):
    chunk = math.gcd(seq, B_CHUNK)
    sub = math.gcd(chunk, B_SUB)
    tl = _pick_tile(seq, 256, chunk)
    nt = seq // tl
    h_ = B_HEADS
    hps = HGRN_PROMPT_HEADS_PER_STEP
    assert h_ % hps == 0
    wide = hps * LANES

    def col(block0):
        return pl.BlockSpec((tl, wide), lambda b, h, t: (b * nt + t, block0 // hps + h))

    return pl.pallas_call(
        functools.partial(_hgrn_kernel, chunk=chunk, sub=sub, valid=chunk, layer=layer,
                          has_state=False),
        grid=(bp, h_ // hps, nt),
        in_specs=[col(0), col(h_), col(2 * h_), col(3 * h_),
                  pl.BlockSpec((lb_logits.shape[0], wide), lambda b, h, t: (0, h)),
                  pl.BlockSpec((1, LANES), lambda b, h, t: (0, 0)),
                  pl.BlockSpec((tl, tl), lambda b, h, t: (0, 0))],
        out_specs=[pl.BlockSpec((tl, wide), lambda b, h, t: (b * nt + t, h)),
                   pl.BlockSpec((None, hps, B_DK, B_DV), lambda b, h, t: (b, h, 0, 0))],
        out_shape=[jax.ShapeDtypeStruct((bp * seq, h_ * B_DV), BF16),
                   jax.ShapeDtypeStruct((bp, h_, B_DK, B_DV), F32)],
        scratch_shapes=[pltpu.VMEM((hps, B_DV, B_DK), F32),
                        pltpu.VMEM((tl, wide), F32), pltpu.VMEM((tl, wide), F32)],
        compiler_params=_params("parallel", "parallel", "arbitrary"),
        name="hgrn_prompt",
    )(proj, proj, proj, proj, lb_logits, g_out.reshape(1, LANES), _tri(tl, chunk))


def _hgrn_sample(proj_s, state, lb_logits, g_out, layer, state_layer):
    bs, ls, n = proj_s.shape
    chunk = 8
    assert ls <= chunk
    padded = jnp.pad(proj_s, ((0, 0), (0, chunk - ls), (0, 0)))
    h_ = B_HEADS
    hps = HGRN_SAMPLE_HEADS_PER_STEP
    assert h_ % hps == 0
    wide = hps * LANES

    def col(block0):
        return pl.BlockSpec((None, chunk, wide), lambda b, h, t: (b, 0, block0 // hps + h))

    o, s_new = pl.pallas_call(
        functools.partial(_hgrn_kernel, chunk=chunk, sub=chunk, valid=ls, layer=layer,
                          has_state=True),
        grid=(bs, h_ // hps, 1),
        in_specs=[col(0), col(h_), col(2 * h_), col(3 * h_),
                  pl.BlockSpec((lb_logits.shape[0], wide), lambda b, h, t: (0, h)),
                  pl.BlockSpec((1, LANES), lambda b, h, t: (0, 0)),
                  pl.BlockSpec((chunk, chunk), lambda b, h, t: (0, 0)),
                  pl.BlockSpec((None, None, hps, B_DK, B_DV),
                               lambda b, h, t: (state_layer, b, h, 0, 0))],
        out_specs=[pl.BlockSpec((None, chunk, wide), lambda b, h, t: (b, 0, h)),
                   pl.BlockSpec((None, hps, B_DK, B_DV), lambda b, h, t: (b, h, 0, 0))],
        out_shape=[jax.ShapeDtypeStruct((bs, chunk, h_ * B_DV), F32),
                   jax.ShapeDtypeStruct((bs, h_, B_DK, B_DV), F32)],
        scratch_shapes=[pltpu.VMEM((hps, B_DV, B_DK), F32),
                        pltpu.VMEM((chunk, wide), F32), pltpu.VMEM((chunk, wide), F32)],
        compiler_params=_params("parallel", "parallel", "arbitrary"),
        name="hgrn_sample",
    )(padded, padded, padded, padded, lb_logits, g_out.reshape(1, LANES), _tri(chunk, chunk), state)
    return o[:, :ls], s_new


def _rope_tables(pos):
    half = A_DK // 2
    inv_freq = ROPE_THETA ** (-jnp.arange(half, dtype=F32) / half)
    ang = pos.astype(F32)[:, None] * inv_freq[None, :]
    cos, sin = jnp.cos(ang), jnp.sin(ang)
    reps = LANES // A_DK
    return (jnp.tile(cos, (1, 2 * reps)), jnp.tile(jnp.concatenate([-sin, sin], axis=1), (1, reps)))


def kernel(x_prompt, x_sample, cache_attn_k, cache_attn_v, state_hgrn, cache_mem_k, cache_mem_v,
           page_table, mem_prompt, norm_ffn, w_ffn_gate, w_ffn_up, w_ffn_down, norm_mix, norm_mem,
           w_mem_kv, gq_mem, gk_mem, w_out, w_in_attn, gq_attn, gk_attn, lam_q1, lam_k1, lam_q2,
           lam_k2, g_subln, w_in_hgrn, lb_logits, g_hgrn_out):
    bp, seq, d = x_prompt.shape
    bs, ls, _ = x_sample.shape
    depth = norm_mix.shape[0]
    mem = mem_prompt.shape[1]
    mp = bp * seq
    ms = bs * ls
    past_len = page_table.shape[1] * PAGE_SIZE
    mem_w = MEM_HEADS * MEM_HD
    qk_w = A_HEADS * 2 * A_DK
    v_w = A_HEADS * A_DV

    xp = x_prompt.reshape(mp, d)
    xs = x_sample.reshape(ms, d)
    cos_p, sin_p = _rope_tables(jnp.arange(seq))
    cos_s, sin_s = _rope_tables(jnp.tile(past_len + jnp.arange(ls), bs))
    mem_rows = mem_prompt.reshape(bp * mem, d)
    cmk = cache_mem_k.reshape(depth, bs, mem * MEM_HEADS, MEM_HD)
    cmv = cache_mem_v.reshape(depth, bs, mem * MEM_HEADS, MEM_HD)

    k_rows_p, v_rows_p, k_rows_s, v_rows_s = [], [], [], []
    st_p, st_s, mem_k_new, mem_v_new = [], [], [], []
    for i in range(depth):
        xp, xs = _ffn_half(xp, xs, norm_ffn, w_ffn_gate, w_ffn_up, w_ffn_down, i, 0)

        kv = _norm_matmul(mem_rows, None, norm_mem, i, w_mem_kv, i)
        k_norm = _mem_k_norm(kv, gk_mem[i])
        mem_k_new.append(k_norm.reshape(bp, mem, MEM_HEADS, MEM_HD))
        mem_v_new.append(kv[:, mem_w:].reshape(bp, mem, MEM_HEADS, MEM_HD))

        if i % 2 == 0:
            a = i // 2
            lam_init = 0.8 - 0.6 * math.exp(-0.3 * i)
            proj, proj_s2 = _norm_matmul(xp, xs, norm_mix, i, w_in_attn, a)
            q_scale = A_DK ** -0.5 * LOG2_E
            q_p = _rot_norm_rows(proj, 0, cos_p, sin_p, gq_attn[a], q_scale, BF16)
            q_s = _rot_norm_rows(proj_s2, 0, cos_s, sin_s, gq_attn[a], q_scale, BF16)
            k_hm, v_hm = _key_value_head_major(proj, A_HEADS, 2 * A_HEADS, cos_p, sin_p,
                                               gk_attn[a], bp, seq)
            k_s = _rot_norm_rows(proj_s2, A_HEADS, cos_s, sin_s, gk_attn[a], 1.0, F32)
            lamv = jnp.stack([lam_q1[a], lam_k1[a], lam_q2[a], lam_k2[a]])
            o_p = _diff_attn_prompt(q_p, k_hm, v_hm, lamv, g_subln[a], bp, seq, lam_init)
            k_rows_p.append(jnp.transpose(k_hm, (0, 2, 1, 3)))
            v_rows_p.append(jnp.transpose(v_hm, (0, 2, 1, 3)))
            proj_s = proj_s2.reshape(bs, ls, proj_s2.shape[1])
            q_s = q_s.reshape(bs, ls, qk_w)
            k_s = k_s.reshape(bs, ls, qk_w)
            v_s = proj_s[..., 2 * qk_w:2 * qk_w + v_w]
            o_s = _diff_attn_decode(q_s, k_s, v_s, cache_attn_k, cache_attn_v, page_table, lamv,
                                    g_subln[a], a, lam_init)
            k_rows_s.append(k_s.reshape(bs, ls, A_HEADS, 2 * A_DK))
            v_rows_s.append(v_s.reshape(bs, ls, A_HEADS, A_DV))
            mq_block0 = (2 * qk_w + v_w) // LANES
        else:
            j = i // 2
            proj, proj_s2 = _norm_matmul(xp, xs, norm_mix, i, w_in_hgrn, j)
            proj_s = proj_s2.reshape(bs, ls, proj_s2.shape[1])
            o_p, s_p = _hgrn_prompt(proj, lb_logits, g_hgrn_out[j], bp, seq, i)
            o_s, s_s = _hgrn_sample(proj_s, state_hgrn, lb_logits, g_hgrn_out[j], i, j)
            st_p.append(s_p)
            st_s.append(s_s)
            mq_block0 = (2 * B_HEADS * B_DK + 2 * B_HEADS * B_DV) // LANES

        m_p = _mem_attn_prompt(proj, mq_block0, k_norm, kv, gq_mem[i], bp, seq)
        m_s = _mem_attn_sample(proj_s, mq_block0, cmk, cmv, gq_mem[i], i)
        xp, xs = _out_proj(xp, xs, o_p, o_s.reshape(ms, -1), m_p, m_s.reshape(ms, -1), w_out, i)

        xp, xs = _ffn_half(xp, xs, norm_ffn, w_ffn_gate, w_ffn_up, w_ffn_down, i, 1)

    return (xp.reshape(bp, seq, d), xs.reshape(bs, ls, d),
            jnp.stack(k_rows_p), jnp.stack(v_rows_p), jnp.stack(k_rows_s), jnp.stack(v_rows_s),
            jnp.stack(st_p), jnp.stack(st_s), jnp.stack(mem_k_new), jnp.stack(mem_v_new))
```

```python
import functools
import math

import jax
import jax.numpy as jnp
from jax import lax
from jax.experimental import pallas as pl
from jax.experimental.pallas import tpu as pltpu

F32 = jnp.float32
BF16 = jnp.bfloat16

EPS = 1e-6
LOG2_E = math.log2(math.e)
ROPE_THETA = 10000.0
A_HEADS = 12
A_DK = 64
A_DV = 128
B_HEADS = 12
B_DK = 128
B_DV = 128
B_CHUNK = 64
B_SUB = 8
MEM_HEADS = 4
MEM_HD = 128
PAGE_SIZE = 128
LANES = 128
SUBLANES_BF16 = 16
V7X_VMEM_LIMIT_BYTES = 56 * 1024 * 1024
FFN_ROW_TILE = 1024
PROJ_ROW_TILE = 2048
ROT_ROW_TILE = 1024
ROT_HEADS_PER_STEP = 4
MEM_SAMPLE_BATCHES_PER_STEP = 4
COL_TILE = 512
FFN_COL_TILE = 256
NORM_ROWS_PER_ITER = 128
DEC_PAGES_PER_STEP = 8
ATTN_TILE = 512
ATTN_HEADS_PER_STEP = 3
HGRN_SAMPLE_HEADS_PER_STEP = 12
HGRN_PROMPT_HEADS_PER_STEP = 4
NT_DIMS = (((1,), (1,)), ((), ()))
TN_DIMS = (((0,), (0,)), ((), ()))


def _params(*semantics):
    return pltpu.CompilerParams(dimension_semantics=semantics,
                                vmem_limit_bytes=V7X_VMEM_LIMIT_BYTES)


def _pick_tile(n, target, align):
    best = None
    for t in range(align, min(n, target) + 1, align):
        if n % t == 0:
            best = t
    assert best is not None, (n, target, align)
    return best


def _rmsnorm_lanes(x, g):
    ms = jnp.mean(x * x, axis=-1, keepdims=True)
    return x * lax.rsqrt(ms + EPS) * g


def _norm_rows_into(h_ref, x_ref, g_ref, copy_ref=None):
    rows = x_ref.shape[0]
    chunk = _pick_tile(rows, NORM_ROWS_PER_ITER, SUBLANES_BF16 if rows % SUBLANES_BF16 == 0 else 8)
    g = g_ref[...]

    def body(i, carry):
        r = pl.multiple_of(i * chunk, chunk)
        x = x_ref[pl.ds(r, chunk), :]
        h_ref[pl.ds(r, chunk), :] = _rmsnorm_lanes(x, g).astype(h_ref.dtype)
        if copy_ref is not None:
            copy_ref[pl.ds(r, chunk), :] = x
        return carry

    lax.fori_loop(0, rows // chunk, body, 0)


def _held_after_first_row_tile(n_col_blocks):
    return lambda i, j: (0, jnp.where(i == 0, j, n_col_blocks - 1))


def _ffn_kernel(xp_ref, xs_ref, g_ref, wg_ref, wu_ref, wd_ref, op_ref, os_ref, hp_ref, hs_ref,
                *, n_out_chunks):
    i, j = pl.program_id(0), pl.program_id(1)

    @pl.when(j == 0)
    def _():
        _norm_rows_into(hp_ref, xp_ref, g_ref, copy_ref=op_ref)

    @pl.when(jnp.logical_and(i == 0, j == 0))
    def _():
        _norm_rows_into(hs_ref, xs_ref, g_ref, copy_ref=os_ref)

    def accumulate(h_ref, o_ref):
        h = h_ref[...]
        gate = jnp.dot(h, wg_ref[...].astype(BF16), preferred_element_type=F32)
        up = jnp.dot(h, wu_ref[...].astype(BF16), preferred_element_type=F32)
        act = (0.5 * (gate * jax.nn.sigmoid(gate)) * up).astype(BF16)
        width = o_ref.shape[1] // n_out_chunks
        for c in range(n_out_chunks):
            cols = slice(c * width, (c + 1) * width)
            o_ref[:, cols] += jnp.dot(act, wd_ref[:, cols].astype(BF16),
                                      preferred_element_type=F32)

    accumulate(hp_ref, op_ref)

    @pl.when(i == 0)
    def _():
        accumulate(hs_ref, os_ref)


def _ffn_half(xp, xs, norm_ffn, w_gate, w_up, w_down, layer, half):
    mp, d = xp.shape
    ms = xs.shape[0]
    f = w_gate.shape[-1]
    tm = _pick_tile(mp, FFN_ROW_TILE, SUBLANES_BF16)
    tf = _pick_tile(f, FFN_COL_TILE, LANES)
    n_out_chunks = max(1, d // 512)
    g4 = norm_ffn.reshape(norm_ffn.shape[0], 2, 1, d)
    return pl.pallas_call(
        functools.partial(_ffn_kernel, n_out_chunks=n_out_chunks),
        grid=(mp // tm, f // tf),
        in_specs=[
            pl.BlockSpec((tm, d), lambda i, j: (i, 0)),
            pl.BlockSpec((ms, d), lambda i, j: (0, 0), pipeline_mode=pl.Buffered(1)),
            pl.BlockSpec((None, None, 1, d), lambda i, j: (layer, half, 0, 0)),
            pl.BlockSpec((None, None, d, tf), lambda i, j: (layer, half, 0, j)),
            pl.BlockSpec((None, None, d, tf), lambda i, j: (layer, half, 0, j)),
            pl.BlockSpec((None, None, tf, d), lambda i, j: (layer, half, j, 0)),
        ],
        out_specs=[pl.BlockSpec((tm, d), lambda i, j: (i, 0)),
                   pl.BlockSpec((ms, d), lambda i, j: (0, 0))],
        out_shape=[jax.ShapeDtypeStruct((mp, d), F32), jax.ShapeDtypeStruct((ms, d), F32)],
        scratch_shapes=[pltpu.VMEM((tm, d), BF16), pltpu.VMEM((ms, d), BF16)],
        compiler_params=_params("arbitrary", "arbitrary"),
        name="ffn_half",
    )(xp, xs, g4, w_gate, w_up, w_down)


def _norm_matmul_kernel(*refs, with_sample):
    if with_sample:
        xp_ref, xs_ref, g_ref, w_ref, op_ref, os_ref, hp_ref, hs_ref = refs
    else:
        xp_ref, g_ref, w_ref, op_ref, hp_ref = refs
    i, j = pl.program_id(0), pl.program_id(1)

    @pl.when(j == 0)
    def _():
        _norm_rows_into(hp_ref, xp_ref, g_ref)

    op_ref[...] = jnp.dot(hp_ref[...], w_ref[...].astype(BF16), preferred_element_type=F32)

    if with_sample:
        @pl.when(jnp.logical_and(i == 0, j == 0))
        def _():
            _norm_rows_into(hs_ref, xs_ref, g_ref)

        @pl.when(i == 0)
        def _():
            os_ref[...] = jnp.dot(hs_ref[...], w_ref[...].astype(BF16),
                                  preferred_element_type=F32)


def _norm_matmul(xp, xs, gains, g_idx, w, w_idx):
    mp, d = xp.shape
    n = w.shape[-1]
    tm = _pick_tile(mp, PROJ_ROW_TILE, SUBLANES_BF16)
    tn = _pick_tile(n, COL_TILE, LANES)
    nn = n // tn
    g3 = gains.reshape(gains.shape[0], 1, d)
    with_sample = xs is not None
    x_spec = pl.BlockSpec((tm, d), lambda i, j: (i, 0), pipeline_mode=pl.Buffered(1))
    g_spec = pl.BlockSpec((None, 1, d), lambda i, j: (g_idx, 0, 0))
    w_spec = pl.BlockSpec((None, d, tn), lambda i, j: (w_idx, 0, j))
    o_spec = pl.BlockSpec((tm, tn), lambda i, j: (i, j))
    o_shape = jax.ShapeDtypeStruct((mp, n), F32)
    if not with_sample:
        return pl.pallas_call(
            functools.partial(_norm_matmul_kernel, with_sample=False),
            grid=(mp // tm, nn),
            in_specs=[x_spec, g_spec, w_spec],
            out_specs=o_spec,
            out_shape=o_shape,
            scratch_shapes=[pltpu.VMEM((tm, d), BF16)],
            compiler_params=_params("parallel", "arbitrary"),
            name="norm_matmul",
        )(xp, g3, w)
    ms = xs.shape[0]
    return pl.pallas_call(
        functools.partial(_norm_matmul_kernel, with_sample=True),
        grid=(mp // tm, nn),
        in_specs=[x_spec,
                  pl.BlockSpec((ms, d), lambda i, j: (0, 0), pipeline_mode=pl.Buffered(1)),
                  g_spec, w_spec],
        out_specs=[o_spec, pl.BlockSpec((ms, tn), _held_after_first_row_tile(nn))],
        out_shape=[o_shape, jax.ShapeDtypeStruct((ms, n), F32)],
        scratch_shapes=[pltpu.VMEM((tm, d), BF16), pltpu.VMEM((ms, d), BF16)],
        compiler_params=_params("arbitrary", "arbitrary"),
        name="norm_matmul",
    )(xp, xs, g3, w)


def _out_proj_kernel(xp_ref, xs_ref, op_ref, os_ref, mp_ref, ms_ref, wa_ref, wb_ref,
                     yp_ref, ys_ref):
    def project(x_ref, o_ref, m_ref, y_ref):
        acc = jnp.dot(o_ref[...].astype(BF16), wa_ref[...].astype(BF16),
                      preferred_element_type=F32)
        acc += jnp.dot(m_ref[...].astype(BF16), wb_ref[...].astype(BF16),
                       preferred_element_type=F32)
        y_ref[...] = x_ref[...] + acc

    project(xp_ref, op_ref, mp_ref, yp_ref)

    @pl.when(pl.program_id(0) == 0)
    def _():
        project(xs_ref, os_ref, ms_ref, ys_ref)


def _out_proj(xp, xs, o_p, o_s, m_p, m_s, w_out, layer):
    mp, d = xp.shape
    ms = xs.shape[0]
    wo, wm = o_p.shape[1], m_p.shape[1]
    assert wo % wm == 0 and wo + wm == w_out.shape[1]
    tm = _pick_tile(mp, PROJ_ROW_TILE, SUBLANES_BF16)
    tn = _pick_tile(d, COL_TILE, LANES)
    nn = d // tn
    held = _held_after_first_row_tile(nn)
    return pl.pallas_call(
        _out_proj_kernel,
        grid=(mp // tm, nn),
        in_specs=[
            pl.BlockSpec((tm, tn), lambda i, j: (i, j)),
            pl.BlockSpec((ms, tn), held),
            pl.BlockSpec((tm, wo), lambda i, j: (i, 0)),
            pl.BlockSpec((ms, wo), lambda i, j: (0, 0)),
            pl.BlockSpec((tm, wm), lambda i, j: (i, 0)),
            pl.BlockSpec((ms, wm), lambda i, j: (0, 0)),
            pl.BlockSpec((None, wo, tn), lambda i, j: (layer, 0, j)),
            pl.BlockSpec((None, wm, tn), lambda i, j: (layer, wo // wm, j)),
        ],
        out_specs=[pl.BlockSpec((tm, tn), lambda i, j: (i, j)),
                   pl.BlockSpec((ms, tn), held)],
        out_shape=[jax.ShapeDtypeStruct((mp, d), F32), jax.ShapeDtypeStruct((ms, d), F32)],
        compiler_params=_params("arbitrary", "arbitrary"),
        name="out_proj",
    )(xp, xs, o_p, o_s, m_p, m_s, w_out, w_out)


def _group_mean(xsq, gm):
    hi = xsq.astype(BF16)
    lo = (xsq - hi.astype(F32)).astype(BF16)
    return (jnp.dot(hi, gm, preferred_element_type=F32)
            + jnp.dot(lo, gm, preferred_element_type=F32))


def _rot_norm(x, cos, sin_signed, g, gm, scale):
    y = x * lax.rsqrt(_group_mean(x * x, gm) + EPS) * g
    lane = lax.broadcasted_iota(jnp.int32, y.shape, 1)
    lower_half = (lane % A_DK) < (A_DK // 2)
    partner = jnp.where(lower_half,
                        pltpu.roll(y, LANES - A_DK // 2, 1),
                        pltpu.roll(y, A_DK // 2, 1))
    return (y * cos + partner * sin_signed) * scale


def _rot_norm_kernel(p_ref, cos_ref, sin_ref, g_ref, gm_ref, o_ref, *, scale):
    for hh in range(p_ref.shape[1] // LANES):
        cols = slice(hh * LANES, (hh + 1) * LANES)
        o_ref[:, cols] = _rot_norm(p_ref[:, cols], cos_ref[...], sin_ref[...], g_ref[...],
                                   gm_ref[...], scale).astype(o_ref.dtype)


def _group_mean_matrix():
    lane = jnp.arange(LANES)
    return jnp.where((lane[:, None] // A_DK) == (lane[None, :] // A_DK), 1.0 / A_DK, 0.0).astype(BF16)


def _rot_norm_rows(proj, block0, cos, sin_signed, gain, scale, dtype):
    rows, period = proj.shape[0], cos.shape[0]
    assert rows % period == 0
    tm = _pick_tile(period, ROT_ROW_TILE, SUBLANES_BF16)
    nt = period // tm
    hps = ROT_HEADS_PER_STEP
    assert A_HEADS % hps == 0 and block0 % hps == 0
    wide = hps * LANES
    g = jnp.tile(gain, LANES // A_DK).reshape(1, LANES)
    return pl.pallas_call(
        functools.partial(_rot_norm_kernel, scale=scale),
        grid=(rows // tm, A_HEADS // hps),
        in_specs=[
            pl.BlockSpec((tm, wide), lambda i, h: (i, block0 // hps + h)),
            pl.BlockSpec((tm, LANES), lambda i, h: (i % nt, 0)),
            pl.BlockSpec((tm, LANES), lambda i, h: (i % nt, 0)),
            pl.BlockSpec((1, LANES), lambda i, h: (0, 0)),
            pl.BlockSpec((LANES, LANES), lambda i, h: (0, 0)),
        ],
        out_specs=pl.BlockSpec((tm, wide), lambda i, h: (i, h)),
        out_shape=jax.ShapeDtypeStruct((rows, A_HEADS * LANES), dtype),
        compiler_params=_params("parallel", "arbitrary"),
        name="rot_norm_rows",
    )(proj, cos, sin_signed, g, _group_mean_matrix())


def _key_value_head_major_kernel(p_ref, cos_ref, sin_ref, g_ref, gm_ref, v_ref, k_out_ref, v_out_ref):
    for hh in range(k_out_ref.shape[0]):
        cols = slice(hh * LANES, (hh + 1) * LANES)
        k_out_ref[hh] = _rot_norm(p_ref[:, cols], cos_ref[...], sin_ref[...], g_ref[...],
                                  gm_ref[...], 1.0)
        v_out_ref[hh] = v_ref[:, cols]


def _key_value_head_major(proj, k_block0, v_block0, cos, sin_signed, gain, bp, seq):
    tr = _pick_tile(seq, ROT_ROW_TILE, 8)
    nt = seq // tr
    g = jnp.tile(gain, LANES // A_DK).reshape(1, LANES)
    hps = ROT_HEADS_PER_STEP
    assert A_HEADS % hps == 0 and k_block0 % hps == 0 and v_block0 % hps == 0
    wide = hps * LANES
    hm_spec = pl.BlockSpec((None, hps, tr, LANES), lambda b, i, h: (b, h, i, 0))
    hm_shape = jax.ShapeDtypeStruct((bp, A_HEADS, seq, LANES), F32)
    return pl.pallas_call(
        _key_value_head_major_kernel,
        grid=(bp, nt, A_HEADS // hps),
        in_specs=[
            pl.BlockSpec((tr, wide), lambda b, i, h: (b * nt + i, k_block0 // hps + h)),
            pl.BlockSpec((tr, LANES), lambda b, i, h: (i, 0)),
            pl.BlockSpec((tr, LANES), lambda b, i, h: (i, 0)),
            pl.BlockSpec((1, LANES), lambda b, i, h: (0, 0)),
            pl.BlockSpec((LANES, LANES), lambda b, i, h: (0, 0)),
            pl.BlockSpec((tr, wide), lambda b, i, h: (b * nt + i, v_block0 // hps + h)),
        ],
        out_specs=[hm_spec, hm_spec],
        out_shape=[hm_shape, hm_shape],
        compiler_params=_params("parallel", "parallel", "arbitrary"),
        name="key_value_head_major",
    )(proj, cos, sin_signed, g, _group_mean_matrix(), proj)


def _diff_lambda(lamv, lam_init):
    t1 = jnp.sum(lamv[0:1] * lamv[1:2], axis=-1, keepdims=True)
    t2 = jnp.sum(lamv[2:3] * lamv[3:4], axis=-1, keepdims=True)
    return jnp.exp(t1) - jnp.exp(t2) + lam_init


def _diff_attn_kernel(lamv_ref, q_ref, k_ref, v_ref, g_ref, o_ref, kb_ref, vt_ref, *, lam_init):
    tq = q_ref.shape[0]
    hps, n_kv = vt_ref.shape[:2]
    qi = pl.program_id(2)

    def head_cols(hh):
        return slice(hh * LANES, (hh + 1) * LANES)

    @pl.when(qi == 0)
    def _():
        for hh in range(hps):
            kb_ref[hh] = k_ref[hh].astype(BF16)
            for j in range(n_kv):
                vt_ref[hh, j] = v_ref[hh, j * tq:(j + 1) * tq, :].T.astype(BF16)

    lane = lax.broadcasted_iota(jnp.int32, (tq, LANES), 1)
    chains, qs = [], []
    for hh in range(hps):
        q = q_ref[:, head_cols(hh)]
        zero = jnp.zeros_like(q)
        for c in range(2):
            chains.append(hh)
            qs.append(jnp.where((lane < A_DK) if c == 0 else (lane >= A_DK), q, zero))
    key = lax.broadcasted_iota(jnp.int32, (tq, tq), 0)
    qry = lax.broadcasted_iota(jnp.int32, (tq, tq), 1)

    def block(j, carry, diagonal):
        r = pl.multiple_of(j * tq, tq)
        scores = [lax.dot_general(kb_ref[hh, pl.ds(r, tq), :], qc, NT_DIMS,
                                  preferred_element_type=F32)
                  for hh, qc in zip(chains, qs)]
        probs, stats = [], []
        for s, (m_prev, l_prev, _) in zip(scores, carry):
            if diagonal:
                s = jnp.where(key <= qry, s, -jnp.inf)
            m_new = jnp.maximum(m_prev, jnp.max(s, axis=0, keepdims=True))
            alpha = jnp.exp2(m_prev - m_new)
            p = jnp.exp2(s - m_new)
            stats.append((m_new, alpha * l_prev + jnp.sum(p, axis=0, keepdims=True), alpha))
            probs.append(p.astype(BF16))
        return tuple(
            (m_new, l_new, alpha * acc + jnp.dot(vt_ref[hh, j], p, preferred_element_type=F32))
            for hh, p, (m_new, l_new, alpha), (_, _, acc) in zip(chains, probs, stats, carry))

    init = tuple((jnp.full((1, tq), -jnp.inf, F32), jnp.zeros((1, tq), F32),
                  jnp.zeros((A_DV, tq), F32)) for _ in chains)
    carry = lax.fori_loop(0, qi, lambda j, c: block(j, c, False), init)
    carry = block(qi, carry, True)

    lam = _diff_lambda(lamv_ref[...], lam_init)
    for hh in range(hps):
        (_, l0, acc0), (_, l1, acc1) = carry[2 * hh], carry[2 * hh + 1]
        o = acc0 / l0 - lam * (acc1 / l1)
        ms = jnp.mean(o * o, axis=0, keepdims=True)
        o = o * lax.rsqrt(ms + EPS) * g_ref[...] * (1.0 - lam_init)
        o_ref[:, head_cols(hh)] = o.T.astype(o_ref.dtype)


def _diff_attn_prompt(q_all, k_hm, v_hm, lamv, g_subln, bp, seq, lam_init):
    tq = _pick_tile(seq, ATTN_TILE, LANES)
    nq = seq // tq
    hps = ATTN_HEADS_PER_STEP
    assert A_HEADS % hps == 0
    wide = hps * LANES
    return pl.pallas_call(
        functools.partial(_diff_attn_kernel, lam_init=lam_init),
        grid=(bp, A_HEADS // hps, nq),
        in_specs=[
            pl.BlockSpec(lamv.shape, lambda b, h, i: (0, 0)),
            pl.BlockSpec((tq, wide), lambda b, h, i: (b * nq + i, h)),
            pl.BlockSpec((None, hps, seq, LANES), lambda b, h, i: (b, h, 0, 0)),
            pl.BlockSpec((None, hps, seq, LANES), lambda b, h, i: (b, h, 0, 0)),
            pl.BlockSpec((A_DV, 1), lambda b, h, i: (0, 0)),
        ],
        out_specs=pl.BlockSpec((tq, wide), lambda b, h, i: (b * nq + i, h)),
        out_shape=jax.ShapeDtypeStruct((bp * seq, A_HEADS * A_DV), BF16),
        scratch_shapes=[pltpu.VMEM((hps, seq, LANES), BF16),
                        pltpu.VMEM((hps, nq, A_DV, tq), BF16)],
        compiler_params=_params("parallel", "parallel", "arbitrary"),
        name="diff_attn_prompt",
    )(lamv, q_all, k_hm, v_hm, g_subln.reshape(A_DV, 1))


def _diff_attn_decode_kernel(pt_ref, lamv_ref, q_ref, ks_ref, vs_ref, g_ref, *rest,
                             pages, lam_init, n_steps):
    del pt_ref
    k_refs = rest[:pages]
    v_refs = rest[pages:2 * pages]
    o_ref = rest[2 * pages]
    qh_ref, m_ref, l_ref, acc_ref, kb_ref, vb_ref = rest[2 * pages + 1:]
    ls = q_ref.shape[0]
    rph = 2 * ls
    rows = A_HEADS * rph
    step = pl.program_id(1)

    def head_cols(h):
        return slice(h * A_DV, (h + 1) * A_DV)

    def per_head_rows(x):
        return jnp.concatenate([x[:, head_cols(h)] for h in range(A_HEADS) for _ in range(2)],
                               axis=0)

    @pl.when(step == 0)
    def _():
        q = q_ref[...].astype(F32)
        lane = lax.broadcasted_iota(jnp.int32, (ls, LANES), 1)
        for h in range(A_HEADS):
            qh = q[:, head_cols(h)]
            qh_ref[h * rph:h * rph + ls, :] = jnp.where(lane < A_DK, qh, 0.0)
            qh_ref[h * rph + ls:(h + 1) * rph, :] = jnp.where(lane >= A_DK, qh, 0.0)
        m_ref[...] = jnp.full(m_ref.shape, -jnp.inf, F32)
        l_ref[...] = jnp.zeros(l_ref.shape, F32)
        acc_ref[...] = jnp.zeros(acc_ref.shape, F32)

    for h in range(A_HEADS):
        for p in range(pages):
            tok = slice(p * PAGE_SIZE, (p + 1) * PAGE_SIZE)
            kb_ref[h, tok, :] = k_refs[p][h].astype(BF16)
            vb_ref[h, tok, :] = v_refs[p][h].astype(BF16)

    s = jnp.concatenate(
        [lax.dot_general(qh_ref[h * rph:(h + 1) * rph, :].astype(BF16), kb_ref[h], NT_DIMS,
                         preferred_element_type=F32) for h in range(A_HEADS)], axis=0)
    m_prev = m_ref[...]
    m_new = jnp.maximum(m_prev, jnp.max(s, axis=-1, keepdims=True))
    alpha = jnp.exp2(m_prev - m_new)
    p_exp = jnp.exp2(s - m_new)
    l_ref[...] = alpha * l_ref[...] + jnp.sum(p_exp, axis=-1, keepdims=True)
    pv = jnp.concatenate(
        [jnp.dot(p_exp[h * rph:(h + 1) * rph].astype(BF16), vb_ref[h],
                 preferred_element_type=F32) for h in range(A_HEADS)], axis=0)
    acc_ref[...] = alpha * acc_ref[...] + pv
    m_ref[...] = m_new

    @pl.when(step == n_steps - 1)
    def _():
        qf = qh_ref[...]
        ks = ks_ref[...].astype(BF16).astype(F32)
        vs = vs_ref[...].astype(BF16).astype(F32)
        row = lax.broadcasted_iota(jnp.int32, (rows, 1), 0)
        q_of_row = row % ls
        s_new = []
        for t in range(ls):
            k_t = per_head_rows(jnp.broadcast_to(ks[t:t + 1], ks.shape))
            st = jnp.sum(qf * k_t, axis=-1, keepdims=True)
            s_new.append(jnp.where(q_of_row >= t, st, -jnp.inf))
        m_old = m_ref[...]
        m_fin = functools.reduce(jnp.maximum, s_new, m_old)
        a_fin = jnp.exp2(m_old - m_fin)
        l_fin = a_fin * l_ref[...]
        acc = a_fin * acc_ref[...]
        for t in range(ls):
            pt = jnp.exp2(s_new[t] - m_fin)
            l_fin = l_fin + pt
            v_t = per_head_rows(jnp.broadcast_to(vs[t:t + 1], vs.shape))
            acc = acc + pt.astype(BF16).astype(F32) * v_t

        lam = _diff_lambda(lamv_ref[...], lam_init)
        second = (row % rph) >= ls
        acc = acc * (jnp.where(second, -lam, 1.0) / l_fin)
        g = g_ref[...]
        for h in range(A_HEADS):
            o = acc[h * rph:h * rph + ls] + acc[h * rph + ls:(h + 1) * rph]
            o_ref[:, head_cols(h)] = _rmsnorm_lanes(o, g) * (1.0 - lam_init)


def _diff_attn_decode(q_s, k_s, v_s, cache_k, cache_v, page_table, lamv, g_subln, layer, lam_init):
    bs, ls, width = q_s.shape
    n_pages = page_table.shape[1]
    pages = _pick_tile(n_pages, DEC_PAGES_PER_STEP, 1)
    n_steps = n_pages // pages
    rows = ls * 2 * A_HEADS
    assert (2 * ls) % 8 == 0 and cache_k.shape[2:] == (PAGE_SIZE, A_HEADS, 2 * A_DK)

    cache_k = jnp.transpose(cache_k, (0, 1, 3, 2, 4))
    cache_v = jnp.transpose(cache_v, (0, 1, 3, 2, 4))

    def page_spec(p):
        return pl.BlockSpec((None, None, A_HEADS, PAGE_SIZE, LANES),
                            lambda b, s, pt: (layer, pt[b * n_pages + s * pages + p], 0, 0, 0))

    def per_batch():
        return pl.BlockSpec((None, ls, width), lambda b, s, pt: (b, 0, 0))

    grid_spec = pltpu.PrefetchScalarGridSpec(
        num_scalar_prefetch=1,
        grid=(bs, n_steps),
        in_specs=[pl.BlockSpec(lamv.shape, lambda b, s, pt: (0, 0)),
                  per_batch(), per_batch(), per_batch(),
                  pl.BlockSpec((1, LANES), lambda b, s, pt: (0, 0))]
                 + [page_spec(p) for p in range(pages)]
                 + [page_spec(p) for p in range(pages)],
        out_specs=per_batch(),
        scratch_shapes=[pltpu.VMEM((rows, LANES), F32),
                        pltpu.VMEM((rows, 1), F32),
                        pltpu.VMEM((rows, 1), F32),
                        pltpu.VMEM((rows, A_DV), F32),
                        pltpu.VMEM((A_HEADS, pages * PAGE_SIZE, LANES), BF16),
                        pltpu.VMEM((A_HEADS, pages * PAGE_SIZE, A_DV), BF16)],
    )
    return pl.pallas_call(
        functools.partial(_diff_attn_decode_kernel, pages=pages, lam_init=lam_init, n_steps=n_steps),
        grid_spec=grid_spec,
        out_shape=jax.ShapeDtypeStruct((bs, ls, width), F32),
        compiler_params=_params("parallel", "arbitrary"),
        name="diff_attn_decode",
    )(page_table.reshape(-1), lamv, q_s, k_s, v_s, g_subln.reshape(1, LANES),
      *([cache_k] * pages), *([cache_v] * pages))


def _mem_attn_kernel(q_ref, k_ref, v_ref, g_ref, o_ref):
    g = g_ref[...]
    cols = [slice(h * MEM_HD, (h + 1) * MEM_HD) for h in range(MEM_HEADS)]
    scores = [lax.dot_general(_rmsnorm_lanes(q_ref[:, c], g).astype(BF16),
                              k_ref[:, c].astype(BF16), NT_DIMS, preferred_element_type=F32)
              for c in cols]
    probs = []
    for s in scores:
        s = s * (MEM_HD ** -0.5)
        e = jnp.exp(s - jnp.max(s, axis=-1, keepdims=True))
        probs.append((e / jnp.sum(e, axis=-1, keepdims=True)).astype(BF16))
    for c, p in zip(cols, probs):
        o_ref[:, c] = jnp.dot(p, v_ref[:, c].astype(BF16),
                              preferred_element_type=F32).astype(o_ref.dtype)


def _mem_attn_prompt(proj, q_block0, k_norm, kv, gq, bp, seq):
    mem = k_norm.shape[0] // bp
    tq = _pick_tile(seq, 512, LANES)
    nq = seq // tq
    wide = MEM_HEADS * MEM_HD
    assert q_block0 % MEM_HEADS == 0
    return pl.pallas_call(
        _mem_attn_kernel,
        grid=(bp, nq),
        in_specs=[
            pl.BlockSpec((tq, wide), lambda b, i: (b * nq + i, q_block0 // MEM_HEADS)),
            pl.BlockSpec((mem, wide), lambda b, i: (b, 0)),
            pl.BlockSpec((mem, wide), lambda b, i: (b, 1)),
            pl.BlockSpec((1, LANES), lambda b, i: (0, 0)),
        ],
        out_specs=pl.BlockSpec((tq, wide), lambda b, i: (b * nq + i, 0)),
        out_shape=jax.ShapeDtypeStruct((bp * seq, wide), BF16),
        compiler_params=_params("parallel", "arbitrary"),
        name="mem_attn_prompt",
    )(proj, k_norm, kv, gq.reshape(1, LANES))


def _mem_attn_interleaved_kernel(q_ref, k_ref, v_ref, g_ref, o_ref):
    g = g_ref[...]
    k = k_ref[...].astype(BF16)
    v = v_ref[...].astype(BF16)
    ls, rows = q_ref.shape[0], k.shape[0]
    head_of_row = lax.broadcasted_iota(jnp.int32, (ls, rows), 1) % MEM_HEADS
    cols = [slice(h * MEM_HD, (h + 1) * MEM_HD) for h in range(MEM_HEADS)]
    scores = [lax.dot_general(_rmsnorm_lanes(q_ref[:, c], g).astype(BF16), k, NT_DIMS,
                              preferred_element_type=F32) for c in cols]
    probs = []
    for h, s in enumerate(scores):
        s = jnp.where(head_of_row == h, s * (MEM_HD ** -0.5), -jnp.inf)
        e = jnp.exp(s - jnp.max(s, axis=-1, keepdims=True))
        probs.append((e / jnp.sum(e, axis=-1, keepdims=True)).astype(BF16))
    for c, p in zip(cols, probs):
        o_ref[:, c] = jnp.dot(p, v, preferred_element_type=F32).astype(o_ref.dtype)


def _mem_attn_sample(proj_s, q_block0, cache_k, cache_v, gq, layer):
    bs, ls, _ = proj_s.shape
    mem = cache_k.shape[2]
    wide = MEM_HEADS * MEM_HD
    assert q_block0 % MEM_HEADS == 0
    bps = _pick_tile(bs, MEM_SAMPLE_BATCHES_PER_STEP, 1)

    def several_batches(q_ref, k_ref, v_ref, g_ref, o_ref):
        for bi in range(bps):
            _mem_attn_interleaved_kernel(q_ref.at[bi], k_ref.at[bi], v_ref.at[bi], g_ref,
                                         o_ref.at[bi])

    return pl.pallas_call(
        several_batches,
        grid=(bs // bps,),
        in_specs=[
            pl.BlockSpec((bps, ls, wide), lambda b: (b, 0, q_block0 // MEM_HEADS)),
            pl.BlockSpec((None, bps, mem, MEM_HD), lambda b: (layer, b, 0, 0)),
            pl.BlockSpec((None, bps, mem, MEM_HD), lambda b: (layer, b, 0, 0)),
            pl.BlockSpec((1, LANES), lambda b: (0, 0)),
        ],
        out_specs=pl.BlockSpec((bps, ls, wide), lambda b: (b, 0, 0)),
        out_shape=jax.ShapeDtypeStruct((bs, ls, wide), F32),
        compiler_params=_params("arbitrary"),
        name="mem_attn_sample",
    )(proj_s, cache_k, cache_v, gq.reshape(1, LANES))


def _mem_k_norm_kernel(k_ref, g_ref, o_ref):
    o_ref[...] = _rmsnorm_lanes(k_ref[...], g_ref[...])


def _mem_k_norm(kv, gk):
    rows = kv.shape[0]
    tm = _pick_tile(rows, 256, 8)
    return pl.pallas_call(
        _mem_k_norm_kernel,
        grid=(rows // tm, MEM_HEADS),
        in_specs=[pl.BlockSpec((tm, LANES), lambda i, h: (i, h)),
                  pl.BlockSpec((1, LANES), lambda i, h: (0, 0))],
        out_specs=pl.BlockSpec((tm, LANES), lambda i, h: (i, h)),
        out_shape=jax.ShapeDtypeStruct((rows, MEM_HEADS * MEM_HD), F32),
        compiler_params=_params("parallel", "arbitrary"),
        name="mem_k_norm",
    )(kv, gk.reshape(1, LANES))


def _split3(x):
    p1 = x.astype(BF16)
    r1 = x - p1.astype(F32)
    p2 = r1.astype(BF16)
    p3 = (r1 - p2.astype(F32)).astype(BF16)
    return p1, p2, p3


def _hgrn_kernel(*refs, chunk, sub, valid, layer, has_state):
    if has_state:
        (q_ref, f_ref, v_ref, gate_ref, lb_ref, g_ref, tri_ref, s0_ref,
         o_ref, s_out_ref, st_ref, b_scr, kk_scr) = refs
    else:
        (q_ref, f_ref, v_ref, gate_ref, lb_ref, g_ref, tri_ref,
         o_ref, s_out_ref, st_ref, b_scr, kk_scr) = refs
    tl = q_ref.shape[0]
    heads = st_ref.shape[0]
    t = pl.program_id(2)

    @pl.when(t == 0)
    def _():
        for hh in range(heads):
            if has_state:
                st_ref[hh] = s0_ref[hh].T
            else:
                st_ref[hh] = jnp.zeros(st_ref.shape[1:], F32)

    lb = lb_ref[...]
    e = jnp.exp(lb - jnp.max(lb, axis=0, keepdims=True))
    sm = e / jnp.sum(e, axis=0, keepdims=True)
    lower = jnp.zeros((1, sm.shape[1]), F32)
    for r in range(1, layer + 1):
        lower = lower + sm[r:r + 1]

    g = g_ref[...]
    n_chunks = tl // chunk
    n_sub = chunk // sub
    chunk_row = lax.broadcasted_iota(jnp.int32, (chunk, 1), 0)

    q_all = q_ref[...]
    v_all = v_ref[...]
    f_all = lower + (1.0 - lower) * jax.nn.sigmoid(f_ref[...])
    kk_all = 1.0 - f_all
    tri = tri_ref[...]
    p1, p2, p3 = _split3(jnp.log(f_all))
    b_all = (jnp.dot(tri, p1, preferred_element_type=F32)
             + jnp.dot(tri, p2, preferred_element_type=F32)
             + jnp.dot(tri, p3, preferred_element_type=F32))
    b_all = b_all * LOG2_E
    v16_all = v_all.astype(BF16)

    units = [(hh, ci) for hh in range(heads) for ci in range(n_chunks)]

    def unit_block(x, unit):
        hh, ci = unit
        return x[ci * chunk:(ci + 1) * chunk, hh * LANES:(hh + 1) * LANES]

    incs, atts, b_lasts = [], [], []
    for u in units:
        q, b, kk = unit_block(q_all, u), unit_block(b_all, u), unit_block(kk_all, u)
        b_last = b[valid - 1:valid]
        k_dec = jnp.where(chunk_row < valid, kk * jnp.exp2(b_last - b), 0.0)
        incs.append(lax.dot_general(unit_block(v16_all, u), k_dec.astype(BF16), TN_DIMS,
                                    preferred_element_type=F32))
        b_lasts.append(b_last)
        for i in range(1, n_sub):
            rs = slice(i * sub, (i + 1) * sub)
            b_ref = b[i * sub - 1:i * sub]
            q_dec = (q[rs] * jnp.exp2(b[rs] - b_ref)).astype(BF16)
            k_dec = (kk[:i * sub] * jnp.exp2(b_ref - b[:i * sub])).astype(BF16)
            atts.append(lax.dot_general(q_dec, k_dec, NT_DIMS, preferred_element_type=F32))

    b_scr[...] = b_all
    kk_scr[...] = kk_all
    causal = [jnp.where(lax.broadcasted_iota(jnp.int32, (sub, LANES), 0) >= s, 0.0, -jnp.inf)
              for s in range(sub)]
    diag = []
    for u in units:
        hh, ci = u
        cols = slice(hh * LANES, (hh + 1) * LANES)
        q, b = unit_block(q_all, u), unit_block(b_all, u)
        for i in range(n_sub):
            rs = slice(i * sub, (i + 1) * sub)
            qb, bb = q[rs], b[rs]
            terms = []
            for s in range(sub):
                row = ci * chunk + i * sub + s
                d = (bb - b_scr[row:row + 1, cols]) + causal[s]
                w = jnp.sum(qb * kk_scr[row:row + 1, cols] * jnp.exp2(d), axis=-1, keepdims=True)
                terms.append(w * v_ref[row:row + 1, cols])
            while len(terms) > 1:
                terms = [a + c for a, c in zip(terms[0::2], terms[1::2])]
            diag.append(terms[0])

    intra = []
    for n, u in enumerate(units):
        v16 = unit_block(v16_all, u)
        for i in range(n_sub):
            o_i = diag[n * n_sub + i]
            if i > 0:
                att = atts[n * (n_sub - 1) + i - 1]
                o_i = o_i + jnp.dot(att.astype(BF16), v16[:i * sub], preferred_element_type=F32)
            intra.append(o_i)

    states = [st_ref[hh] for hh in range(heads)]
    for n, u in enumerate(units):
        hh, ci = u
        q, b = unit_block(q_all, u), unit_block(b_all, u)
        o_inter = lax.dot_general((q * jnp.exp2(b)).astype(BF16), states[hh].astype(BF16),
                                  NT_DIMS, preferred_element_type=F32)
        states[hh] = states[hh] * jnp.exp2(b_lasts[n]) + incs[n]
        parts = intra[n * n_sub:(n + 1) * n_sub]
        o = o_inter + (parts[0] if n_sub == 1 else jnp.concatenate(parts, axis=0))
        rows, cols = slice(ci * chunk, (ci + 1) * chunk), slice(hh * LANES, (hh + 1) * LANES)
        gate = gate_ref[rows, cols]
        o_ref[rows, cols] = (_rmsnorm_lanes(o, g)
                             * (gate * jax.nn.sigmoid(gate))).astype(o_ref.dtype)
    for hh in range(heads):
        st_ref[hh] = states[hh]

    @pl.when(t == pl.num_programs(2) - 1)
    def _():
        for hh in range(heads):
            s_out_ref[hh] = states[hh].T


def _tri(rows, chunk):
    r = jnp.arange(rows)
    same = (r[:, None] // chunk) == (r[None, :] // chunk)
    return (same & (r[:, None] >= r[None, :])).astype(BF16)


def _hgrn_prompt(proj, lb_logits, g_out, bp, seq, layer):
    chunk = math.gcd(seq, B_CHUNK)
    sub = math.gcd(chunk, B_SUB)
    tl = _pick_tile(seq, 256, chunk)
    nt = seq // tl
    h_ = B_HEADS
    hps = HGRN_PROMPT_HEADS_PER_STEP
    assert h_ % hps == 0
    wide = hps * LANES

    def col(block0):
        return pl.BlockSpec((tl, wide), lambda b, h, t: (b * nt + t, block0 // hps + h))

    return pl.pallas_call(
        functools.partial(_hgrn_kernel, chunk=chunk, sub=sub, valid=chunk, layer=layer,
                          has_state=False),
        grid=(bp, h_ // hps, nt),
        in_specs=[col(0), col(h_), col(2 * h_), col(3 * h_),
                  pl.BlockSpec((lb_logits.shape[0], wide), lambda b, h, t: (0, h)),
                  pl.BlockSpec((1, LANES), lambda b, h, t: (0, 0)),
                  pl.BlockSpec((tl, tl), lambda b, h, t: (0, 0))],
        out_specs=[pl.BlockSpec((tl, wide), lambda b, h, t: (b * nt + t, h)),
                   pl.BlockSpec((None, hps, B_DK, B_DV), lambda b, h, t: (b, h, 0, 0))],
        out_shape=[jax.ShapeDtypeStruct((bp * seq, h_ * B_DV), BF16),
                   jax.ShapeDtypeStruct((bp, h_, B_DK, B_DV), F32)],
        scratch_shapes=[pltpu.VMEM((hps, B_DV, B_DK), F32),
                        pltpu.VMEM((tl, wide), F32), pltpu.VMEM((tl, wide), F32)],
        compiler_params=_params("parallel", "parallel", "arbitrary"),
        name="hgrn_prompt",
    )(proj, proj, proj, proj, lb_logits, g_out.reshape(1, LANES), _tri(tl, chunk))


def _hgrn_sample(proj_s, state, lb_logits, g_out, layer, state_layer):
    bs, ls, n = proj_s.shape
    chunk = 8
    assert ls <= chunk
    padded = jnp.pad(proj_s, ((0, 0), (0, chunk - ls), (0, 0)))
    h_ = B_HEADS
    hps = HGRN_SAMPLE_HEADS_PER_STEP
    assert h_ % hps == 0
    wide = hps * LANES

    def col(block0):
        return pl.BlockSpec((None, chunk, wide), lambda b, h, t: (b, 0, block0 // hps + h))

    o, s_new = pl.pallas_call(
        functools.partial(_hgrn_kernel, chunk=chunk, sub=chunk, valid=ls, layer=layer,
                          has_state=True),
        grid=(bs, h_ // hps, 1),
        in_specs=[col(0), col(h_), col(2 * h_), col(3 * h_),
                  pl.BlockSpec((lb_logits.shape[0], wide), lambda b, h, t: (0, h)),
                  pl.BlockSpec((1, LANES), lambda b, h, t: (0, 0)),
                  pl.BlockSpec((chunk, chunk), lambda b, h, t: (0, 0)),
                  pl.BlockSpec((None, None, hps, B_DK, B_DV),
                               lambda b, h, t: (state_layer, b, h, 0, 0))],
        out_specs=[pl.BlockSpec((None, chunk, wide), lambda b, h, t: (b, 0, h)),
                   pl.BlockSpec((None, hps, B_DK, B_DV), lambda b, h, t: (b, h, 0, 0))],
        out_shape=[jax.ShapeDtypeStruct((bs, chunk, h_ * B_DV), F32),
                   jax.ShapeDtypeStruct((bs, h_, B_DK, B_DV), F32)],
        scratch_shapes=[pltpu.VMEM((hps, B_DV, B_DK), F32),
                        pltpu.VMEM((chunk, wide), F32), pltpu.VMEM((chunk, wide), F32)],
        compiler_params=_params("parallel", "parallel", "arbitrary"),
        name="hgrn_sample",
    )(padded, padded, padded, padded, lb_logits, g_out.reshape(1, LANES), _tri(chunk, chunk), state)
    return o[:, :ls], s_new


def _rope_tables(pos):
    half = A_DK // 2
    inv_freq = ROPE_THETA ** (-jnp.arange(half, dtype=F32) / half)
    ang = pos.astype(F32)[:, None] * inv_freq[None, :]
    cos, sin = jnp.cos(ang), jnp.sin(ang)
    reps = LANES // A_DK
    return (jnp.tile(cos, (1, 2 * reps)), jnp.tile(jnp.concatenate([-sin, sin], axis=1), (1, reps)))


def kernel(x_prompt, x_sample, cache_attn_k, cache_attn_v, state_hgrn, cache_mem_k, cache_mem_v,
           page_table, mem_prompt, norm_ffn, w_ffn_gate, w_ffn_up, w_ffn_down, norm_mix, norm_mem,
           w_mem_kv, gq_mem, gk_mem, w_out, w_in_attn, gq_attn, gk_attn, lam_q1, lam_k1, lam_q2,
           lam_k2, g_subln, w_in_hgrn, lb_logits, g_hgrn_out):
    bp, seq, d = x_prompt.shape
    bs, ls, _ = x_sample.shape
    depth = norm_mix.shape[0]
    mem = mem_prompt.shape[1]
    mp = bp * seq
    ms = bs * ls
    past_len = page_table.shape[1] * PAGE_SIZE
    mem_w = MEM_HEADS * MEM_HD
    qk_w = A_HEADS * 2 * A_DK
    v_w = A_HEADS * A_DV

    xp = x_prompt.reshape(mp, d)
    xs = x_sample.reshape(ms, d)
    cos_p, sin_p = _rope_tables(jnp.arange(seq))
    cos_s, sin_s = _rope_tables(jnp.tile(past_len + jnp.arange(ls), bs))
    mem_rows = mem_prompt.reshape(bp * mem, d)
    cmk = cache_mem_k.reshape(depth, bs, mem * MEM_HEADS, MEM_HD)
    cmv = cache_mem_v.reshape(depth, bs, mem * MEM_HEADS, MEM_HD)

    k_rows_p, v_rows_p, k_rows_s, v_rows_s = [], [], [], []
    st_p, st_s, mem_k_new, mem_v_new = [], [], [], []
    for i in range(depth):
        xp, xs = _ffn_half(xp, xs, norm_ffn, w_ffn_gate, w_ffn_up, w_ffn_down, i, 0)

        kv = _norm_matmul(mem_rows, None, norm_mem, i, w_mem_kv, i)
        k_norm = _mem_k_norm(kv, gk_mem[i])
        mem_k_new.append(k_norm.reshape(bp, mem, MEM_HEADS, MEM_HD))
        mem_v_new.append(kv[:, mem_w:].reshape(bp, mem, MEM_HEADS, MEM_HD))

        if i % 2 == 0:
            a = i // 2
            lam_init = 0.8 - 0.6 * math.exp(-0.3 * i)
            proj, proj_s2 = _norm_matmul(xp, xs, norm_mix, i, w_in_attn, a)
            q_scale = A_DK ** -0.5 * LOG2_E
            q_p = _rot_norm_rows(proj, 0, cos_p, sin_p, gq_attn[a], q_scale, BF16)
            q_s = _rot_norm_rows(proj_s2, 0, cos_s, sin_s, gq_attn[a], q_scale, BF16)
            k_hm, v_hm = _key_value_head_major(proj, A_HEADS, 2 * A_HEADS, cos_p, sin_p,
                                               gk_attn[a], bp, seq)
            k_s = _rot_norm_rows(proj_s2, A_HEADS, cos_s, sin_s, gk_attn[a], 1.0, F32)
            lamv = jnp.stack([lam_q1[a], lam_k1[a], lam_q2[a], lam_k2[a]])
            o_p = _diff_attn_prompt(q_p, k_hm, v_hm, lamv, g_subln[a], bp, seq, lam_init)
            k_rows_p.append(jnp.transpose(k_hm, (0, 2, 1, 3)))
            v_rows_p.append(jnp.transpose(v_hm, (0, 2, 1, 3)))
            proj_s = proj_s2.reshape(bs, ls, proj_s2.shape[1])
            q_s = q_s.reshape(bs, ls, qk_w)
            k_s = k_s.reshape(bs, ls, qk_w)
            v_s = proj_s[..., 2 * qk_w:2 * qk_w + v_w]
            o_s = _diff_attn_decode(q_s, k_s, v_s, cache_attn_k, cache_attn_v, page_table, lamv,
                                    g_subln[a], a, lam_init)
            k_rows_s.append(k_s.reshape(bs, ls, A_HEADS, 2 * A_DK))
            v_rows_s.append(v_s.reshape(bs, ls, A_HEADS, A_DV))
            mq_block0 = (2 * qk_w + v_w) // LANES
        else:
            j = i // 2
            proj, proj_s2 = _norm_matmul(xp, xs, norm_mix, i, w_in_hgrn, j)
            proj_s = proj_s2.reshape(bs, ls, proj_s2.shape[1])
            o_p, s_p = _hgrn_prompt(proj, lb_logits, g_hgrn_out[j], bp, seq, i)
            o_s, s_s = _hgrn_sample(proj_s, state_hgrn, lb_logits, g_hgrn_out[j], i, j)
            st_p.append(s_p)
            st_s.append(s_s)
            mq_block0 = (2 * B_HEADS * B_DK + 2 * B_HEADS * B_DV) // LANES

        m_p = _mem_attn_prompt(proj, mq_block0, k_norm, kv, gq_mem[i], bp, seq)
        m_s = _mem_attn_sample(proj_s, mq_block0, cmk, cmv, gq_mem[i], i)
        xp, xs = _out_proj(xp, xs, o_p, o_s.reshape(ms, -1), m_p, m_s.reshape(ms, -1), w_out, i)

        xp, xs = _ffn_half(xp, xs, norm_ffn, w_ffn_gate, w_ffn_up, w_ffn_down, i, 1)

    return (xp.reshape(bp, seq, d), xs.reshape(bs, ls, d),
            jnp.stack(k_rows_p), jnp.stack(v_rows_p), jnp.stack(k_rows_s), jnp.stack(v_rows_s),
            jnp.stack(st_p), jnp.stack(st_s), jnp.stack(mem_k_new), jnp.stack(mem_v_new))
```

```python
import functools
import math

import jax
import jax.numpy as jnp
from jax import lax
from jax.experimental import pallas as pl
from jax.experimental.pallas import tpu as pltpu

F32 = jnp.float32
BF16 = jnp.bfloat16

EPS = 1e-6
LOG2_E = math.log2(math.e)
ROPE_THETA = 10000.0
A_HEADS = 12
A_DK = 64
A_DV = 128
B_HEADS = 12
B_DK = 128
B_DV = 128
B_CHUNK = 64
B_SUB = 8
MEM_HEADS = 4
MEM_HD = 128
PAGE_SIZE = 128
LANES = 128
SUBLANES_BF16 = 16
V7X_VMEM_LIMIT_BYTES = 56 * 1024 * 1024
FFN_ROW_TILE = 1024
PROJ_ROW_TILE = 2048
ROT_ROW_TILE = 1024
ROT_HEADS_PER_STEP = 4
MEM_SAMPLE_BATCHES_PER_STEP = 4
COL_TILE = 512
FFN_COL_TILE = 256
NORM_ROWS_PER_ITER = 128
DEC_PAGES_PER_STEP = 8
ATTN_TILE = 512
ATTN_HEADS_PER_STEP = 4
HGRN_SAMPLE_HEADS_PER_STEP = 12
HGRN_PROMPT_HEADS_PER_STEP = 6
NT_DIMS = (((1,), (1,)), ((), ()))
TN_DIMS = (((0,), (0,)), ((), ()))


def _params(*semantics):
    return pltpu.CompilerParams(dimension_semantics=semantics,
                                vmem_limit_bytes=V7X_VMEM_LIMIT_BYTES)


def _pick_tile(n, target, align):
    best = None
    for t in range(align, min(n, target) + 1, align):
        if n % t == 0:
            best = t
    assert best is not None, (n, target, align)
    return best


def _rmsnorm_lanes(x, g):
    ms = jnp.mean(x * x, axis=-1, keepdims=True)
    return x * lax.rsqrt(ms + EPS) * g


def _norm_rows_into(h_ref, x_ref, g_ref, copy_ref=None):
    rows = x_ref.shape[0]
    chunk = _pick_tile(rows, NORM_ROWS_PER_ITER, SUBLANES_BF16 if rows % SUBLANES_BF16 == 0 else 8)
    g = g_ref[...]

    def body(i, carry):
        r = pl.multiple_of(i * chunk, chunk)
        x = x_ref[pl.ds(r, chunk), :]
        h_ref[pl.ds(r, chunk), :] = _rmsnorm_lanes(x, g).astype(h_ref.dtype)
        if copy_ref is not None:
            copy_ref[pl.ds(r, chunk), :] = x
        return carry

    lax.fori_loop(0, rows // chunk, body, 0)


def _held_after_first_row_tile(n_col_blocks):
    return lambda i, j: (0, jnp.where(i == 0, j, n_col_blocks - 1))


def _ffn_kernel(xp_ref, xs_ref, g_ref, wg_ref, wu_ref, wd_ref, op_ref, os_ref, hp_ref, hs_ref,
                *, n_out_chunks):
    i, j = pl.program_id(0), pl.program_id(1)

    @pl.when(j == 0)
    def _():
        _norm_rows_into(hp_ref, xp_ref, g_ref, copy_ref=op_ref)

    @pl.when(jnp.logical_and(i == 0, j == 0))
    def _():
        _norm_rows_into(hs_ref, xs_ref, g_ref, copy_ref=os_ref)

    def accumulate(h_ref, o_ref):
        h = h_ref[...]
        gate = jnp.dot(h, wg_ref[...].astype(BF16), preferred_element_type=F32)
        up = jnp.dot(h, wu_ref[...].astype(BF16), preferred_element_type=F32)
        act = (0.5 * (gate * jax.nn.sigmoid(gate)) * up).astype(BF16)
        width = o_ref.shape[1] // n_out_chunks
        for c in range(n_out_chunks):
            cols = slice(c * width, (c + 1) * width)
            o_ref[:, cols] += jnp.dot(act, wd_ref[:, cols].astype(BF16),
                                      preferred_element_type=F32)

    accumulate(hp_ref, op_ref)

    @pl.when(i == 0)
    def _():
        accumulate(hs_ref, os_ref)


def _ffn_half(xp, xs, norm_ffn, w_gate, w_up, w_down, layer, half):
    mp, d = xp.shape
    ms = xs.shape[0]
    f = w_gate.shape[-1]
    tm = _pick_tile(mp, FFN_ROW_TILE, SUBLANES_BF16)
    tf = _pick_tile(f, FFN_COL_TILE, LANES)
    n_out_chunks = max(1, d // 512)
    g4 = norm_ffn.reshape(norm_ffn.shape[0], 2, 1, d)
    return pl.pallas_call(
        functools.partial(_ffn_kernel, n_out_chunks=n_out_chunks),
        grid=(mp // tm, f // tf),
        in_specs=[
            pl.BlockSpec((tm, d), lambda i, j: (i, 0)),
            pl.BlockSpec((ms, d), lambda i, j: (0, 0), pipeline_mode=pl.Buffered(1)),
            pl.BlockSpec((None, None, 1, d), lambda i, j: (layer, half, 0, 0)),
            pl.BlockSpec((None, None, d, tf), lambda i, j: (layer, half, 0, j)),
            pl.BlockSpec((None, None, d, tf), lambda i, j: (layer, half, 0, j)),
            pl.BlockSpec((None, None, tf, d), lambda i, j: (layer, half, j, 0)),
        ],
        out_specs=[pl.BlockSpec((tm, d), lambda i, j: (i, 0)),
                   pl.BlockSpec((ms, d), lambda i, j: (0, 0))],
        out_shape=[jax.ShapeDtypeStruct((mp, d), F32), jax.ShapeDtypeStruct((ms, d), F32)],
        scratch_shapes=[pltpu.VMEM((tm, d), BF16), pltpu.VMEM((ms, d), BF16)],
        compiler_params=_params("arbitrary", "arbitrary"),
        name="ffn_half",
    )(xp, xs, g4, w_gate, w_up, w_down)


def _norm_matmul_kernel(*refs, with_sample):
    if with_sample:
        xp_ref, xs_ref, g_ref, w_ref, op_ref, os_ref, hp_ref, hs_ref = refs
    else:
        xp_ref, g_ref, w_ref, op_ref, hp_ref = refs
    i, j = pl.program_id(0), pl.program_id(1)

    @pl.when(j == 0)
    def _():
        _norm_rows_into(hp_ref, xp_ref, g_ref)

    op_ref[...] = jnp.dot(hp_ref[...], w_ref[...].astype(BF16), preferred_element_type=F32)

    if with_sample:
        @pl.when(jnp.logical_and(i == 0, j == 0))
        def _():
            _norm_rows_into(hs_ref, xs_ref, g_ref)

        @pl.when(i == 0)
        def _():
            os_ref[...] = jnp.dot(hs_ref[...], w_ref[...].astype(BF16),
                                  preferred_element_type=F32)


def _norm_matmul(xp, xs, gains, g_idx, w, w_idx):
    mp, d = xp.shape
    n = w.shape[-1]
    tm = _pick_tile(mp, PROJ_ROW_TILE, SUBLANES_BF16)
    tn = _pick_tile(n, COL_TILE, LANES)
    nn = n // tn
    g3 = gains.reshape(gains.shape[0], 1, d)
    with_sample = xs is not None
    x_spec = pl.BlockSpec((tm, d), lambda i, j: (i, 0), pipeline_mode=pl.Buffered(1))
    g_spec = pl.BlockSpec((None, 1, d), lambda i, j: (g_idx, 0, 0))
    w_spec = pl.BlockSpec((None, d, tn), lambda i, j: (w_idx, 0, j))
    o_spec = pl.BlockSpec((tm, tn), lambda i, j: (i, j))
    o_shape = jax.ShapeDtypeStruct((mp, n), F32)
    if not with_sample:
        return pl.pallas_call(
            functools.partial(_norm_matmul_kernel, with_sample=False),
            grid=(mp // tm, nn),
            in_specs=[x_spec, g_spec, w_spec],
            out_specs=o_spec,
            out_shape=o_shape,
            scratch_shapes=[pltpu.VMEM((tm, d), BF16)],
            compiler_params=_params("parallel", "arbitrary"),
            name="norm_matmul",
        )(xp, g3, w)
    ms = xs.shape[0]
    return pl.pallas_call(
        functools.partial(_norm_matmul_kernel, with_sample=True),
        grid=(mp // tm, nn),
        in_specs=[x_spec,
                  pl.BlockSpec((ms, d), lambda i, j: (0, 0), pipeline_mode=pl.Buffered(1)),
                  g_spec, w_spec],
        out_specs=[o_spec, pl.BlockSpec((ms, tn), _held_after_first_row_tile(nn))],
        out_shape=[o_shape, jax.ShapeDtypeStruct((ms, n), F32)],
        scratch_shapes=[pltpu.VMEM((tm, d), BF16), pltpu.VMEM((ms, d), BF16)],
        compiler_params=_params("arbitrary", "arbitrary"),
        name="norm_matmul",
    )(xp, xs, g3, w)


def _out_proj_kernel(xp_ref, xs_ref, op_ref, os_ref, mp_ref, ms_ref, wa_ref, wb_ref,
                     yp_ref, ys_ref):
    def project(x_ref, o_ref, m_ref, y_ref):
        acc = jnp.dot(o_ref[...].astype(BF16), wa_ref[...].astype(BF16),
                      preferred_element_type=F32)
        acc += jnp.dot(m_ref[...].astype(BF16), wb_ref[...].astype(BF16),
                       preferred_element_type=F32)
        y_ref[...] = x_ref[...] + acc

    project(xp_ref, op_ref, mp_ref, yp_ref)

    @pl.when(pl.program_id(0) == 0)
    def _():
        project(xs_ref, os_ref, ms_ref, ys_ref)


def _out_proj(xp, xs, o_p, o_s, m_p, m_s, w_out, layer):
    mp, d = xp.shape
    ms = xs.shape[0]
    wo, wm = o_p.shape[1], m_p.shape[1]
    assert wo % wm == 0 and wo + wm == w_out.shape[1]
    tm = _pick_tile(mp, PROJ_ROW_TILE, SUBLANES_BF16)
    tn = _pick_tile(d, COL_TILE, LANES)
    nn = d // tn
    held = _held_after_first_row_tile(nn)
    return pl.pallas_call(
        _out_proj_kernel,
        grid=(mp // tm, nn),
        in_specs=[
            pl.BlockSpec((tm, tn), lambda i, j: (i, j)),
            pl.BlockSpec((ms, tn), held),
            pl.BlockSpec((tm, wo), lambda i, j: (i, 0)),
            pl.BlockSpec((ms, wo), lambda i, j: (0, 0)),
            pl.BlockSpec((tm, wm), lambda i, j: (i, 0)),
            pl.BlockSpec((ms, wm), lambda i, j: (0, 0)),
            pl.BlockSpec((None, wo, tn), lambda i, j: (layer, 0, j)),
            pl.BlockSpec((None, wm, tn), lambda i, j: (layer, wo // wm, j)),
        ],
        out_specs=[pl.BlockSpec((tm, tn), lambda i, j: (i, j)),
                   pl.BlockSpec((ms, tn), held)],
        out_shape=[jax.ShapeDtypeStruct((mp, d), F32), jax.ShapeDtypeStruct((ms, d), F32)],
        compiler_params=_params("arbitrary", "arbitrary"),
        name="out_proj",
    )(xp, xs, o_p, o_s, m_p, m_s, w_out, w_out)


def _group_mean(xsq, gm):
    hi = xsq.astype(BF16)
    lo = (xsq - hi.astype(F32)).astype(BF16)
    return (jnp.dot(hi, gm, preferred_element_type=F32)
            + jnp.dot(lo, gm, preferred_element_type=F32))


def _rot_norm(x, cos, sin_signed, g, gm, scale):
    y = x * lax.rsqrt(_group_mean(x * x, gm) + EPS) * g
    lane = lax.broadcasted_iota(jnp.int32, y.shape, 1)
    lower_half = (lane % A_DK) < (A_DK // 2)
    partner = jnp.where(lower_half,
                        pltpu.roll(y, LANES - A_DK // 2, 1),
                        pltpu.roll(y, A_DK // 2, 1))
    return (y * cos + partner * sin_signed) * scale


def _rot_norm_kernel(p_ref, cos_ref, sin_ref, g_ref, gm_ref, o_ref, *, scale):
    for hh in range(p_ref.shape[1] // LANES):
        cols = slice(hh * LANES, (hh + 1) * LANES)
        o_ref[:, cols] = _rot_norm(p_ref[:, cols], cos_ref[...], sin_ref[...], g_ref[...],
                                   gm_ref[...], scale).astype(o_ref.dtype)


def _group_mean_matrix():
    lane = jnp.arange(LANES)
    return jnp.where((lane[:, None] // A_DK) == (lane[None, :] // A_DK), 1.0 / A_DK, 0.0).astype(BF16)


def _rot_norm_rows(proj, block0, cos, sin_signed, gain, scale, dtype):
    rows, period = proj.shape[0], cos.shape[0]
    assert rows % period == 0
    tm = _pick_tile(period, ROT_ROW_TILE, SUBLANES_BF16)
    nt = period // tm
    hps = ROT_HEADS_PER_STEP
    assert A_HEADS % hps == 0 and block0 % hps == 0
    wide = hps * LANES
    g = jnp.tile(gain, LANES // A_DK).reshape(1, LANES)
    return pl.pallas_call(
        functools.partial(_rot_norm_kernel, scale=scale),
        grid=(rows // tm, A_HEADS // hps),
        in_specs=[
            pl.BlockSpec((tm, wide), lambda i, h: (i, block0 // hps + h)),
            pl.BlockSpec((tm, LANES), lambda i, h: (i % nt, 0)),
            pl.BlockSpec((tm, LANES), lambda i, h: (i % nt, 0)),
            pl.BlockSpec((1, LANES), lambda i, h: (0, 0)),
            pl.BlockSpec((LANES, LANES), lambda i, h: (0, 0)),
        ],
        out_specs=pl.BlockSpec((tm, wide), lambda i, h: (i, h)),
        out_shape=jax.ShapeDtypeStruct((rows, A_HEADS * LANES), dtype),
        compiler_params=_params("parallel", "arbitrary"),
        name="rot_norm_rows",
    )(proj, cos, sin_signed, g, _group_mean_matrix())


def _key_value_head_major_kernel(p_ref, cos_ref, sin_ref, g_ref, gm_ref, v_ref, k_out_ref, v_out_ref):
    for hh in range(k_out_ref.shape[0]):
        cols = slice(hh * LANES, (hh + 1) * LANES)
        k_out_ref[hh] = _rot_norm(p_ref[:, cols], cos_ref[...], sin_ref[...], g_ref[...],
                                  gm_ref[...], 1.0)
        v_out_ref[hh] = v_ref[:, cols]


def _key_value_head_major(proj, k_block0, v_block0, cos, sin_signed, gain, bp, seq):
    tr = _pick_tile(seq, ROT_ROW_TILE, 8)
    nt = seq // tr
    g = jnp.tile(gain, LANES // A_DK).reshape(1, LANES)
    hps = ROT_HEADS_PER_STEP
    assert A_HEADS % hps == 0 and k_block0 % hps == 0 and v_block0 % hps == 0
    wide = hps * LANES
    hm_spec = pl.BlockSpec((None, hps, tr, LANES), lambda b, i, h: (b, h, i, 0))
    hm_shape = jax.ShapeDtypeStruct((bp, A_HEADS, seq, LANES), F32)
    return pl.pallas_call(
        _key_value_head_major_kernel,
        grid=(bp, nt, A_HEADS // hps),
        in_specs=[
            pl.BlockSpec((tr, wide), lambda b, i, h: (b * nt + i, k_block0 // hps + h)),
            pl.BlockSpec((tr, LANES), lambda b, i, h: (i, 0)),
            pl.BlockSpec((tr, LANES), lambda b, i, h: (i, 0)),
            pl.BlockSpec((1, LANES), lambda b, i, h: (0, 0)),
            pl.BlockSpec((LANES, LANES), lambda b, i, h: (0, 0)),
            pl.BlockSpec((tr, wide), lambda b, i, h: (b * nt + i, v_block0 // hps + h)),
        ],
        out_specs=[hm_spec, hm_spec],
        out_shape=[hm_shape, hm_shape],
        compiler_params=_params("parallel", "parallel", "arbitrary"),
        name="key_value_head_major",
    )(proj, cos, sin_signed, g, _group_mean_matrix(), proj)


def _diff_lambda(lamv, lam_init):
    t1 = jnp.sum(lamv[0:1] * lamv[1:2], axis=-1, keepdims=True)
    t2 = jnp.sum(lamv[2:3] * lamv[3:4], axis=-1, keepdims=True)
    return jnp.exp(t1) - jnp.exp(t2) + lam_init


def _diff_attn_kernel(lamv_ref, q_ref, k_ref, v_ref, g_ref, o_ref, kb_ref, vt_ref, *, lam_init):
    tq = q_ref.shape[0]
    hps, n_kv = vt_ref.shape[:2]
    qi = pl.program_id(2)

    def head_cols(hh):
        return slice(hh * LANES, (hh + 1) * LANES)

    @pl.when(qi == 0)
    def _():
        for hh in range(hps):
            kb_ref[hh] = k_ref[hh].astype(BF16)
            for j in range(n_kv):
                vt_ref[hh, j] = v_ref[hh, j * tq:(j + 1) * tq, :].T.astype(BF16)

    lane = lax.broadcasted_iota(jnp.int32, (tq, LANES), 1)
    chains, qs = [], []
    for hh in range(hps):
        q = q_ref[:, head_cols(hh)]
        zero = jnp.zeros_like(q)
        for c in range(2):
            chains.append(hh)
            qs.append(jnp.where((lane < A_DK) if c == 0 else (lane >= A_DK), q, zero))
    key = lax.broadcasted_iota(jnp.int32, (tq, tq), 0)
    qry = lax.broadcasted_iota(jnp.int32, (tq, tq), 1)

    def block(j, carry, diagonal):
        r = pl.multiple_of(j * tq, tq)
        scores = [lax.dot_general(kb_ref[hh, pl.ds(r, tq), :], qc, NT_DIMS,
                                  preferred_element_type=F32)
                  for hh, qc in zip(chains, qs)]
        probs, stats = [], []
        for s, (m_prev, l_prev, _) in zip(scores, carry):
            if diagonal:
                s = jnp.where(key <= qry, s, -jnp.inf)
            m_new = jnp.maximum(m_prev, jnp.max(s, axis=0, keepdims=True))
            alpha = jnp.exp2(m_prev - m_new)
            p = jnp.exp2(s - m_new)
            stats.append((m_new, alpha * l_prev + jnp.sum(p, axis=0, keepdims=True), alpha))
            probs.append(p.astype(BF16))
        return tuple(
            (m_new, l_new, alpha * acc + jnp.dot(vt_ref[hh, j], p, preferred_element_type=F32))
            for hh, p, (m_new, l_new, alpha), (_, _, acc) in zip(chains, probs, stats, carry))

    init = tuple((jnp.full((1, tq), -jnp.inf, F32), jnp.zeros((1, tq), F32),
                  jnp.zeros((A_DV, tq), F32)) for _ in chains)
    carry = lax.fori_loop(0, qi, lambda j, c: block(j, c, False), init)
    carry = block(qi, carry, True)

    lam = _diff_lambda(lamv_ref[...], lam_init)
    for hh in range(hps):
        (_, l0, acc0), (_, l1, acc1) = carry[2 * hh], carry[2 * hh + 1]
        o = acc0 / l0 - lam * (acc1 / l1)
        ms = jnp.mean(o * o, axis=0, keepdims=True)
        o = o * lax.rsqrt(ms + EPS) * g_ref[...] * (1.0 - lam_init)
        o_ref[:, head_cols(hh)] = o.T.astype(o_ref.dtype)


def _diff_attn_prompt(q_all, k_hm, v_hm, lamv, g_subln, bp, seq, lam_init):
    tq = _pick_tile(seq, ATTN_TILE, LANES)
    nq = seq // tq
    hps = ATTN_HEADS_PER_STEP
    assert A_HEADS % hps == 0
    wide = hps * LANES
    return pl.pallas_call(
        functools.partial(_diff_attn_kernel, lam_init=lam_init),
        grid=(bp, A_HEADS // hps, nq),
        in_specs=[
            pl.BlockSpec(lamv.shape, lambda b, h, i: (0, 0)),
            pl.BlockSpec((tq, wide), lambda b, h, i: (b * nq + i, h)),
            pl.BlockSpec((None, hps, seq, LANES), lambda b, h, i: (b, h, 0, 0)),
            pl.BlockSpec((None, hps, seq, LANES), lambda b, h, i: (b, h, 0, 0)),
            pl.BlockSpec((A_DV, 1), lambda b, h, i: (0, 0)),
        ],
        out_specs=pl.BlockSpec((tq, wide), lambda b, h, i: (b * nq + i, h)),
        out_shape=jax.ShapeDtypeStruct((bp * seq, A_HEADS * A_DV), BF16),
        scratch_shapes=[pltpu.VMEM((hps, seq, LANES), BF16),
                        pltpu.VMEM((hps, nq, A_DV, tq), BF16)],
        compiler_params=_params("parallel", "parallel", "arbitrary"),
        name="diff_attn_prompt",
    )(lamv, q_all, k_hm, v_hm, g_subln.reshape(A_DV, 1))


def _diff_attn_decode_kernel(pt_ref, lamv_ref, q_ref, ks_ref, vs_ref, g_ref, *rest,
                             pages, lam_init, n_steps):
    del pt_ref
    k_refs = rest[:pages]
    v_refs = rest[pages:2 * pages]
    o_ref = rest[2 * pages]
    qh_ref, m_ref, l_ref, acc_ref, kb_ref, vb_ref = rest[2 * pages + 1:]
    ls = q_ref.shape[0]
    rph = 2 * ls
    rows = A_HEADS * rph
    step = pl.program_id(1)

    def head_cols(h):
        return slice(h * A_DV, (h + 1) * A_DV)

    def per_head_rows(x):
        return jnp.concatenate([x[:, head_cols(h)] for h in range(A_HEADS) for _ in range(2)],
                               axis=0)

    @pl.when(step == 0)
    def _():
        q = q_ref[...].astype(F32)
        lane = lax.broadcasted_iota(jnp.int32, (ls, LANES), 1)
        for h in range(A_HEADS):
            qh = q[:, head_cols(h)]
            qh_ref[h * rph:h * rph + ls, :] = jnp.where(lane < A_DK, qh, 0.0)
            qh_ref[h * rph + ls:(h + 1) * rph, :] = jnp.where(lane >= A_DK, qh, 0.0)
        m_ref[...] = jnp.full(m_ref.shape, -jnp.inf, F32)
        l_ref[...] = jnp.zeros(l_ref.shape, F32)
        acc_ref[...] = jnp.zeros(acc_ref.shape, F32)

    for h in range(A_HEADS):
        for p in range(pages):
            tok = slice(p * PAGE_SIZE, (p + 1) * PAGE_SIZE)
            kb_ref[h, tok, :] = k_refs[p][h].astype(BF16)
            vb_ref[h, tok, :] = v_refs[p][h].astype(BF16)

    s = jnp.concatenate(
        [lax.dot_general(qh_ref[h * rph:(h + 1) * rph, :].astype(BF16), kb_ref[h], NT_DIMS,
                         preferred_element_type=F32) for h in range(A_HEADS)], axis=0)
    m_prev = m_ref[...]
    m_new = jnp.maximum(m_prev, jnp.max(s, axis=-1, keepdims=True))
    alpha = jnp.exp2(m_prev - m_new)
    p_exp = jnp.exp2(s - m_new)
    l_ref[...] = alpha * l_ref[...] + jnp.sum(p_exp, axis=-1, keepdims=True)
    pv = jnp.concatenate(
        [jnp.dot(p_exp[h * rph:(h + 1) * rph].astype(BF16), vb_ref[h],
                 preferred_element_type=F32) for h in range(A_HEADS)], axis=0)
    acc_ref[...] = alpha * acc_ref[...] + pv
    m_ref[...] = m_new

    @pl.when(step == n_steps - 1)
    def _():
        qf = qh_ref[...]
        ks = ks_ref[...].astype(BF16).astype(F32)
        vs = vs_ref[...].astype(BF16).astype(F32)
        row = lax.broadcasted_iota(jnp.int32, (rows, 1), 0)
        q_of_row = row % ls
        s_new = []
        for t in range(ls):
            k_t = per_head_rows(jnp.broadcast_to(ks[t:t + 1], ks.shape))
            st = jnp.sum(qf * k_t, axis=-1, keepdims=True)
            s_new.append(jnp.where(q_of_row >= t, st, -jnp.inf))
        m_old = m_ref[...]
        m_fin = functools.reduce(jnp.maximum, s_new, m_old)
        a_fin = jnp.exp2(m_old - m_fin)
        l_fin = a_fin * l_ref[...]
        acc = a_fin * acc_ref[...]
        for t in range(ls):
            pt = jnp.exp2(s_new[t] - m_fin)
            l_fin = l_fin + pt
            v_t = per_head_rows(jnp.broadcast_to(vs[t:t + 1], vs.shape))
            acc = acc + pt.astype(BF16).astype(F32) * v_t

        lam = _diff_lambda(lamv_ref[...], lam_init)
        second = (row % rph) >= ls
        acc = acc * (jnp.where(second, -lam, 1.0) / l_fin)
        g = g_ref[...]
        for h in range(A_HEADS):
            o = acc[h * rph:h * rph + ls] + acc[h * rph + ls:(h + 1) * rph]
            o_ref[:, head_cols(h)] = _rmsnorm_lanes(o, g) * (1.0 - lam_init)


def _diff_attn_decode(q_s, k_s, v_s, cache_k, cache_v, page_table, lamv, g_subln, layer, lam_init):
    bs, ls, width = q_s.shape
    n_pages = page_table.shape[1]
    pages = _pick_tile(n_pages, DEC_PAGES_PER_STEP, 1)
    n_steps = n_pages // pages
    rows = ls * 2 * A_HEADS
    assert (2 * ls) % 8 == 0 and cache_k.shape[2:] == (PAGE_SIZE, A_HEADS, 2 * A_DK)

    cache_k = jnp.transpose(cache_k, (0, 1, 3, 2, 4))
    cache_v = jnp.transpose(cache_v, (0, 1, 3, 2, 4))

    def page_spec(p):
        return pl.BlockSpec((None, None, A_HEADS, PAGE_SIZE, LANES),
                            lambda b, s, pt: (layer, pt[b * n_pages + s * pages + p], 0, 0, 0))

    def per_batch():
        return pl.BlockSpec((None, ls, width), lambda b, s, pt: (b, 0, 0))

    grid_spec = pltpu.PrefetchScalarGridSpec(
        num_scalar_prefetch=1,
        grid=(bs, n_steps),
        in_specs=[pl.BlockSpec(lamv.shape, lambda b, s, pt: (0, 0)),
                  per_batch(), per_batch(), per_batch(),
                  pl.BlockSpec((1, LANES), lambda b, s, pt: (0, 0))]
                 + [page_spec(p) for p in range(pages)]
                 + [page_spec(p) for p in range(pages)],
        out_specs=per_batch(),
        scratch_shapes=[pltpu.VMEM((rows, LANES), F32),
                        pltpu.VMEM((rows, 1), F32),
                        pltpu.VMEM((rows, 1), F32),
                        pltpu.VMEM((rows, A_DV), F32),
                        pltpu.VMEM((A_HEADS, pages * PAGE_SIZE, LANES), BF16),
                        pltpu.VMEM((A_HEADS, pages * PAGE_SIZE, A_DV), BF16)],
    )
    return pl.pallas_call(
        functools.partial(_diff_attn_decode_kernel, pages=pages, lam_init=lam_init, n_steps=n_steps),
        grid_spec=grid_spec,
        out_shape=jax.ShapeDtypeStruct((bs, ls, width), F32),
        compiler_params=_params("parallel", "arbitrary"),
        name="diff_attn_decode",
    )(page_table.reshape(-1), lamv, q_s, k_s, v_s, g_subln.reshape(1, LANES),
      *([cache_k] * pages), *([cache_v] * pages))


def _mem_attn_kernel(q_ref, k_ref, v_ref, g_ref, o_ref):
    g = g_ref[...]
    cols = [slice(h * MEM_HD, (h + 1) * MEM_HD) for h in range(MEM_HEADS)]
    scores = [lax.dot_general(_rmsnorm_lanes(q_ref[:, c], g).astype(BF16),
                              k_ref[:, c].astype(BF16), NT_DIMS, preferred_element_type=F32)
              for c in cols]
    probs = []
    for s in scores:
        s = s * (MEM_HD ** -0.5)
        e = jnp.exp(s - jnp.max(s, axis=-1, keepdims=True))
        probs.append((e / jnp.sum(e, axis=-1, keepdims=True)).astype(BF16))
    for c, p in zip(cols, probs):
        o_ref[:, c] = jnp.dot(p, v_ref[:, c].astype(BF16),
                              preferred_element_type=F32).astype(o_ref.dtype)


def _mem_attn_prompt(proj, q_block0, k_norm, kv, gq, bp, seq):
    mem = k_norm.shape[0] // bp
    tq = _pick_tile(seq, 512, LANES)
    nq = seq // tq
    wide = MEM_HEADS * MEM_HD
    assert q_block0 % MEM_HEADS == 0
    return pl.pallas_call(
        _mem_attn_kernel,
        grid=(bp, nq),
        in_specs=[
            pl.BlockSpec((tq, wide), lambda b, i: (b * nq + i, q_block0 // MEM_HEADS)),
            pl.BlockSpec((mem, wide), lambda b, i: (b, 0)),
            pl.BlockSpec((mem, wide), lambda b, i: (b, 1)),
            pl.BlockSpec((1, LANES), lambda b, i: (0, 0)),
        ],
        out_specs=pl.BlockSpec((tq, wide), lambda b, i: (b * nq + i, 0)),
        out_shape=jax.ShapeDtypeStruct((bp * seq, wide), BF16),
        compiler_params=_params("parallel", "arbitrary"),
        name="mem_attn_prompt",
    )(proj, k_norm, kv, gq.reshape(1, LANES))


def _mem_attn_interleaved_kernel(q_ref, k_ref, v_ref, g_ref, o_ref):
    g = g_ref[...]
    k = k_ref[...].astype(BF16)
    v = v_ref[...].astype(BF16)
    ls, rows = q_ref.shape[0], k.shape[0]
    head_of_row = lax.broadcasted_iota(jnp.int32, (ls, rows), 1) % MEM_HEADS
    cols = [slice(h * MEM_HD, (h + 1) * MEM_HD) for h in range(MEM_HEADS)]
    scores = [lax.dot_general(_rmsnorm_lanes(q_ref[:, c], g).astype(BF16), k, NT_DIMS,
                              preferred_element_type=F32) for c in cols]
    probs = []
    for h, s in enumerate(scores):
        s = jnp.where(head_of_row == h, s * (MEM_HD ** -0.5), -jnp.inf)
        e = jnp.exp(s - jnp.max(s, axis=-1, keepdims=True))
        probs.append((e / jnp.sum(e, axis=-1, keepdims=True)).astype(BF16))
    for c, p in zip(cols, probs):
        o_ref[:, c] = jnp.dot(p, v, preferred_element_type=F32).astype(o_ref.dtype)


def _mem_attn_sample(proj_s, q_block0, cache_k, cache_v, gq, layer):
    bs, ls, _ = proj_s.shape
    mem = cache_k.shape[2]
    wide = MEM_HEADS * MEM_HD
    assert q_block0 % MEM_HEADS == 0
    bps = _pick_tile(bs, MEM_SAMPLE_BATCHES_PER_STEP, 1)

    def several_batches(q_ref, k_ref, v_ref, g_ref, o_ref):
        for bi in range(bps):
            _mem_attn_interleaved_kernel(q_ref.at[bi], k_ref.at[bi], v_ref.at[bi], g_ref,
                                         o_ref.at[bi])

    return pl.pallas_call(
        several_batches,
        grid=(bs // bps,),
        in_specs=[
            pl.BlockSpec((bps, ls, wide), lambda b: (b, 0, q_block0 // MEM_HEADS)),
            pl.BlockSpec((None, bps, mem, MEM_HD), lambda b: (layer, b, 0, 0)),
            pl.BlockSpec((None, bps, mem, MEM_HD), lambda b: (layer, b, 0, 0)),
            pl.BlockSpec((1, LANES), lambda b: (0, 0)),
        ],
        out_specs=pl.BlockSpec((bps, ls, wide), lambda b: (b, 0, 0)),
        out_shape=jax.ShapeDtypeStruct((bs, ls, wide), F32),
        compiler_params=_params("arbitrary"),
        name="mem_attn_sample",
    )(proj_s, cache_k, cache_v, gq.reshape(1, LANES))


def _mem_k_norm_kernel(k_ref, g_ref, o_ref):
    o_ref[...] = _rmsnorm_lanes(k_ref[...], g_ref[...])


def _mem_k_norm(kv, gk):
    rows = kv.shape[0]
    tm = _pick_tile(rows, 256, 8)
    return pl.pallas_call(
        _mem_k_norm_kernel,
        grid=(rows // tm, MEM_HEADS),
        in_specs=[pl.BlockSpec((tm, LANES), lambda i, h: (i, h)),
                  pl.BlockSpec((1, LANES), lambda i, h: (0, 0))],
        out_specs=pl.BlockSpec((tm, LANES), lambda i, h: (i, h)),
        out_shape=jax.ShapeDtypeStruct((rows, MEM_HEADS * MEM_HD), F32),
        compiler_params=_params("parallel", "arbitrary"),
        name="mem_k_norm",
    )(kv, gk.reshape(1, LANES))


def _split3(x):
    p1 = x.astype(BF16)
    r1 = x - p1.astype(F32)
    p2 = r1.astype(BF16)
    p3 = (r1 - p2.astype(F32)).astype(BF16)
    return p1, p2, p3


def _hgrn_kernel(*refs, chunk, sub, valid, layer, has_state):
    if has_state:
        (q_ref, f_ref, v_ref, gate_ref, lb_ref, g_ref, tri_ref, s0_ref,
         o_ref, s_out_ref, st_ref, b_scr, kk_scr) = refs
    else:
        (q_ref, f_ref, v_ref, gate_ref, lb_ref, g_ref, tri_ref,
         o_ref, s_out_ref, st_ref, b_scr, kk_scr) = refs
    tl = q_ref.shape[0]
    heads = st_ref.shape[0]
    t = pl.program_id(2)

    @pl.when(t == 0)
    def _():
        for hh in range(heads):
            if has_state:
                st_ref[hh] = s0_ref[hh].T
            else:
                st_ref[hh] = jnp.zeros(st_ref.shape[1:], F32)

    lb = lb_ref[...]
    e = jnp.exp(lb - jnp.max(lb, axis=0, keepdims=True))
    sm = e / jnp.sum(e, axis=0, keepdims=True)
    lower = jnp.zeros((1, sm.shape[1]), F32)
    for r in range(1, layer + 1):
        lower = lower + sm[r:r + 1]

    g = g_ref[...]
    n_chunks = tl // chunk
    n_sub = chunk // sub
    chunk_row = lax.broadcasted_iota(jnp.int32, (chunk, 1), 0)

    q_all = q_ref[...]
    v_all = v_ref[...]
    f_all = lower + (1.0 - lower) * jax.nn.sigmoid(f_ref[...])
    kk_all = 1.0 - f_all
    tri = tri_ref[...]
    p1, p2, p3 = _split3(jnp.log(f_all))
    b_all = (jnp.dot(tri, p1, preferred_element_type=F32)
             + jnp.dot(tri, p2, preferred_element_type=F32)
             + jnp.dot(tri, p3, preferred_element_type=F32))
    b_all = b_all * LOG2_E
    v16_all = v_all.astype(BF16)

    units = [(hh, ci) for hh in range(heads) for ci in range(n_chunks)]

    def unit_block(x, unit):
        hh, ci = unit
        return x[ci * chunk:(ci + 1) * chunk, hh * LANES:(hh + 1) * LANES]

    incs, atts, b_lasts = [], [], []
    for u in units:
        q, b, kk = unit_block(q_all, u), unit_block(b_all, u), unit_block(kk_all, u)
        b_last = b[valid - 1:valid]
        k_dec = jnp.where(chunk_row < valid, kk * jnp.exp2(b_last - b), 0.0)
        incs.append(lax.dot_general(unit_block(v16_all, u), k_dec.astype(BF16), TN_DIMS,
                                    preferred_element_type=F32))
        b_lasts.append(b_last)
        for i in range(1, n_sub):
            rs = slice(i * sub, (i + 1) * sub)
            b_ref = b[i * sub - 1:i * sub]
            q_dec = (q[rs] * jnp.exp2(b[rs] - b_ref)).astype(BF16)
            k_dec = (kk[:i * sub] * jnp.exp2(b_ref - b[:i * sub])).astype(BF16)
            atts.append(lax.dot_general(q_dec, k_dec, NT_DIMS, preferred_element_type=F32))

    b_scr[...] = b_all
    kk_scr[...] = kk_all
    causal = [jnp.where(lax.broadcasted_iota(jnp.int32, (sub, LANES), 0) >= s, 0.0, -jnp.inf)
              for s in range(sub)]
    diag = []
    for u in units:
        hh, ci = u
        cols = slice(hh * LANES, (hh + 1) * LANES)
        q, b = unit_block(q_all, u), unit_block(b_all, u)
        for i in range(n_sub):
            rs = slice(i * sub, (i + 1) * sub)
            qb, bb = q[rs], b[rs]
            terms = []
            for s in range(sub):
                row = ci * chunk + i * sub + s
                d = (bb - b_scr[row:row + 1, cols]) + causal[s]
                w = jnp.sum(qb * kk_scr[row:row + 1, cols] * jnp.exp2(d), axis=-1, keepdims=True)
                terms.append(w * v_ref[row:row + 1, cols])
            while len(terms) > 1:
                terms = [a + c for a, c in zip(terms[0::2], terms[1::2])]
            diag.append(terms[0])

    intra = []
    for n, u in enumerate(units):
        v16 = unit_block(v16_all, u)
        for i in range(n_sub):
            o_i = diag[n * n_sub + i]
            if i > 0:
                att = atts[n * (n_sub - 1) + i - 1]
                o_i = o_i + jnp.dot(att.astype(BF16), v16[:i * sub], preferred_element_type=F32)
            intra.append(o_i)

    states = [st_ref[hh] for hh in range(heads)]
    for n, u in enumerate(units):
        hh, ci = u
        q, b = unit_block(q_all, u), unit_block(b_all, u)
        o_inter = lax.dot_general((q * jnp.exp2(b)).astype(BF16), states[hh].astype(BF16),
                                  NT_DIMS, preferred_element_type=F32)
        states[hh] = states[hh] * jnp.exp2(b_lasts[n]) + incs[n]
        parts = intra[n * n_sub:(n + 1) * n_sub]
        o = o_inter + (parts[0] if n_sub == 1 else jnp.concatenate(parts, axis=0))
        rows, cols = slice(ci * chunk, (ci + 1) * chunk), slice(hh * LANES, (hh + 1) * LANES)
        gate = gate_ref[rows, cols]
        o_ref[rows, cols] = (_rmsnorm_lanes(o, g)
                             * (gate * jax.nn.sigmoid(gate))).astype(o_ref.dtype)
    for hh in range(heads):
        st_ref[hh] = states[hh]

    @pl.when(t == pl.num_programs(2) - 1)
    def _():
        for hh in range(heads):
            s_out_ref[hh] = states[hh].T


def _tri(rows, chunk):
    r = jnp.arange(rows)
    same = (r[:, None] // chunk) == (r[None, :] // chunk)
    return (same & (r[:, None] >= r[None, :])).astype(BF16)


def _hgrn_prompt(proj, lb_logits, g_out, bp, seq, layer):
    chunk = math.gcd(seq, B_CHUNK)
    sub = math.gcd(chunk, B_SUB)
    tl = _pick_tile(seq, 256, chunk)
    nt = seq // tl
    h_ = B_HEADS
    hps = HGRN_PROMPT_HEADS_PER_STEP
    assert h_ % hps == 0
    wide = hps * LANES

    def col(block0):
        return pl.BlockSpec((tl, wide), lambda b, h, t: (b * nt + t, block0 // hps + h))

    return pl.pallas_call(
        functools.partial(_hgrn_kernel, chunk=chunk, sub=sub, valid=chunk, layer=layer,
                          has_state=False),
        grid=(bp, h_ // hps, nt),
        in_specs=[col(0), col(h_), col(2 * h_), col(3 * h_),
                  pl.BlockSpec((lb_logits.shape[0], wide), lambda b, h, t: (0, h)),
                  pl.BlockSpec((1, LANES), lambda b, h, t: (0, 0)),
                  pl.BlockSpec((tl, tl), lambda b, h, t: (0, 0))],
        out_specs=[pl.BlockSpec((tl, wide), lambda b, h, t: (b * nt + t, h)),
                   pl.BlockSpec((None, hps, B_DK, B_DV), lambda b, h, t: (b, h, 0, 0))],
        out_shape=[jax.ShapeDtypeStruct((bp * seq, h_ * B_DV), BF16),
                   jax.ShapeDtypeStruct((bp, h_, B_DK, B_DV), F32)],
        scratch_shapes=[pltpu.VMEM((hps, B_DV, B_DK), F32),
                        pltpu.VMEM((tl, wide), F32), pltpu.VMEM((tl, wide), F32)],
        compiler_params=_params("parallel", "parallel", "arbitrary"),
        name="hgrn_prompt",
    )(proj, proj, proj, proj, lb_logits, g_out.reshape(1, LANES), _tri(tl, chunk))


def _hgrn_sample(proj_s, state, lb_logits, g_out, layer, state_layer):
    bs, ls, n = proj_s.shape
    chunk = 8
    assert ls <= chunk
    padded = jnp.pad(proj_s, ((0, 0), (0, chunk - ls), (0, 0)))
    h_ = B_HEADS
    hps = HGRN_SAMPLE_HEADS_PER_STEP
    assert h_ % hps == 0
    wide = hps * LANES

    def col(block0):
        return pl.BlockSpec((None, chunk, wide), lambda b, h, t: (b, 0, block0 // hps + h))

    o, s_new = pl.pallas_call(
        functools.partial(_hgrn_kernel, chunk=chunk, sub=chunk, valid=ls, layer=layer,
                          has_state=True),
        grid=(bs, h_ // hps, 1),
        in_specs=[col(0), col(h_), col(2 * h_), col(3 * h_),
                  pl.BlockSpec((lb_logits.shape[0], wide), lambda b, h, t: (0, h)),
                  pl.BlockSpec((1, LANES), lambda b, h, t: (0, 0)),
                  pl.BlockSpec((chunk, chunk), lambda b, h, t: (0, 0)),
                  pl.BlockSpec((None, None, hps, B_DK, B_DV),
                               lambda b, h, t: (state_layer, b, h, 0, 0))],
        out_specs=[pl.BlockSpec((None, chunk, wide), lambda b, h, t: (b, 0, h)),
                   pl.BlockSpec((None, hps, B_DK, B_DV), lambda b, h, t: (b, h, 0, 0))],
        out_shape=[jax.ShapeDtypeStruct((bs, chunk, h_ * B_DV), F32),
                   jax.ShapeDtypeStruct((bs, h_, B_DK, B_DV), F32)],
        scratch_shapes=[pltpu.VMEM((hps, B_DV, B_DK), F32),
                        pltpu.VMEM((chunk, wide), F32), pltpu.VMEM((chunk, wide), F32)],
        compiler_params=_params("parallel", "parallel", "arbitrary"),
        name="hgrn_sample",
    )(padded, padded, padded, padded, lb_logits, g_out.reshape(1, LANES), _tri(chunk, chunk), state)
    return o[:, :ls], s_new


def _rope_tables(pos):
    half = A_DK // 2
    inv_freq = ROPE_THETA ** (-jnp.arange(half, dtype=F32) / half)
    ang = pos.astype(F32)[:, None] * inv_freq[None, :]
    cos, sin = jnp.cos(ang), jnp.sin(ang)
    reps = LANES // A_DK
    return (jnp.tile(cos, (1, 2 * reps)), jnp.tile(jnp.concatenate([-sin, sin], axis=1), (1, reps)))


def kernel(x_prompt, x_sample, cache_attn_k, cache_attn_v, state_hgrn, cache_mem_k, cache_mem_v,
           page_table, mem_prompt, norm_ffn, w_ffn_gate, w_ffn_up, w_ffn_down, norm_mix, norm_mem,
           w_mem_kv, gq_mem, gk_mem, w_out, w_in_attn, gq_attn, gk_attn, lam_q1, lam_k1, lam_q2,
           lam_k2, g_subln, w_in_hgrn, lb_logits, g_hgrn_out):
    bp, seq, d = x_prompt.shape
    bs, ls, _ = x_sample.shape
    depth = norm_mix.shape[0]
    mem = mem_prompt.shape[1]
    mp = bp * seq
    ms = bs * ls
    past_len = page_table.shape[1] * PAGE_SIZE
    mem_w = MEM_HEADS * MEM_HD
    qk_w = A_HEADS * 2 * A_DK
    v_w = A_HEADS * A_DV

    xp = x_prompt.reshape(mp, d)
    xs = x_sample.reshape(ms, d)
    cos_p, sin_p = _rope_tables(jnp.arange(seq))
    cos_s, sin_s = _rope_tables(jnp.tile(past_len + jnp.arange(ls), bs))
    mem_rows = mem_prompt.reshape(bp * mem, d)
    cmk = cache_mem_k.reshape(depth, bs, mem * MEM_HEADS, MEM_HD)
    cmv = cache_mem_v.reshape(depth, bs, mem * MEM_HEADS, MEM_HD)

    k_rows_p, v_rows_p, k_rows_s, v_rows_s = [], [], [], []
    st_p, st_s, mem_k_new, mem_v_new = [], [], [], []
    for i in range(depth):
        xp, xs = _ffn_half(xp, xs, norm_ffn, w_ffn_gate, w_ffn_up, w_ffn_down, i, 0)

        kv = _norm_matmul(mem_rows, None, norm_mem, i, w_mem_kv, i)
        k_norm = _mem_k_norm(kv, gk_mem[i])
        mem_k_new.append(k_norm.reshape(bp, mem, MEM_HEADS, MEM_HD))
        mem_v_new.append(kv[:, mem_w:].reshape(bp, mem, MEM_HEADS, MEM_HD))

        if i % 2 == 0:
            a = i // 2
            lam_init = 0.8 - 0.6 * math.exp(-0.3 * i)
            proj, proj_s2 = _norm_matmul(xp, xs, norm_mix, i, w_in_attn, a)
            q_scale = A_DK ** -0.5 * LOG2_E
            q_p = _rot_norm_rows(proj, 0, cos_p, sin_p, gq_attn[a], q_scale, BF16)
            q_s = _rot_norm_rows(proj_s2, 0, cos_s, sin_s, gq_attn[a], q_scale, BF16)
            k_hm, v_hm = _key_value_head_major(proj, A_HEADS, 2 * A_HEADS, cos_p, sin_p,
                                               gk_attn[a], bp, seq)
            k_s = _rot_norm_rows(proj_s2, A_HEADS, cos_s, sin_s, gk_attn[a], 1.0, F32)
            lamv = jnp.stack([lam_q1[a], lam_k1[a], lam_q2[a], lam_k2[a]])
            o_p = _diff_attn_prompt(q_p, k_hm, v_hm, lamv, g_subln[a], bp, seq, lam_init)
            k_rows_p.append(jnp.transpose(k_hm, (0, 2, 1, 3)))
            v_rows_p.append(jnp.transpose(v_hm, (0, 2, 1, 3)))
            proj_s = proj_s2.reshape(bs, ls, proj_s2.shape[1])
            q_s = q_s.reshape(bs, ls, qk_w)
            k_s = k_s.reshape(bs, ls, qk_w)
            v_s = proj_s[..., 2 * qk_w:2 * qk_w + v_w]
            o_s = _diff_attn_decode(q_s, k_s, v_s, cache_attn_k, cache_attn_v, page_table, lamv,
                                    g_subln[a], a, lam_init)
            k_rows_s.append(k_s.reshape(bs, ls, A_HEADS, 2 * A_DK))
            v_rows_s.append(v_s.reshape(bs, ls, A_HEADS, A_DV))
            mq_block0 = (2 * qk_w + v_w) // LANES
        else:
            j = i // 2
            proj, proj_s2 = _norm_matmul(xp, xs, norm_mix, i, w_in_hgrn, j)
            proj_s = proj_s2.reshape(bs, ls, proj_s2.shape[1])
            o_p, s_p = _hgrn_prompt(proj, lb_logits, g_hgrn_out[j], bp, seq, i)
            o_s, s_s = _hgrn_sample(proj_s, state_hgrn, lb_logits, g_hgrn_out[j], i, j)
            st_p.append(s_p)
            st_s.append(s_s)
            mq_block0 = (2 * B_HEADS * B_DK + 2 * B_HEADS * B_DV) // LANES

        m_p = _mem_attn_prompt(proj, mq_block0, k_norm, kv, gq_mem[i], bp, seq)
        m_s = _mem_attn_sample(proj_s, mq_block0, cmk, cmv, gq_mem[i], i)
        xp, xs = _out_proj(xp, xs, o_p, o_s.reshape(ms, -1), m_p, m_s.reshape(ms, -1), w_out, i)

        xp, xs = _ffn_half(xp, xs, norm_ffn, w_ffn_gate, w_ffn_up, w_ffn_down, i, 1)

    return (xp.reshape(bp, seq, d), xs.reshape(bs, ls, d),
            jnp.stack(k_rows_p), jnp.stack(v_rows_p), jnp.stack(k_rows_s), jnp.stack(v_rows_s),
            jnp.stack(st_p), jnp.stack(st_s), jnp.stack(mem_k_new), jnp.stack(mem_v_new))
```

```python
import functools
import math

import jax
import jax.numpy as jnp
from jax import lax
from jax.experimental import pallas as pl
from jax.experimental.pallas import tpu as pltpu

F32 = jnp.float32
BF16 = jnp.bfloat16

EPS = 1e-6
LOG2_E = math.log2(math.e)
ROPE_THETA = 10000.0
A_HEADS = 12
A_DK = 64
A_DV = 128
B_HEADS = 12
B_DK = 128
B_DV = 128
B_CHUNK = 64
B_SUB = 8
MEM_HEADS = 4
MEM_HD = 128
PAGE_SIZE = 128
LANES = 128
SUBLANES_BF16 = 16
V7X_VMEM_LIMIT_BYTES = 56 * 1024 * 1024
FFN_ROW_TILE = 1024
PROJ_ROW_TILE = 2048
ROT_ROW_TILE = 1024
ROT_HEADS_PER_STEP = 4
MEM_SAMPLE_BATCHES_PER_STEP = 4
COL_TILE = 512
FFN_COL_TILE = 256
NORM_ROWS_PER_ITER = 128
DEC_PAGES_PER_STEP = 8
ATTN_TILE = 512
ATTN_HEADS_PER_STEP = 3
HGRN_SAMPLE_HEADS_PER_STEP = 12
HGRN_PROMPT_HEADS_PER_STEP = 4
NT_DIMS = (((1,), (1,)), ((), ()))
TN_DIMS = (((0,), (0,)), ((), ()))


def _params(*semantics):
    return pltpu.CompilerParams(dimension_semantics=semantics,
                                vmem_limit_bytes=V7X_VMEM_LIMIT_BYTES)


def _pick_tile(n, target, align):
    best = None
    for t in range(align, min(n, target) + 1, align):
        if n % t == 0:
            best = t
    assert best is not None, (n, target, align)
    return best


def _rmsnorm_lanes(x, g):
    ms = jnp.mean(x * x, axis=-1, keepdims=True)
    return x * lax.rsqrt(ms + EPS) * g


def _norm_rows_into(h_ref, x_ref, g_ref, copy_ref=None):
    rows = x_ref.shape[0]
    chunk = _pick_tile(rows, NORM_ROWS_PER_ITER, SUBLANES_BF16 if rows % SUBLANES_BF16 == 0 else 8)
    g = g_ref[...]

    def body(i, carry):
        r = pl.multiple_of(i * chunk, chunk)
        x = x_ref[pl.ds(r, chunk), :]
        h_ref[pl.ds(r, chunk), :] = _rmsnorm_lanes(x, g).astype(h_ref.dtype)
        if copy_ref is not None:
            copy_ref[pl.ds(r, chunk), :] = x
        return carry

    lax.fori_loop(0, rows // chunk, body, 0)


def _held_after_first_row_tile(n_col_blocks):
    return lambda i, j: (0, jnp.where(i == 0, j, n_col_blocks - 1))


def _ffn_kernel(xp_ref, xs_ref, g_ref, wg_ref, wu_ref, wd_ref, op_ref, os_ref, hp_ref, hs_ref,
                *, n_out_chunks):
    i, j = pl.program_id(0), pl.program_id(1)

    @pl.when(j == 0)
    def _():
        _norm_rows_into(hp_ref, xp_ref, g_ref, copy_ref=op_ref)

    @pl.when(jnp.logical_and(i == 0, j == 0))
    def _():
        _norm_rows_into(hs_ref, xs_ref, g_ref, copy_ref=os_ref)

    def accumulate(h_ref, o_ref):
        h = h_ref[...]
        gate = jnp.dot(h, wg_ref[...].astype(BF16), preferred_element_type=F32)
        up = jnp.dot(h, wu_ref[...].astype(BF16), preferred_element_type=F32)
        act = (0.5 * (gate * jax.nn.sigmoid(gate)) * up).astype(BF16)
        width = o_ref.shape[1] // n_out_chunks
        for c in range(n_out_chunks):
            cols = slice(c * width, (c + 1) * width)
            o_ref[:, cols] += jnp.dot(act, wd_ref[:, cols].astype(BF16),
                                      preferred_element_type=F32)

    accumulate(hp_ref, op_ref)

    @pl.when(i == 0)
    def _():
        accumulate(hs_ref, os_ref)


def _ffn_half(xp, xs, norm_ffn, w_gate, w_up, w_down, layer, half):
    mp, d = xp.shape
    ms = xs.shape[0]
    f = w_gate.shape[-1]
    tm = _pick_tile(mp, FFN_ROW_TILE, SUBLANES_BF16)
    tf = _pick_tile(f, FFN_COL_TILE, LANES)
    n_out_chunks = max(1, d // 512)
    g4 = norm_ffn.reshape(norm_ffn.shape[0], 2, 1, d)
    return pl.pallas_call(
        functools.partial(_ffn_kernel, n_out_chunks=n_out_chunks),
        grid=(mp // tm, f // tf),
        in_specs=[
            pl.BlockSpec((tm, d), lambda i, j: (i, 0)),
            pl.BlockSpec((ms, d), lambda i, j: (0, 0), pipeline_mode=pl.Buffered(1)),
            pl.BlockSpec((None, None, 1, d), lambda i, j: (layer, half, 0, 0)),
            pl.BlockSpec((None, None, d, tf), lambda i, j: (layer, half, 0, j)),
            pl.BlockSpec((None, None, d, tf), lambda i, j: (layer, half, 0, j)),
            pl.BlockSpec((None, None, tf, d), lambda i, j: (layer, half, j, 0)),
        ],
        out_specs=[pl.BlockSpec((tm, d), lambda i, j: (i, 0)),
                   pl.BlockSpec((ms, d), lambda i, j: (0, 0))],
        out_shape=[jax.ShapeDtypeStruct((mp, d), F32), jax.ShapeDtypeStruct((ms, d), F32)],
        scratch_shapes=[pltpu.VMEM((tm, d), BF16), pltpu.VMEM((ms, d), BF16)],
        compiler_params=_params("arbitrary", "arbitrary"),
        name="ffn_half",
    )(xp, xs, g4, w_gate, w_up, w_down)


def _norm_matmul_kernel(*refs, with_sample):
    if with_sample:
        xp_ref, xs_ref, g_ref, w_ref, op_ref, os_ref, hp_ref, hs_ref = refs
    else:
        xp_ref, g_ref, w_ref, op_ref, hp_ref = refs
    i, j = pl.program_id(0), pl.program_id(1)

    @pl.when(j == 0)
    def _():
        _norm_rows_into(hp_ref, xp_ref, g_ref)

    op_ref[...] = jnp.dot(hp_ref[...], w_ref[...].astype(BF16), preferred_element_type=F32)

    if with_sample:
        @pl.when(jnp.logical_and(i == 0, j == 0))
        def _():
            _norm_rows_into(hs_ref, xs_ref, g_ref)

        @pl.when(i == 0)
        def _():
            os_ref[...] = jnp.dot(hs_ref[...], w_ref[...].astype(BF16),
                                  preferred_element_type=F32)


def _norm_matmul(xp, xs, gains, g_idx, w, w_idx):
    mp, d = xp.shape
    n = w.shape[-1]
    tm = _pick_tile(mp, PROJ_ROW_TILE, SUBLANES_BF16)
    tn = _pick_tile(n, COL_TILE, LANES)
    nn = n // tn
    g3 = gains.reshape(gains.shape[0], 1, d)
    with_sample = xs is not None
    x_spec = pl.BlockSpec((tm, d), lambda i, j: (i, 0), pipeline_mode=pl.Buffered(1))
    g_spec = pl.BlockSpec((None, 1, d), lambda i, j: (g_idx, 0, 0))
    w_spec = pl.BlockSpec((None, d, tn), lambda i, j: (w_idx, 0, j))
    o_spec = pl.BlockSpec((tm, tn), lambda i, j: (i, j))
    o_shape = jax.ShapeDtypeStruct((mp, n), F32)
    if not with_sample:
        return pl.pallas_call(
            functools.partial(_norm_matmul_kernel, with_sample=False),
            grid=(mp // tm, nn),
            in_specs=[x_spec, g_spec, w_spec],
            out_specs=o_spec,
            out_shape=o_shape,
            scratch_shapes=[pltpu.VMEM((tm, d), BF16)],
            compiler_params=_params("parallel", "arbitrary"),
            name="norm_matmul",
        )(xp, g3, w)
    ms = xs.shape[0]
    return pl.pallas_call(
        functools.partial(_norm_matmul_kernel, with_sample=True),
        grid=(mp // tm, nn),
        in_specs=[x_spec,
                  pl.BlockSpec((ms, d), lambda i, j: (0, 0), pipeline_mode=pl.Buffered(1)),
                  g_spec, w_spec],
        out_specs=[o_spec, pl.BlockSpec((ms, tn), _held_after_first_row_tile(nn))],
        out_shape=[o_shape, jax.ShapeDtypeStruct((ms, n), F32)],
        scratch_shapes=[pltpu.VMEM((tm, d), BF16), pltpu.VMEM((ms, d), BF16)],
        compiler_params=_params("arbitrary", "arbitrary"),
        name="norm_matmul",
    )(xp, xs, g3, w)


def _out_proj_kernel(xp_ref, xs_ref, op_ref, os_ref, mp_ref, ms_ref, wa_ref, wb_ref,
                     yp_ref, ys_ref):
    def project(x_ref, o_ref, m_ref, y_ref):
        acc = jnp.dot(o_ref[...].astype(BF16), wa_ref[...].astype(BF16),
                      preferred_element_type=F32)
        acc += jnp.dot(m_ref[...].astype(BF16), wb_ref[...].astype(BF16),
                       preferred_element_type=F32)
        y_ref[...] = x_ref[...] + acc

    project(xp_ref, op_ref, mp_ref, yp_ref)

    @pl.when(pl.program_id(0) == 0)
    def _():
        project(xs_ref, os_ref, ms_ref, ys_ref)


def _out_proj(xp, xs, o_p, o_s, m_p, m_s, w_out, layer):
    mp, d = xp.shape
    ms = xs.shape[0]
    wo, wm = o_p.shape[1], m_p.shape[1]
    assert wo % wm == 0 and wo + wm == w_out.shape[1]
    tm = _pick_tile(mp, PROJ_ROW_TILE, SUBLANES_BF16)
    tn = _pick_tile(d, COL_TILE, LANES)
    nn = d // tn
    held = _held_after_first_row_tile(nn)
    return pl.pallas_call(
        _out_proj_kernel,
        grid=(mp // tm, nn),
        in_specs=[
            pl.BlockSpec((tm, tn), lambda i, j: (i, j)),
            pl.BlockSpec((ms, tn), held),
            pl.BlockSpec((tm, wo), lambda i, j: (i, 0)),
            pl.BlockSpec((ms, wo), lambda i, j: (0, 0)),
            pl.BlockSpec((tm, wm), lambda i, j: (i, 0)),
            pl.BlockSpec((ms, wm), lambda i, j: (0, 0)),
            pl.BlockSpec((None, wo, tn), lambda i, j: (layer, 0, j)),
            pl.BlockSpec((None, wm, tn), lambda i, j: (layer, wo // wm, j)),
        ],
        out_specs=[pl.BlockSpec((tm, tn), lambda i, j: (i, j)),
                   pl.BlockSpec((ms, tn), held)],
        out_shape=[jax.ShapeDtypeStruct((mp, d), F32), jax.ShapeDtypeStruct((ms, d), F32)],
        compiler_params=_params("arbitrary", "arbitrary"),
        name="out_proj",
    )(xp, xs, o_p, o_s, m_p, m_s, w_out, w_out)


def _group_mean(xsq, gm):
    hi = xsq.astype(BF16)
    lo = (xsq - hi.astype(F32)).astype(BF16)
    return (jnp.dot(hi, gm, preferred_element_type=F32)
            + jnp.dot(lo, gm, preferred_element_type=F32))


def _rot_norm(x, cos, sin_signed, g, gm, scale):
    y = x * lax.rsqrt(_group_mean(x * x, gm) + EPS) * g
    lane = lax.broadcasted_iota(jnp.int32, y.shape, 1)
    lower_half = (lane % A_DK) < (A_DK // 2)
    partner = jnp.where(lower_half,
                        pltpu.roll(y, LANES - A_DK // 2, 1),
                        pltpu.roll(y, A_DK // 2, 1))
    return (y * cos + partner * sin_signed) * scale


def _rot_norm_kernel(p_ref, cos_ref, sin_ref, g_ref, gm_ref, o_ref, *, scale):
    for hh in range(p_ref.shape[1] // LANES):
        cols = slice(hh * LANES, (hh + 1) * LANES)
        o_ref[:, cols] = _rot_norm(p_ref[:, cols], cos_ref[...], sin_ref[...], g_ref[...],
                                   gm_ref[...], scale).astype(o_ref.dtype)


def _group_mean_matrix():
    lane = jnp.arange(LANES)
    return jnp.where((lane[:, None] // A_DK) == (lane[None, :] // A_DK), 1.0 / A_DK, 0.0).astype(BF16)


def _rot_norm_rows(proj, block0, cos, sin_signed, gain, scale, dtype):
    rows, period = proj.shape[0], cos.shape[0]
    assert rows % period == 0
    tm = _pick_tile(period, ROT_ROW_TILE, SUBLANES_BF16)
    nt = period // tm
    hps = ROT_HEADS_PER_STEP
    assert A_HEADS % hps == 0 and block0 % hps == 0
    wide = hps * LANES
    g = jnp.tile(gain, LANES // A_DK).reshape(1, LANES)
    return pl.pallas_call(
        functools.partial(_rot_norm_kernel, scale=scale),
        grid=(rows // tm, A_HEADS // hps),
        in_specs=[
            pl.BlockSpec((tm, wide), lambda i, h: (i, block0 // hps + h)),
            pl.BlockSpec((tm, LANES), lambda i, h: (i % nt, 0)),
            pl.BlockSpec((tm, LANES), lambda i, h: (i % nt, 0)),
            pl.BlockSpec((1, LANES), lambda i, h: (0, 0)),
            pl.BlockSpec((LANES, LANES), lambda i, h: (0, 0)),
        ],
        out_specs=pl.BlockSpec((tm, wide), lambda i, h: (i, h)),
        out_shape=jax.ShapeDtypeStruct((rows, A_HEADS * LANES), dtype),
        compiler_params=_params("parallel", "arbitrary"),
        name="rot_norm_rows",
    )(proj, cos, sin_signed, g, _group_mean_matrix())


def _qkv_prep_kernel(q_ref, k_ref, v_ref, cos_ref, sin_ref, gq_ref, gk_ref, gm_ref,
                     q_out_ref, k_out_ref, v_out_ref, *, q_scale):
    cos, sin, gm = cos_ref[...], sin_ref[...], gm_ref[...]
    for hh in range(k_out_ref.shape[0]):
        cols = slice(hh * LANES, (hh + 1) * LANES)
        q_out_ref[:, cols] = _rot_norm(q_ref[:, cols], cos, sin, gq_ref[...], gm,
                                       q_scale).astype(q_out_ref.dtype)
        k_out_ref[hh] = _rot_norm(k_ref[:, cols], cos, sin, gk_ref[...], gm, 1.0)
        v_out_ref[hh] = v_ref[:, cols]


def _qkv_prep(proj, q_block0, k_block0, v_block0, cos, sin_signed, gq, gk, q_scale, bp, seq):
    tr = _pick_tile(seq, ROT_ROW_TILE, SUBLANES_BF16)
    nt = seq // tr
    hps = ROT_HEADS_PER_STEP
    assert A_HEADS % hps == 0 and all(b % hps == 0 for b in (q_block0, k_block0, v_block0))
    wide = hps * LANES

    def section(block0):
        return pl.BlockSpec((tr, wide), lambda b, i, h: (b * nt + i, block0 // hps + h))

    def gain(g):
        return jnp.tile(g, LANES // A_DK).reshape(1, LANES)

    table = pl.BlockSpec((tr, LANES), lambda b, i, h: (i, 0))
    row = pl.BlockSpec((1, LANES), lambda b, i, h: (0, 0))
    hm_spec = pl.BlockSpec((None, hps, tr, LANES), lambda b, i, h: (b, h, i, 0))
    hm_shape = jax.ShapeDtypeStruct((bp, A_HEADS, seq, LANES), F32)
    return pl.pallas_call(
        functools.partial(_qkv_prep_kernel, q_scale=q_scale),
        grid=(bp, nt, A_HEADS // hps),
        in_specs=[section(q_block0), section(k_block0), section(v_block0), table, table, row, row,
                  pl.BlockSpec((LANES, LANES), lambda b, i, h: (0, 0))],
        out_specs=[pl.BlockSpec((tr, wide), lambda b, i, h: (b * nt + i, h)), hm_spec, hm_spec],
        out_shape=[jax.ShapeDtypeStruct((bp * seq, A_HEADS * LANES), BF16), hm_shape, hm_shape],
        compiler_params=_params("parallel", "parallel", "arbitrary"),
        name="qkv_prep",
    )(proj, proj, proj, cos, sin_signed, gain(gq), gain(gk), _group_mean_matrix())


def _diff_lambda(lamv, lam_init):
    t1 = jnp.sum(lamv[0:1] * lamv[1:2], axis=-1, keepdims=True)
    t2 = jnp.sum(lamv[2:3] * lamv[3:4], axis=-1, keepdims=True)
    return jnp.exp(t1) - jnp.exp(t2) + lam_init


def _diff_attn_kernel(lamv_ref, q_ref, k_ref, v_ref, g_ref, o_ref, kb_ref, vt_ref, *, lam_init):
    tq = q_ref.shape[0]
    hps, n_kv = vt_ref.shape[:2]
    qi = pl.program_id(2)

    def head_cols(hh):
        return slice(hh * LANES, (hh + 1) * LANES)

    @pl.when(qi == 0)
    def _():
        for hh in range(hps):
            kb_ref[hh] = k_ref[hh].astype(BF16)
            for j in range(n_kv):
                vt_ref[hh, j] = v_ref[hh, j * tq:(j + 1) * tq, :].T.astype(BF16)

    lane = lax.broadcasted_iota(jnp.int32, (tq, LANES), 1)
    chains, qs = [], []
    for hh in range(hps):
        q = q_ref[:, head_cols(hh)]
        zero = jnp.zeros_like(q)
        for c in range(2):
            chains.append(hh)
            qs.append(jnp.where((lane < A_DK) if c == 0 else (lane >= A_DK), q, zero))
    key = lax.broadcasted_iota(jnp.int32, (tq, tq), 0)
    qry = lax.broadcasted_iota(jnp.int32, (tq, tq), 1)

    def block(j, carry, diagonal):
        r = pl.multiple_of(j * tq, tq)
        scores = [lax.dot_general(kb_ref[hh, pl.ds(r, tq), :], qc, NT_DIMS,
                                  preferred_element_type=F32)
                  for hh, qc in zip(chains, qs)]
        probs, stats = [], []
        for s, (m_prev, l_prev, _) in zip(scores, carry):
            if diagonal:
                s = jnp.where(key <= qry, s, -jnp.inf)
            m_new = jnp.maximum(m_prev, jnp.max(s, axis=0, keepdims=True))
            alpha = jnp.exp2(m_prev - m_new)
            p = jnp.exp2(s - m_new)
            stats.append((m_new, alpha * l_prev + jnp.sum(p, axis=0, keepdims=True), alpha))
            probs.append(p.astype(BF16))
        return tuple(
            (m_new, l_new, alpha * acc + jnp.dot(vt_ref[hh, j], p, preferred_element_type=F32))
            for hh, p, (m_new, l_new, alpha), (_, _, acc) in zip(chains, probs, stats, carry))

    init = tuple((jnp.full((1, tq), -jnp.inf, F32), jnp.zeros((1, tq), F32),
                  jnp.zeros((A_DV, tq), F32)) for _ in chains)
    carry = lax.fori_loop(0, qi, lambda j, c: block(j, c, False), init)
    carry = block(qi, carry, True)

    lam = _diff_lambda(lamv_ref[...], lam_init)
    for hh in range(hps):
        (_, l0, acc0), (_, l1, acc1) = carry[2 * hh], carry[2 * hh + 1]
        o = acc0 / l0 - lam * (acc1 / l1)
        ms = jnp.mean(o * o, axis=0, keepdims=True)
        o = o * lax.rsqrt(ms + EPS) * g_ref[...] * (1.0 - lam_init)
        o_ref[:, head_cols(hh)] = o.T.astype(o_ref.dtype)


def _diff_attn_prompt(q_all, k_hm, v_hm, lamv, g_subln, bp, seq, lam_init):
    tq = _pick_tile(seq, ATTN_TILE, LANES)
    nq = seq // tq
    hps = ATTN_HEADS_PER_STEP
    assert A_HEADS % hps == 0
    wide = hps * LANES
    return pl.pallas_call(
        functools.partial(_diff_attn_kernel, lam_init=lam_init),
        grid=(bp, A_HEADS // hps, nq),
        in_specs=[
            pl.BlockSpec(lamv.shape, lambda b, h, i: (0, 0)),
            pl.BlockSpec((tq, wide), lambda b, h, i: (b * nq + i, h)),
            pl.BlockSpec((None, hps, seq, LANES), lambda b, h, i: (b, h, 0, 0)),
            pl.BlockSpec((None, hps, seq, LANES), lambda b, h, i: (b, h, 0, 0)),
            pl.BlockSpec((A_DV, 1), lambda b, h, i: (0, 0)),
        ],
        out_specs=pl.BlockSpec((tq, wide), lambda b, h, i: (b * nq + i, h)),
        out_shape=jax.ShapeDtypeStruct((bp * seq, A_HEADS * A_DV), BF16),
        scratch_shapes=[pltpu.VMEM((hps, seq, LANES), BF16),
                        pltpu.VMEM((hps, nq, A_DV, tq), BF16)],
        compiler_params=_params("parallel", "parallel", "arbitrary"),
        name="diff_attn_prompt",
    )(lamv, q_all, k_hm, v_hm, g_subln.reshape(A_DV, 1))


def _diff_attn_decode_kernel(pt_ref, lamv_ref, q_ref, ks_ref, vs_ref, g_ref, *rest,
                             pages, lam_init, n_steps):
    del pt_ref
    k_refs = rest[:pages]
    v_refs = rest[pages:2 * pages]
    o_ref = rest[2 * pages]
    qh_ref, m_ref, l_ref, acc_ref, kb_ref, vb_ref = rest[2 * pages + 1:]
    ls = q_ref.shape[0]
    rph = 2 * ls
    rows = A_HEADS * rph
    step = pl.program_id(1)

    def head_cols(h):
        return slice(h * A_DV, (h + 1) * A_DV)

    def per_head_rows(x):
        return jnp.concatenate([x[:, head_cols(h)] for h in range(A_HEADS) for _ in range(2)],
                               axis=0)

    @pl.when(step == 0)
    def _():
        q = q_ref[...].astype(F32)
        lane = lax.broadcasted_iota(jnp.int32, (ls, LANES), 1)
        for h in range(A_HEADS):
            qh = q[:, head_cols(h)]
            qh_ref[h * rph:h * rph + ls, :] = jnp.where(lane < A_DK, qh, 0.0)
            qh_ref[h * rph + ls:(h + 1) * rph, :] = jnp.where(lane >= A_DK, qh, 0.0)
        m_ref[...] = jnp.full(m_ref.shape, -jnp.inf, F32)
        l_ref[...] = jnp.zeros(l_ref.shape, F32)
        acc_ref[...] = jnp.zeros(acc_ref.shape, F32)

    for h in range(A_HEADS):
        for p in range(pages):
            tok = slice(p * PAGE_SIZE, (p + 1) * PAGE_SIZE)
            kb_ref[h, tok, :] = k_refs[p][h].astype(BF16)
            vb_ref[h, tok, :] = v_refs[p][h].astype(BF16)

    s = jnp.concatenate(
        [lax.dot_general(qh_ref[h * rph:(h + 1) * rph, :].astype(BF16), kb_ref[h], NT_DIMS,
                         preferred_element_type=F32) for h in range(A_HEADS)], axis=0)
    m_prev = m_ref[...]
    m_new = jnp.maximum(m_prev, jnp.max(s, axis=-1, keepdims=True))
    alpha = jnp.exp2(m_prev - m_new)
    p_exp = jnp.exp2(s - m_new)
    l_ref[...] = alpha * l_ref[...] + jnp.sum(p_exp, axis=-1, keepdims=True)
    pv = jnp.concatenate(
        [jnp.dot(p_exp[h * rph:(h + 1) * rph].astype(BF16), vb_ref[h],
                 preferred_element_type=F32) for h in range(A_HEADS)], axis=0)
    acc_ref[...] = alpha * acc_ref[...] + pv
    m_ref[...] = m_new

    @pl.when(step == n_steps - 1)
    def _():
        qf = qh_ref[...]
        ks = ks_ref[...].astype(BF16).astype(F32)
        vs = vs_ref[...].astype(BF16).astype(F32)
        row = lax.broadcasted_iota(jnp.int32, (rows, 1), 0)
        q_of_row = row % ls
        s_new = []
        for t in range(ls):
            k_t = per_head_rows(jnp.broadcast_to(ks[t:t + 1], ks.shape))
            st = jnp.sum(qf * k_t, axis=-1, keepdims=True)
            s_new.append(jnp.where(q_of_row >= t, st, -jnp.inf))
        m_old = m_ref[...]
        m_fin = functools.reduce(jnp.maximum, s_new, m_old)
        a_fin = jnp.exp2(m_old - m_fin)
        l_fin = a_fin * l_ref[...]
        acc = a_fin * acc_ref[...]
        for t in range(ls):
            pt = jnp.exp2(s_new[t] - m_fin)
            l_fin = l_fin + pt
            v_t = per_head_rows(jnp.broadcast_to(vs[t:t + 1], vs.shape))
            acc = acc + pt.astype(BF16).astype(F32) * v_t

        lam = _diff_lambda(lamv_ref[...], lam_init)
        second = (row % rph) >= ls
        acc = acc * (jnp.where(second, -lam, 1.0) / l_fin)
        g = g_ref[...]
        for h in range(A_HEADS):
            o = acc[h * rph:h * rph + ls] + acc[h * rph + ls:(h + 1) * rph]
            o_ref[:, head_cols(h)] = _rmsnorm_lanes(o, g) * (1.0 - lam_init)


def _diff_attn_decode(q_s, k_s, v_s, cache_k, cache_v, page_table, lamv, g_subln, layer, lam_init):
    bs, ls, width = q_s.shape
    n_pages = page_table.shape[1]
    pages = _pick_tile(n_pages, DEC_PAGES_PER_STEP, 1)
    n_steps = n_pages // pages
    rows = ls * 2 * A_HEADS
    assert (2 * ls) % 8 == 0 and cache_k.shape[2:] == (PAGE_SIZE, A_HEADS, 2 * A_DK)

    cache_k = jnp.transpose(cache_k, (0, 1, 3, 2, 4))
    cache_v = jnp.transpose(cache_v, (0, 1, 3, 2, 4))

    def page_spec(p):
        return pl.BlockSpec((None, None, A_HEADS, PAGE_SIZE, LANES),
                            lambda b, s, pt: (layer, pt[b * n_pages + s * pages + p], 0, 0, 0))

    def per_batch():
        return pl.BlockSpec((None, ls, width), lambda b, s, pt: (b, 0, 0))

    grid_spec = pltpu.PrefetchScalarGridSpec(
        num_scalar_prefetch=1,
        grid=(bs, n_steps),
        in_specs=[pl.BlockSpec(lamv.shape, lambda b, s, pt: (0, 0)),
                  per_batch(), per_batch(), per_batch(),
                  pl.BlockSpec((1, LANES), lambda b, s, pt: (0, 0))]
                 + [page_spec(p) for p in range(pages)]
                 + [page_spec(p) for p in range(pages)],
        out_specs=per_batch(),
        scratch_shapes=[pltpu.VMEM((rows, LANES), F32),
                        pltpu.VMEM((rows, 1), F32),
                        pltpu.VMEM((rows, 1), F32),
                        pltpu.VMEM((rows, A_DV), F32),
                        pltpu.VMEM((A_HEADS, pages * PAGE_SIZE, LANES), BF16),
                        pltpu.VMEM((A_HEADS, pages * PAGE_SIZE, A_DV), BF16)],
    )
    return pl.pallas_call(
        functools.partial(_diff_attn_decode_kernel, pages=pages, lam_init=lam_init, n_steps=n_steps),
        grid_spec=grid_spec,
        out_shape=jax.ShapeDtypeStruct((bs, ls, width), F32),
        compiler_params=_params("parallel", "arbitrary"),
        name="diff_attn_decode",
    )(page_table.reshape(-1), lamv, q_s, k_s, v_s, g_subln.reshape(1, LANES),
      *([cache_k] * pages), *([cache_v] * pages))


def _mem_attn_kernel(q_ref, k_ref, v_ref, g_ref, o_ref):
    g = g_ref[...]
    cols = [slice(h * MEM_HD, (h + 1) * MEM_HD) for h in range(MEM_HEADS)]
    scores = [lax.dot_general(_rmsnorm_lanes(q_ref[:, c], g).astype(BF16),
                              k_ref[:, c].astype(BF16), NT_DIMS, preferred_element_type=F32)
              for c in cols]
    probs = []
    for s in scores:
        s = s * (MEM_HD ** -0.5)
        e = jnp.exp(s - jnp.max(s, axis=-1, keepdims=True))
        probs.append((e / jnp.sum(e, axis=-1, keepdims=True)).astype(BF16))
    for c, p in zip(cols, probs):
        o_ref[:, c] = jnp.dot(p, v_ref[:, c].astype(BF16),
                              preferred_element_type=F32).astype(o_ref.dtype)


def _mem_attn_prompt(proj, q_block0, k_norm, kv, gq, bp, seq):
    mem = k_norm.shape[0] // bp
    tq = _pick_tile(seq, 512, LANES)
    nq = seq // tq
    wide = MEM_HEADS * MEM_HD
    assert q_block0 % MEM_HEADS == 0
    return pl.pallas_call(
        _mem_attn_kernel,
        grid=(bp, nq),
        in_specs=[
            pl.BlockSpec((tq, wide), lambda b, i: (b * nq + i, q_block0 // MEM_HEADS)),
            pl.BlockSpec((mem, wide), lambda b, i: (b, 0)),
            pl.BlockSpec((mem, wide), lambda b, i: (b, 1)),
            pl.BlockSpec((1, LANES), lambda b, i: (0, 0)),
        ],
        out_specs=pl.BlockSpec((tq, wide), lambda b, i: (b * nq + i, 0)),
        out_shape=jax.ShapeDtypeStruct((bp * seq, wide), BF16),
        compiler_params=_params("parallel", "arbitrary"),
        name="mem_attn_prompt",
    )(proj, k_norm, kv, gq.reshape(1, LANES))


def _mem_attn_interleaved_kernel(q_ref, k_ref, v_ref, g_ref, o_ref):
    g = g_ref[...]
    k = k_ref[...].astype(BF16)
    v = v_ref[...].astype(BF16)
    ls, rows = q_ref.shape[0], k.shape[0]
    head_of_row = lax.broadcasted_iota(jnp.int32, (ls, rows), 1) % MEM_HEADS
    cols = [slice(h * MEM_HD, (h + 1) * MEM_HD) for h in range(MEM_HEADS)]
    scores = [lax.dot_general(_rmsnorm_lanes(q_ref[:, c], g).astype(BF16), k, NT_DIMS,
                              preferred_element_type=F32) for c in cols]
    probs = []
    for h, s in enumerate(scores):
        s = jnp.where(head_of_row == h, s * (MEM_HD ** -0.5), -jnp.inf)
        e = jnp.exp(s - jnp.max(s, axis=-1, keepdims=True))
        probs.append((e / jnp.sum(e, axis=-1, keepdims=True)).astype(BF16))
    for c, p in zip(cols, probs):
        o_ref[:, c] = jnp.dot(p, v, preferred_element_type=F32).astype(o_ref.dtype)


def _mem_attn_sample(proj_s, q_block0, cache_k, cache_v, gq, layer):
    bs, ls, _ = proj_s.shape
    mem = cache_k.shape[2]
    wide = MEM_HEADS * MEM_HD
    assert q_block0 % MEM_HEADS == 0
    bps = _pick_tile(bs, MEM_SAMPLE_BATCHES_PER_STEP, 1)

    def several_batches(q_ref, k_ref, v_ref, g_ref, o_ref):
        for bi in range(bps):
            _mem_attn_interleaved_kernel(q_ref.at[bi], k_ref.at[bi], v_ref.at[bi], g_ref,
                                         o_ref.at[bi])

    return pl.pallas_call(
        several_batches,
        grid=(bs // bps,),
        in_specs=[
            pl.BlockSpec((bps, ls, wide), lambda b: (b, 0, q_block0 // MEM_HEADS)),
            pl.BlockSpec((None, bps, mem, MEM_HD), lambda b: (layer, b, 0, 0)),
            pl.BlockSpec((None, bps, mem, MEM_HD), lambda b: (layer, b, 0, 0)),
            pl.BlockSpec((1, LANES), lambda b: (0, 0)),
        ],
        out_specs=pl.BlockSpec((bps, ls, wide), lambda b: (b, 0, 0)),
        out_shape=jax.ShapeDtypeStruct((bs, ls, wide), F32),
        compiler_params=_params("arbitrary"),
        name="mem_attn_sample",
    )(proj_s, cache_k, cache_v, gq.reshape(1, LANES))


def _mem_k_norm_kernel(k_ref, g_ref, o_ref):
    o_ref[...] = _rmsnorm_lanes(k_ref[...], g_ref[...])


def _mem_k_norm(kv, gk):
    rows = kv.shape[0]
    tm = _pick_tile(rows, 256, 8)
    return pl.pallas_call(
        _mem_k_norm_kernel,
        grid=(rows // tm, MEM_HEADS),
        in_specs=[pl.BlockSpec((tm, LANES), lambda i, h: (i, h)),
                  pl.BlockSpec((1, LANES), lambda i, h: (0, 0))],
        out_specs=pl.BlockSpec((tm, LANES), lambda i, h: (i, h)),
        out_shape=jax.ShapeDtypeStruct((rows, MEM_HEADS * MEM_HD), F32),
        compiler_params=_params("parallel", "arbitrary"),
        name="mem_k_norm",
    )(kv, gk.reshape(1, LANES))


def _split3(x):
    p1 = x.astype(BF16)
    r1 = x - p1.astype(F32)
    p2 = r1.astype(BF16)
    p3 = (r1 - p2.astype(F32)).astype(BF16)
    return p1, p2, p3


def _hgrn_kernel(*refs, chunk, sub, valid, layer, has_state):
    if has_state:
        (q_ref, f_ref, v_ref, gate_ref, lb_ref, g_ref, tri_ref, s0_ref,
         o_ref, s_out_ref, st_ref, b_scr, kk_scr) = refs
    else:
        (q_ref, f_ref, v_ref, gate_ref, lb_ref, g_ref, tri_ref,
         o_ref, s_out_ref, st_ref, b_scr, kk_scr) = refs
    tl = q_ref.shape[0]
    heads = st_ref.shape[0]
    t = pl.program_id(2)

    @pl.when(t == 0)
    def _():
        for hh in range(heads):
            if has_state:
                st_ref[hh] = s0_ref[hh].T
            else:
                st_ref[hh] = jnp.zeros(st_ref.shape[1:], F32)

    lb = lb_ref[...]
    e = jnp.exp(lb - jnp.max(lb, axis=0, keepdims=True))
    sm = e / jnp.sum(e, axis=0, keepdims=True)
    lower = jnp.zeros((1, sm.shape[1]), F32)
    for r in range(1, layer + 1):
        lower = lower + sm[r:r + 1]

    g = g_ref[...]
    n_chunks = tl // chunk
    n_sub = chunk // sub
    chunk_row = lax.broadcasted_iota(jnp.int32, (chunk, 1), 0)

    q_all = q_ref[...]
    v_all = v_ref[...]
    f_all = lower + (1.0 - lower) * jax.nn.sigmoid(f_ref[...])
    kk_all = 1.0 - f_all
    tri = tri_ref[...]
    p1, p2, p3 = _split3(jnp.log(f_all))
    b_all = (jnp.dot(tri, p1, preferred_element_type=F32)
             + jnp.dot(tri, p2, preferred_element_type=F32)
             + jnp.dot(tri, p3, preferred_element_type=F32))
    b_all = b_all * LOG2_E
    v16_all = v_all.astype(BF16)

    units = [(hh, ci) for hh in range(heads) for ci in range(n_chunks)]

    def unit_block(x, unit):
        hh, ci = unit
        return x[ci * chunk:(ci + 1) * chunk, hh * LANES:(hh + 1) * LANES]

    incs, atts, b_lasts = [], [], []
    for u in units:
        q, b, kk = unit_block(q_all, u), unit_block(b_all, u), unit_block(kk_all, u)
        b_last = b[valid - 1:valid]
        k_dec = jnp.where(chunk_row < valid, kk * jnp.exp2(b_last - b), 0.0)
        incs.append(lax.dot_general(unit_block(v16_all, u), k_dec.astype(BF16), TN_DIMS,
                                    preferred_element_type=F32))
        b_lasts.append(b_last)
        for i in range(1, n_sub):
            rs = slice(i * sub, (i + 1) * sub)
            b_ref = b[i * sub - 1:i * sub]
            q_dec = (q[rs] * jnp.exp2(b[rs] - b_ref)).astype(BF16)
            k_dec = (kk[:i * sub] * jnp.exp2(b_ref - b[:i * sub])).astype(BF16)
            atts.append(lax.dot_general(q_dec, k_dec, NT_DIMS, preferred_element_type=F32))

    b_scr[...] = b_all
    kk_scr[...] = kk_all
    causal = [jnp.where(lax.broadcasted_iota(jnp.int32, (sub, LANES), 0) >= s, 0.0, -jnp.inf)
              for s in range(sub)]
    diag = []
    for u in units:
        hh, ci = u
        cols = slice(hh * LANES, (hh + 1) * LANES)
        q, b = unit_block(q_all, u), unit_block(b_all, u)
        for i in range(n_sub):
            rs = slice(i * sub, (i + 1) * sub)
            qb, bb = q[rs], b[rs]
            terms = []
            for s in range(sub):
                row = ci * chunk + i * sub + s
                d = (bb - b_scr[row:row + 1, cols]) + causal[s]
                w = jnp.sum(qb * kk_scr[row:row + 1, cols] * jnp.exp2(d), axis=-1, keepdims=True)
                terms.append(w * v_ref[row:row + 1, cols])
            while len(terms) > 1:
                terms = [a + c for a, c in zip(terms[0::2], terms[1::2])]
            diag.append(terms[0])

    intra = []
    for n, u in enumerate(units):
        v16 = unit_block(v16_all, u)
        for i in range(n_sub):
            o_i = diag[n * n_sub + i]
            if i > 0:
                att = atts[n * (n_sub - 1) + i - 1]
                o_i = o_i + jnp.dot(att.astype(BF16), v16[:i * sub], preferred_element_type=F32)
            intra.append(o_i)

    states = [st_ref[hh] for hh in range(heads)]
    for n, u in enumerate(units):
        hh, ci = u
        q, b = unit_block(q_all, u), unit_block(b_all, u)
        o_inter = lax.dot_general((q * jnp.exp2(b)).astype(BF16), states[hh].astype(BF16),
                                  NT_DIMS, preferred_element_type=F32)
        states[hh] = states[hh] * jnp.exp2(b_lasts[n]) + incs[n]
        parts = intra[n * n_sub:(n + 1) * n_sub]
        o = o_inter + (parts[0] if n_sub == 1 else jnp.concatenate(parts, axis=0))
        rows, cols = slice(ci * chunk, (ci + 1) * chunk), slice(hh * LANES, (hh + 1) * LANES)
        gate = gate_ref[rows, cols]
        o_ref[rows, cols] = (_rmsnorm_lanes(o, g)
                             * (gate * jax.nn.sigmoid(gate))).astype(o_ref.dtype)
    for hh in range(heads):
        st_ref[hh] = states[hh]

    @pl.when(t == pl.num_programs(2) - 1)
    def _():
        for hh in range(heads):
            s_out_ref[hh] = states[hh].T


def _tri(rows, chunk):
    r = jnp.arange(rows)
    same = (r[:, None] // chunk) == (r[None, :] // chunk)
    return (same & (r[:, None] >= r[None, :])).astype(BF16)


def _hgrn_prompt(proj, lb_logits, g_out, bp, seq, layer):
    chunk = math.gcd(seq, B_CHUNK)
    sub = math.gcd(chunk, B_SUB)
    tl = _pick_tile(seq, 256, chunk)
    nt = seq // tl
    h_ = B_HEADS
    hps = HGRN_PROMPT_HEADS_PER_STEP
    assert h_ % hps == 0
    wide = hps * LANES

    def col(block0):
        return pl.BlockSpec((tl, wide), lambda b, h, t: (b * nt + t, block0 // hps + h))

    return pl.pallas_call(
        functools.partial(_hgrn_kernel, chunk=chunk, sub=sub, valid=chunk, layer=layer,
                          has_state=False),
        grid=(bp, h_ // hps, nt),
        in_specs=[col(0), col(h_), col(2 * h_), col(3 * h_),
                  pl.BlockSpec((lb_logits.shape[0], wide), lambda b, h, t: (0, h)),
                  pl.BlockSpec((1, LANES), lambda b, h, t: (0, 0)),
                  pl.BlockSpec((tl, tl), lambda b, h, t: (0, 0))],
        out_specs=[pl.BlockSpec((tl, wide), lambda b, h, t: (b * nt + t, h)),
                   pl.BlockSpec((None, hps, B_DK, B_DV), lambda b, h, t: (b, h, 0, 0))],
        out_shape=[jax.ShapeDtypeStruct((bp * seq, h_ * B_DV), BF16),
                   jax.ShapeDtypeStruct((bp, h_, B_DK, B_DV), F32)],
        scratch_shapes=[pltpu.VMEM((hps, B_DV, B_DK), F32),
                        pltpu.VMEM((tl, wide), F32), pltpu.VMEM((tl, wide), F32)],
        compiler_params=_params("parallel", "parallel", "arbitrary"),
        name="hgrn_prompt",
    )(proj, proj, proj, proj, lb_logits, g_out.reshape(1, LANES), _tri(tl, chunk))


def _hgrn_sample(proj_s, state, lb_logits, g_out, layer, state_layer):
    bs, ls, n = proj_s.shape
    chunk = 8
    assert ls <= chunk
    padded = jnp.pad(proj_s, ((0, 0), (0, chunk - ls), (0, 0)))
    h_ = B_HEADS
    hps = HGRN_SAMPLE_HEADS_PER_STEP
    assert h_ % hps == 0
    wide = hps * LANES

    def col(block0):
        return pl.BlockSpec((None, chunk, wide), lambda b, h, t: (b, 0, block0 // hps + h))

    o, s_new = pl.pallas_call(
        functools.partial(_hgrn_kernel, chunk=chunk, sub=chunk, valid=ls, layer=layer,
                          has_state=True),
        grid=(bs, h_ // hps, 1),
        in_specs=[col(0), col(h_), col(2 * h_), col(3 * h_),
                  pl.BlockSpec((lb_logits.shape[0], wide), lambda b, h, t: (0, h)),
                  pl.BlockSpec((1, LANES), lambda b, h, t: (0, 0)),
                  pl.BlockSpec((chunk, chunk), lambda b, h, t: (0, 0)),
                  pl.BlockSpec((None, None, hps, B_DK, B_DV),
                               lambda b, h, t: (state_layer, b, h, 0, 0))],
        out_specs=[pl.BlockSpec((None, chunk, wide), lambda b, h, t: (b, 0, h)),
                   pl.BlockSpec((None, hps, B_DK, B_DV), lambda b, h, t: (b, h, 0, 0))],
        out_shape=[jax.ShapeDtypeStruct((bs, chunk, h_ * B_DV), F32),
                   jax.ShapeDtypeStruct((bs, h_, B_DK, B_DV), F32)],
        scratch_shapes=[pltpu.VMEM((hps, B_DV, B_DK), F32),
                        pltpu.VMEM((chunk, wide), F32), pltpu.VMEM((chunk, wide), F32)],
        compiler_params=_params("parallel", "parallel", "arbitrary"),
        name="hgrn_sample",
    )(padded, padded, padded, padded, lb_logits, g_out.reshape(1, LANES), _tri(chunk, chunk), state)
    return o[:, :ls], s_new


def _rope_tables(pos):
    half = A_DK // 2
    inv_freq = ROPE_THETA ** (-jnp.arange(half, dtype=F32) / half)
    ang = pos.astype(F32)[:, None] * inv_freq[None, :]
    cos, sin = jnp.cos(ang), jnp.sin(ang)
    reps = LANES // A_DK
    return (jnp.tile(cos, (1, 2 * reps)), jnp.tile(jnp.concatenate([-sin, sin], axis=1), (1, reps)))


def kernel(x_prompt, x_sample, cache_attn_k, cache_attn_v, state_hgrn, cache_mem_k, cache_mem_v,
           page_table, mem_prompt, norm_ffn, w_ffn_gate, w_ffn_up, w_ffn_down, norm_mix, norm_mem,
           w_mem_kv, gq_mem, gk_mem, w_out, w_in_attn, gq_attn, gk_attn, lam_q1, lam_k1, lam_q2,
           lam_k2, g_subln, w_in_hgrn, lb_logits, g_hgrn_out):
    bp, seq, d = x_prompt.shape
    bs, ls, _ = x_sample.shape
    depth = norm_mix.shape[0]
    mem = mem_prompt.shape[1]
    mp = bp * seq
    ms = bs * ls
    past_len = page_table.shape[1] * PAGE_SIZE
    mem_w = MEM_HEADS * MEM_HD
    qk_w = A_HEADS * 2 * A_DK
    v_w = A_HEADS * A_DV

    xp = x_prompt.reshape(mp, d)
    xs = x_sample.reshape(ms, d)
    cos_p, sin_p = _rope_tables(jnp.arange(seq))
    cos_s, sin_s = _rope_tables(jnp.tile(past_len + jnp.arange(ls), bs))
    mem_rows = mem_prompt.reshape(bp * mem, d)
    cmk = cache_mem_k.reshape(depth, bs, mem * MEM_HEADS, MEM_HD)
    cmv = cache_mem_v.reshape(depth, bs, mem * MEM_HEADS, MEM_HD)

    k_rows_p, v_rows_p, k_rows_s, v_rows_s = [], [], [], []
    st_p, st_s, mem_k_new, mem_v_new = [], [], [], []
    for i in range(depth):
        xp, xs = _ffn_half(xp, xs, norm_ffn, w_ffn_gate, w_ffn_up, w_ffn_down, i, 0)

        kv = _norm_matmul(mem_rows, None, norm_mem, i, w_mem_kv, i)
        k_norm = _mem_k_norm(kv, gk_mem[i])
        mem_k_new.append(k_norm.reshape(bp, mem, MEM_HEADS, MEM_HD))
        mem_v_new.append(kv[:, mem_w:].reshape(bp, mem, MEM_HEADS, MEM_HD))

        if i % 2 == 0:
            a = i // 2
            lam_init = 0.8 - 0.6 * math.exp(-0.3 * i)
            proj, proj_s2 = _norm_matmul(xp, xs, norm_mix, i, w_in_attn, a)
            q_scale = A_DK ** -0.5 * LOG2_E
            q_p, k_hm, v_hm = _qkv_prep(proj, 0, A_HEADS, 2 * A_HEADS, cos_p, sin_p,
                                        gq_attn[a], gk_attn[a], q_scale, bp, seq)
            q_s = _rot_norm_rows(proj_s2, 0, cos_s, sin_s, gq_attn[a], q_scale, BF16)
            k_s = _rot_norm_rows(proj_s2, A_HEADS, cos_s, sin_s, gk_attn[a], 1.0, F32)
            lamv = jnp.stack([lam_q1[a], lam_k1[a], lam_q2[a], lam_k2[a]])
            o_p = _diff_attn_prompt(q_p, k_hm, v_hm, lamv, g_subln[a], bp, seq, lam_init)
            k_rows_p.append(jnp.transpose(k_hm, (0, 2, 1, 3)))
            v_rows_p.append(jnp.transpose(v_hm, (0, 2, 1, 3)))
            proj_s = proj_s2.reshape(bs, ls, proj_s2.shape[1])
            q_s = q_s.reshape(bs, ls, qk_w)
            k_s = k_s.reshape(bs, ls, qk_w)
            v_s = proj_s[..., 2 * qk_w:2 * qk_w + v_w]
            o_s = _diff_attn_decode(q_s, k_s, v_s, cache_attn_k, cache_attn_v, page_table, lamv,
                                    g_subln[a], a, lam_init)
            k_rows_s.append(k_s.reshape(bs, ls, A_HEADS, 2 * A_DK))
            v_rows_s.append(v_s.reshape(bs, ls, A_HEADS, A_DV))
            mq_block0 = (2 * qk_w + v_w) // LANES
        else:
            j = i // 2
            proj, proj_s2 = _norm_matmul(xp, xs, norm_mix, i, w_in_hgrn, j)
            proj_s = proj_s2.reshape(bs, ls, proj_s2.shape[1])
            o_p, s_p = _hgrn_prompt(proj, lb_logits, g_hgrn_out[j], bp, seq, i)
            o_s, s_s = _hgrn_sample(proj_s, state_hgrn, lb_logits, g_hgrn_out[j], i, j)
            st_p.append(s_p)
            st_s.append(s_s)
            mq_block0 = (2 * B_HEADS * B_DK + 2 * B_HEADS * B_DV) // LANES

        m_p = _mem_attn_prompt(proj, mq_block0, k_norm, kv, gq_mem[i], bp, seq)
        m_s = _mem_attn_sample(proj_s, mq_block0, cmk, cmv, gq_mem[i], i)
        xp, xs = _out_proj(xp, xs, o_p, o_s.reshape(ms, -1), m_p, m_s.reshape(ms, -1), w_out, i)

        xp, xs = _ffn_half(xp, xs, norm_ffn, w_ffn_gate, w_ffn_up, w_ffn_down, i, 1)

    return (xp.reshape(bp, seq, d), xs.reshape(bs, ls, d),
            jnp.stack(k_rows_p), jnp.stack(v_rows_p), jnp.stack(k_rows_s), jnp.stack(v_rows_s),
            jnp.stack(st_p), jnp.stack(st_s), jnp.stack(mem_k_new), jnp.stack(mem_v_new))
```
